```python
import jax, jax.numpy as jnp
from jax import lax
import numpy as np


D_MODEL = 2048
BATCH = 4
SEQ = 2048
DEPTH = 2

CHUNK = 64
N_META = 16
N_A_LAYERS = max(1, DEPTH // 2)
N_B_LAYERS = DEPTH - N_A_LAYERS
RW_HEAD = 64
RW_HEADS = D_MODEL // RW_HEAD
DECAY_LORA = 96
AAA_LORA = 96
GATE_LORA = 256
RW_GN_EPS = RW_HEAD * 1e-5
FX_HEAD = 128
FX_HEADS = D_MODEL // FX_HEAD
Q_BLOCK = 128
D_FF = 5632
LN_EPS = 1e-5
ALPHA = (2 * DEPTH) ** 0.25
BETA = (8 * DEPTH) ** -0.25
NEG_INF = -1e30

kernel_name = 'hybrid_rwkv7_fox_yoco_encoder'


def layer_norm(x, g, b):
    xf = x.astype(jnp.float32)
    mu = jnp.mean(xf, axis=-1, keepdims=True)
    var = jnp.mean(jnp.square(xf - mu), axis=-1, keepdims=True)
    y = (xf - mu) * lax.rsqrt(var + LN_EPS)
    return (y * g + b).astype(x.dtype)


def swiglu(x, w1, w3, w2):
    return (jax.nn.silu(x @ w1) * (x @ w3)) @ w2


def wkv7_scan(r, decay, k, v, kk, a):
    B, L, H, N = r.shape
    xs = tuple(jnp.moveaxis(t, 1, 0) for t in (r, decay, k, v, kk, a))

    def step(S, inp):
        r_t, w_t, k_t, v_t, kk_t, a_t = inp
        sa = jnp.einsum('bhvk,bhk->bhv', S, -kk_t)
        S = (S * w_t[:, :, None, :]
             + sa[..., None] * (kk_t * a_t)[:, :, None, :]
             + v_t[..., None] * k_t[:, :, None, :])
        y = jnp.einsum('bhvk,bhk->bhv', S, r_t)
        return S, y

    S0 = jnp.zeros((B, H, N, N), jnp.float32)
    _, ys = lax.scan(step, S0, xs)
    return jnp.moveaxis(ys, 0, 1)


def rwkv7_mix(x, mu, w_rkv, w_o, w0, w1, w2, a0, a1, a2, g1, g2, k_k, k_a, r_k, lnx_g, lnx_b):
    B, L, D = x.shape
    xx = jnp.pad(x, ((0, 0), (1, 0), (0, 0)))[:, :-1] - x
    xr, xw, xk, xv, xa, xg = [x + xx * mu[i] for i in range(6)]
    rkv = jnp.einsum('nbld,nde->nble', jnp.stack([xr, xk, xv]), w_rkv)
    r, k, v = rkv[0], rkv[1], rkv[2]
    w = -jax.nn.softplus(-(w0 + jnp.tanh(xw @ w1) @ w2)) - 0.5
    a = jax.nn.sigmoid(a0 + (xa @ a1) @ a2)
    g = jax.nn.sigmoid(xg @ g1) @ g2
    hs = (B, L, RW_HEADS, RW_HEAD)
    f32 = jnp.float32
    kk = (k * k_k).astype(f32).reshape(hs)
    kk = kk / jnp.maximum(jnp.sqrt(jnp.sum(kk * kk, axis=-1, keepdims=True)), 1e-12)
    k = (k * (1.0 + (a - 1.0) * k_a)).astype(f32).reshape(hs)
    r = r.astype(f32).reshape(hs)
    v = v.astype(f32).reshape(hs)
    a = a.astype(f32).reshape(hs)
    decay = jnp.exp(-jnp.exp(w.astype(f32))).reshape(hs)
    y = wkv7_scan(r, decay, k, v, kk, a)
    mu_y = jnp.mean(y, axis=-1, keepdims=True)
    var_y = jnp.mean(jnp.square(y - mu_y), axis=-1, keepdims=True)
    yn = ((y - mu_y) * lax.rsqrt(var_y + RW_GN_EPS)).reshape(B, L, D) * lnx_g + lnx_b
    bonus = (jnp.sum(r * k * r_k, axis=-1, keepdims=True) * v).reshape(B, L, D)
    out = ((yn + bonus) * g).astype(x.dtype)
    return out @ w_o


def shared_kv(h, w_kvf, b_f):
    B, L, D = h.shape
    z = h @ w_kvf
    k = z[..., :D].reshape(B, L, FX_HEADS, FX_HEAD)
    v = z[..., D:2 * D].reshape(B, L, FX_HEADS, FX_HEAD)
    log_f = jax.nn.log_sigmoid(z[..., 2 * D:].astype(jnp.float32) + b_f)
    c = jnp.transpose(jnp.cumsum(log_f, axis=1), (0, 2, 1))
    return k, v, c


def fox_attention(x, w_q, w_o, k, v, c):
    B, L, D = x.shape
    q = (x @ w_q).reshape(B, L, FX_HEADS, FX_HEAD)
    scale = FX_HEAD ** -0.5
    outs = []
    for qs in range(0, L, Q_BLOCK):
        qe = min(qs + Q_BLOCK, L)
        s = jnp.einsum('bthd,bshd->bhts', q[:, qs:qe], k[:, :qe]).astype(jnp.float32) * scale
        s = s + c[:, :, qs:qe, None] - c[:, :, None, :qe]
        causal = jnp.arange(qe)[None, :] <= jnp.arange(qs, qe)[:, None]
        s = jnp.where(causal[None, None], s, NEG_INF)
        p = jax.nn.softmax(s, axis=-1).astype(v.dtype)
        outs.append(jnp.einsum('bhts,bshd->bthd', p, v[:, :qe]))
    o = jnp.concatenate(outs, axis=1).reshape(B, L, D)
    return o @ w_o


def setup_inputs(seed: int = 0) -> dict:
    key = jax.random.key(seed)
    ks = jax.random.split(key, 28)
    D, F, NA, NB = D_MODEL, D_FF, N_A_LAYERS, N_B_LAYERS

    def nrm(k, shape, scale):
        return jax.random.normal(k, shape, jnp.float32) * scale

    return {
        'x': nrm(ks[0], (BATCH, SEQ, D), 1.0),
        'meta_tokens': nrm(ks[1], (N_META, D), 1.0),
        'ln_g': 1.0 + nrm(ks[2], (DEPTH, 3, D), 0.02),
        'ln_b': nrm(ks[3], (DEPTH, 3, D), 0.02),
        'ffn_w1': nrm(ks[4], (DEPTH, 2, D, F), D ** -0.5),
        'ffn_w3': nrm(ks[5], (DEPTH, 2, D, F), D ** -0.5),
        'ffn_w2': nrm(ks[6], (DEPTH, 2, F, D), BETA * F ** -0.5),
        'rw_mu': jax.random.uniform(ks[7], (NA, 6, D), jnp.float32),
        'rw_w_rkv': nrm(ks[8], (NA, 3, D, D), D ** -0.5),
        'rw_w_o': nrm(ks[9], (NA, D, D), BETA * D ** -0.5),
        'rw_w0': jax.random.uniform(ks[10], (NA, D), jnp.float32, minval=-6.5, maxval=-1.5),
        'rw_w1': nrm(ks[11], (NA, D, DECAY_LORA), D ** -0.5),
        'rw_w2': nrm(ks[12], (NA, DECAY_LORA, D), 0.1 * DECAY_LORA ** -0.5),
        'rw_a0': nrm(ks[13], (NA, D), 0.1),
        'rw_a1': nrm(ks[14], (NA, D, AAA_LORA), D ** -0.5),
        'rw_a2': nrm(ks[15], (NA, AAA_LORA, D), AAA_LORA ** -0.5),
        'rw_g1': nrm(ks[16], (NA, D, GATE_LORA), D ** -0.5),
        'rw_g2': nrm(ks[17], (NA, GATE_LORA, D), GATE_LORA ** -0.5),
        'rw_k_k': 0.85 + nrm(ks[18], (NA, D), 0.02),
        'rw_k_a': 1.0 + nrm(ks[19], (NA, D), 0.02),
        'rw_r_k': nrm(ks[20], (NA, RW_HEADS, RW_HEAD), 0.1),
        'rw_lnx_g': 1.0 + nrm(ks[21], (NA, D), 0.02),
        'rw_lnx_b': nrm(ks[22], (NA, D), 0.02),
        'fx_w_q': nrm(ks[23], (NB, D, D), D ** -0.5),
        'fx_w_o': nrm(ks[24], (NB, D, D), BETA * D ** -0.5),
        'fx_w_kvf': nrm(ks[25], (D, 2 * D + FX_HEADS), D ** -0.5),
        'fx_b_f': jax.random.uniform(ks[26], (FX_HEADS,), jnp.float32, minval=1.0, maxval=4.0),
    }


def reference(x, meta_tokens, ln_g, ln_b, ffn_w1, ffn_w3, ffn_w2,
              rw_mu, rw_w_rkv, rw_w_o, rw_w0, rw_w1, rw_w2, rw_a0, rw_a1, rw_a2,
              rw_g1, rw_g2, rw_k_k, rw_k_a, rw_r_k, rw_lnx_g, rw_lnx_b,
              fx_w_q, fx_w_o, fx_w_kvf, fx_b_f):
    B = x.shape[0]
    meta = jnp.broadcast_to(meta_tokens.astype(x.dtype)[None], (B, N_META, x.shape[-1]))
    h = jnp.concatenate([meta, x], axis=1)
    k_s = v_s = c_s = None
    for l in range(DEPTH):
        h = layer_norm(ALPHA * h + 0.5 * swiglu(h, ffn_w1[l, 0], ffn_w3[l, 0], ffn_w2[l, 0]),
                       ln_g[l, 0], ln_b[l, 0])
        if l < N_A_LAYERS:
            mix = rwkv7_mix(h, rw_mu[l], rw_w_rkv[l], rw_w_o[l], rw_w0[l], rw_w1[l], rw_w2[l],
                            rw_a0[l], rw_a1[l], rw_a2[l], rw_g1[l], rw_g2[l],
                            rw_k_k[l], rw_k_a[l], rw_r_k[l], rw_lnx_g[l], rw_lnx_b[l])
        else:
            j = l - N_A_LAYERS
            mix = fox_attention(h, fx_w_q[j], fx_w_o[j], k_s, v_s, c_s)
        h = layer_norm(ALPHA * h + mix, ln_g[l, 1], ln_b[l, 1])
        h = layer_norm(ALPHA * h + 0.5 * swiglu(h, ffn_w1[l, 1], ffn_w3[l, 1], ffn_w2[l, 1]),
                       ln_g[l, 2], ln_b[l, 2])
        if l == N_A_LAYERS - 1:
            k_s, v_s, c_s = shared_kv(h, fx_w_kvf, fx_b_f)
    return h[:, N_META:]
```

```python
import functools

import jax
import jax.numpy as jnp
from jax import lax
from jax.experimental import pallas as pl
from jax.experimental.pallas import tpu as pltpu

N_META = 16
RW_HEAD = 64
FX_HEAD = 128
LN_EPS = 1e-5
RW_GN_EPS = RW_HEAD * 1e-5
NEG_INF = -1e30

LANES = 128
SEQ_ALIGN = 128
WKV_CHUNK = 64
WKV_PAIRS_PER_STEP = 4
VMEM_LIMIT = 56 * 1024 * 1024

F32 = jnp.float32
BF16 = jnp.bfloat16


def _cparams(sem):
    return pltpu.CompilerParams(dimension_semantics=sem, vmem_limit_bytes=VMEM_LIMIT)


def _dot(a, b):
    return jnp.dot(a, b, preferred_element_type=F32)


def _dot_nt(a, b):
    return lax.dot_general(a, b, (((1,), (1,)), ((), ())), preferred_element_type=F32)


def _split(x):
    hi = x.astype(BF16)
    lo = (x - hi.astype(F32)).astype(BF16)
    return hi, lo


def _softplus(x):
    return jnp.maximum(x, 0.0) + jnp.log1p(jnp.exp(-jnp.abs(x)))


def _sigmoid(x):
    return 1.0 / (1.0 + jnp.exp(-x))


def _layer_norm(y, g, b):
    mu = jnp.mean(y, axis=-1, keepdims=True)
    yc = y - mu
    var = jnp.mean(yc * yc, axis=-1, keepdims=True)
    return yc * lax.rsqrt(var + LN_EPS) * g + b


def _mm_kernel(x_ref, w_ref, b_ref, o_ref, *, act):
    acc = _dot(x_ref[...], w_ref[...]) + b_ref[...]
    if act == "tanh":
        acc = jnp.tanh(acc)
    elif act == "sigmoid":
        acc = _sigmoid(acc)
    elif act == "decay":
        acc = -_softplus(-acc) - 0.5
    o_ref[...] = acc.astype(o_ref.dtype)


def _matmul(x, w, bias=None, act=None, out_dtype=F32, tm=None, tn=None):
    t, k = x.shape
    n = w.shape[1]
    tm = tm or _pick_tm(t)
    tn = tn or min(n, 512)
    if bias is None:
        bias = jnp.zeros((1, n), F32)
    return pl.pallas_call(
        functools.partial(_mm_kernel, act=act),
        grid=(t // tm, n // tn),
        in_specs=[pl.BlockSpec((tm, k), lambda i, j: (i, 0)),
                  pl.BlockSpec((k, tn), lambda i, j: (0, j)),
                  pl.BlockSpec((1, tn), lambda i, j: (0, j))],
        out_specs=pl.BlockSpec((tm, tn), lambda i, j: (i, j)),
        out_shape=jax.ShapeDtypeStruct((t, n), out_dtype),
        compiler_params=_cparams(("parallel", "parallel")),
    )(x, w, bias.reshape(1, n).astype(F32))


def _pick_tm(t, cap=1088):
    for tm in range(min(cap, t), 0, -16):
        if t % tm == 0:
            return tm
    return t


def _bmm_kernel(x_ref, w_ref, o_ref):
    o_ref[0] = _dot(x_ref[0], w_ref[0]).astype(o_ref.dtype)


def _bmm(x, w, out_dtype=F32, tn=512):
    nb, t, k = x.shape
    n = w.shape[2]
    tm = _pick_tm(t)
    return pl.pallas_call(
        _bmm_kernel,
        grid=(nb, t // tm, n // tn),
        in_specs=[pl.BlockSpec((1, tm, k), lambda b, i, j: (b, i, 0)),
                  pl.BlockSpec((1, k, tn), lambda b, i, j: (b, 0, j))],
        out_specs=pl.BlockSpec((1, tm, tn), lambda b, i, j: (b, i, j)),
        out_shape=jax.ShapeDtypeStruct((nb, t, n), out_dtype),
        compiler_params=_cparams(("parallel", "parallel", "parallel")),
    )(x, w)


def _mm_ln_kernel(x_ref, w_ref, h_ref, g_ref, b_ref, o_ref, *, alpha):
    y = alpha * h_ref[...] + _dot(x_ref[...], w_ref[...])
    o_ref[...] = _layer_norm(y, g_ref[...], b_ref[...])


def _matmul_ln(x, w, h, g, b, alpha, tm=None):
    t, k = x.shape
    n = w.shape[1]
    tm = tm or _pick_tm(t, 512)
    return pl.pallas_call(
        functools.partial(_mm_ln_kernel, alpha=alpha),
        grid=(t // tm,),
        in_specs=[pl.BlockSpec((tm, k), lambda i: (i, 0)),
                  pl.BlockSpec((k, n), lambda i: (0, 0)),
                  pl.BlockSpec((tm, n), lambda i: (i, 0)),
                  pl.BlockSpec((1, n), lambda i: (0, 0)),
                  pl.BlockSpec((1, n), lambda i: (0, 0))],
        out_specs=pl.BlockSpec((tm, n), lambda i: (i, 0)),
        out_shape=jax.ShapeDtypeStruct((t, n), F32),
        compiler_params=_cparams(("parallel",)),
    )(x, w, h, g.reshape(1, n), b.reshape(1, n))


def _ffn_ln_kernel(h_ref, w1_ref, w3_ref, w2_ref, g_ref, b_ref, o_ref, hb_ref, *, alpha):
    j = pl.program_id(1)

    @pl.when(j == 0)
    def _():
        hb_ref[...] = h_ref[...].astype(BF16)

    hb = hb_ref[...]
    u = _dot(hb, w1_ref[...])
    v = _dot(hb, w3_ref[...])
    act = (u * _sigmoid(u) * v).astype(BF16)
    part = _dot(act, w2_ref[...])

    @pl.when(j == 0)
    def _():
        o_ref[...] = part

    @pl.when(j > 0)
    def _():
        o_ref[...] += part

    @pl.when(j == pl.num_programs(1) - 1)
    def _():
        y = alpha * h_ref[...] + 0.5 * o_ref[...]
        o_ref[...] = _layer_norm(y, g_ref[...], b_ref[...])


def _ffn_ln(h, w1, w3, w2, g, b, alpha, tm=None, tf=512):
    t, d = h.shape
    f = w1.shape[1]
    tm = tm or _pick_tm(t, 544)
    return pl.pallas_call(
        functools.partial(_ffn_ln_kernel, alpha=alpha),
        grid=(t // tm, f // tf),
        in_specs=[pl.BlockSpec((tm, d), lambda i, j: (i, 0)),
                  pl.BlockSpec((d, tf), lambda i, j: (0, j)),
                  pl.BlockSpec((d, tf), lambda i, j: (0, j)),
                  pl.BlockSpec((tf, d), lambda i, j: (j, 0)),
                  pl.BlockSpec((1, d), lambda i, j: (0, 0)),
                  pl.BlockSpec((1, d), lambda i, j: (0, 0))],
        out_specs=pl.BlockSpec((tm, d), lambda i, j: (i, 0)),
        out_shape=jax.ShapeDtypeStruct((t, d), F32),
        scratch_shapes=[pltpu.VMEM((tm, d), BF16)],
        compiler_params=_cparams(("parallel", "arbitrary")),
    )(h, w1, w3, w2, g.reshape(1, d), b.reshape(1, d))


def _cumsum_rows(x):
    n = x.shape[0]
    row = lax.broadcasted_iota(jnp.int32, x.shape, 0)
    s = 1
    while s < n:
        x = x + jnp.where(row >= s, pltpu.roll(x, s, 0), 0.0)
        s *= 2
    return x


def _mm1(a, b):
    return _dot(a.astype(BF16), b.astype(BF16))


def _mm3(a, b):
    ah, al = _split(a)
    bh, bl = _split(b)
    return _dot(ah, bh) + (_dot(ah, bl) + _dot(al, bh))


def _wkv_pair(r, wl, k, v, a, g, kkp, kap, rkp, lgp, lbp, s0, c):
    cc = r.shape[0]
    m0, strict, incl, ones_bd, eye = c

    def stack2(x):
        return jnp.concatenate([jnp.where(m0, x, 0.0), jnp.where(m0, 0.0, x)], axis=0)

    def segsum(x):
        hi, lo = _split(x)
        return _dot(hi, ones_bd) + _dot(lo, ones_bd)

    kk0 = k * kkp
    kk = kk0 / jnp.maximum(jnp.sqrt(segsum(kk0 * kk0)), 1e-12)
    kmod = k * (1.0 + (a - 1.0) * kap)
    b = kk * a
    lw = -jnp.exp(wl)
    lcum = _cumsum_rows(lw)
    ltot = lcum[cc - 1:cc, :]
    e_neg = jnp.exp(-lcum)
    e_rem = jnp.exp(ltot - lcum)
    kt = kk * jnp.exp(lcum - lw)
    rt = r * jnp.exp(lcum)
    kh = kmod * e_neg
    bh = b * e_neg

    lhs = jnp.concatenate([stack2(kt), stack2(rt)], axis=0).astype(BF16)
    khb = kh.astype(BF16)
    bhb = bh.astype(BF16)
    rhs = jnp.concatenate([khb, khb, bhb, bhb, s0.astype(BF16)], axis=0)
    m = _dot_nt(lhs, rhs)
    c2 = 2 * cc
    a_k = jnp.where(strict, m[:c2, :c2], 0.0)
    a_b = jnp.where(strict, m[:c2, c2:2 * c2], 0.0)
    b_k = jnp.where(incl, m[c2:, :c2], 0.0)
    b_b = jnp.where(incl, m[c2:, c2:2 * c2], 0.0)
    qs = m[:c2, 2 * c2:]
    rs = m[c2:, 2 * c2:]

    vs = stack2(v)
    rhs_u = qs + _mm1(a_k, vs)
    x = eye - a_b
    p = a_b
    n = 2
    while n < cc:
        p = _mm1(p, p)
        x = x + _mm1(x, p)
        n *= 2
    us = _mm1(x, rhs_u)
    ys = rs + _mm1(jnp.concatenate([b_k, -b_b], axis=1), jnp.concatenate([vs, us], axis=0))
    y = ys[:cc] + ys[cc:]

    upd_l = jnp.concatenate([vs, -us], axis=0)
    upd_r = jnp.concatenate([stack2(kmod * e_rem), stack2(b * e_rem)], axis=0)
    s_new = s0 * jnp.exp(ltot) + _mm1(upd_l.T, upd_r)

    inv_n = 1.0 / RW_HEAD
    mu = segsum(y) * inv_n
    yc = y - mu
    var = segsum(yc * yc) * inv_n
    yn = yc * lax.rsqrt(var + RW_GN_EPS) * lgp + lbp
    bonus = segsum(r * kmod * rkp) * v
    return (yn + bonus) * g, s_new


def _wkv_kernel(r_ref, w_ref, k_ref, v_ref, a_ref, g_ref, kk_ref, ka_ref, rk_ref, lg_ref, lb_ref,
                o_ref, s_ref, *, pairs):
    @pl.when(pl.program_id(2) == 0)
    def _():
        s_ref[...] = jnp.zeros_like(s_ref)

    cc = r_ref.shape[0]
    c2 = 2 * cc
    lane = lax.broadcasted_iota(jnp.int32, (cc, LANES), 1)
    m0 = lane < RW_HEAD
    row = lax.broadcasted_iota(jnp.int32, (c2, c2), 0)
    col = lax.broadcasted_iota(jnp.int32, (c2, c2), 1)
    same = (row >= cc) == (col >= cc)
    strict = same & (col < row)
    incl = same & (col <= row)
    lr = lax.broadcasted_iota(jnp.int32, (LANES, LANES), 0)
    lc = lax.broadcasted_iota(jnp.int32, (LANES, LANES), 1)
    ones_bd = jnp.where((lr >= RW_HEAD) == (lc >= RW_HEAD), 1.0, 0.0).astype(BF16)
    eye = jnp.where(row == col, 1.0, 0.0).astype(F32)
    consts = (m0, strict, incl, ones_bd, eye)

    for p in range(pairs):
        sl = slice(p * LANES, (p + 1) * LANES)
        out, s_new = _wkv_pair(r_ref[:, sl], w_ref[:, sl], k_ref[:, sl], v_ref[:, sl], a_ref[:, sl],
                               g_ref[:, sl], kk_ref[:, sl], ka_ref[:, sl], rk_ref[:, sl],
                               lg_ref[:, sl], lb_ref[:, sl], s_ref[p], consts)
        s_ref[p] = s_new
        o_ref[:, sl] = out.astype(o_ref.dtype)


def _wkv(r, wl, k, v, a, g, kkp, kap, rkp, lgp, lbp, batch, out_dtype=BF16):
    t, d = r.shape
    lp = t // batch
    cc = WKV_CHUNK
    nchunk = lp // cc
    pairs = min(WKV_PAIRS_PER_STEP, d // LANES)
    wblk = pairs * LANES
    tok = pl.BlockSpec((cc, wblk), lambda b, p, c: (b * nchunk + c, p))
    par = pl.BlockSpec((1, wblk), lambda b, p, c: (0, p))
    prm = [x.reshape(1, d) for x in (kkp, kap, rkp, lgp, lbp)]
    return pl.pallas_call(
        functools.partial(_wkv_kernel, pairs=pairs),
        grid=(batch, d // wblk, nchunk),
        in_specs=[tok] * 6 + [par] * 5,
        out_specs=tok,
        out_shape=jax.ShapeDtypeStruct((t, d), out_dtype),
        scratch_shapes=[pltpu.VMEM((pairs, LANES, LANES), F32)],
        compiler_params=_cparams(("parallel", "parallel", "arbitrary")),
    )(r, wl, k, v, a, g, *prm)


def _fgate_kernel(h_ref, wh_ref, wl_ref, b_ref, o_ref, carry_ref):
    @pl.when(pl.program_id(1) == 0)
    def _():
        carry_ref[...] = jnp.zeros_like(carry_ref)

    hh, hl = _split(h_ref[...])
    z = _dot(hh, wh_ref[...]) + (_dot(hh, wl_ref[...]) + _dot(hl, wh_ref[...])) + b_ref[...]
    logf = -_softplus(-z)
    cs = _cumsum_rows(logf) + carry_ref[...]
    o_ref[...] = cs
    carry_ref[...] = cs[cs.shape[0] - 1:, :]


def _fgate_cumsum(h, wf, bf, batch, tm=128):
    t, d = h.shape
    lp = t // batch
    nt = lp // tm
    nh = wf.shape[1]
    wpad = jnp.zeros((d, LANES), F32).at[:, :nh].set(wf)
    bpad = jnp.zeros((1, LANES), F32).at[0, :nh].set(bf)
    wh, wl = _split(wpad)
    return pl.pallas_call(
        _fgate_kernel,
        grid=(batch, nt),
        in_specs=[pl.BlockSpec((tm, d), lambda b, i: (b * nt + i, 0)),
                  pl.BlockSpec((d, LANES), lambda b, i: (0, 0)),
                  pl.BlockSpec((d, LANES), lambda b, i: (0, 0)),
                  pl.BlockSpec((1, LANES), lambda b, i: (0, 0))],
        out_specs=pl.BlockSpec((tm, LANES), lambda b, i: (b * nt + i, 0)),
        out_shape=jax.ShapeDtypeStruct((t, LANES), F32),
        scratch_shapes=[pltpu.VMEM((1, LANES), F32)],
        compiler_params=_cparams(("parallel", "arbitrary")),
    )(h, wh, wl, bpad)


def _fox_kernel(q_ref, k_ref, v_ref, cq_ref, ck_ref, o_ref, *, tq, scale):
    qi = pl.program_id(2)
    q = q_ref[...]
    cq = cq_ref[0, 0]
    rpos = qi * tq + lax.broadcasted_iota(jnp.int32, (tq, tq), 0)
    cidx = lax.broadcasted_iota(jnp.int32, (tq, tq), 1)

    def body(j, carry):
        m_i, l_i, acc = carry
        start = pl.multiple_of(j * tq, tq)
        kt = k_ref[pl.ds(start, tq), :]
        vt = v_ref[pl.ds(start, tq), :]
        s = _dot_nt(q, kt) * scale + cq - ck_ref[0, 0, :, pl.ds(start, tq)]
        s = jnp.where(cidx + j * tq <= rpos, s, NEG_INF)
        m_new = jnp.maximum(m_i, jnp.max(s, axis=-1, keepdims=True))
        alpha = jnp.exp(m_i - m_new)
        p = jnp.exp(s - m_new)
        l_new = alpha * l_i + jnp.sum(p, axis=-1, keepdims=True)
        acc = alpha * acc + _dot(p.astype(BF16), vt)
        return m_new, l_new, acc

    init = (jnp.full((tq, 1), NEG_INF, F32), jnp.zeros((tq, 1), F32), jnp.zeros((tq, FX_HEAD), F32))
    _, l_i, acc = lax.fori_loop(0, qi + 1, body, init)
    o_ref[...] = (acc / l_i).astype(o_ref.dtype)


def _fox_attention(q, kv, c_col, c_row, batch, tq=128):
    t, d = q.shape
    lp = t // batch
    nh = d // FX_HEAD
    nq = lp // tq
    return pl.pallas_call(
        functools.partial(_fox_kernel, tq=tq, scale=FX_HEAD ** -0.5),
        grid=(batch, nh, nq),
        in_specs=[pl.BlockSpec((tq, FX_HEAD), lambda b, h, i: (b * nq + i, h)),
                  pl.BlockSpec((lp, FX_HEAD), lambda b, h, i: (b, h)),
                  pl.BlockSpec((lp, FX_HEAD), lambda b, h, i: (b, nh + h)),
                  pl.BlockSpec((1, 1, tq, 1), lambda b, h, i: (b, h, i, 0)),
                  pl.BlockSpec((1, 1, 1, lp), lambda b, h, i: (b, h, 0, 0))],
        out_specs=pl.BlockSpec((tq, FX_HEAD), lambda b, h, i: (b * nq + i, h)),
        out_shape=jax.ShapeDtypeStruct((t, d), BF16),
        compiler_params=_cparams(("parallel", "parallel", "arbitrary")),
    )(q, kv, kv, c_col, c_row)


def _pad_cols(w, n):
    return jnp.zeros((w.shape[0], n), w.dtype).at[:, :w.shape[1]].set(w)


def _pad_rows(w, n):
    return jnp.zeros((n, w.shape[1]), w.dtype).at[:w.shape[0], :].set(w)


def _rwkv7_mix(h, batch, mu, w_rkv, w_o, w0, w1, w2, a0, a1, a2, g1, g2, k_k, k_a, r_k, lnx_g, lnx_b,
               ln_g, ln_b, alpha):
    t, d = h.shape
    lp = t // batch
    h3 = h.reshape(batch, lp, d)
    xx = jnp.pad(h3, ((0, 0), (1, 0), (0, 0)))[:, :-1] - h3
    xs = [(h3 + xx * mu[i]).reshape(t, d).astype(BF16) for i in range(6)]
    xr, xw, xk, xv, xa, xg = xs
    rkv = _bmm(jnp.stack([xr, xk, xv]), w_rkv.astype(BF16))
    lw = _pad_cols(w1, LANES)
    wl = _matmul(_matmul(xw, lw.astype(BF16), act="tanh", out_dtype=BF16, tn=LANES),
                 _pad_rows(w2, LANES).astype(BF16), bias=w0, act="decay")
    la = _pad_cols(a1, LANES)
    a = _matmul(_matmul(xa, la.astype(BF16), out_dtype=BF16, tn=LANES),
                _pad_rows(a2, LANES).astype(BF16), bias=a0, act="sigmoid")
    g = _matmul(_matmul(xg, g1.astype(BF16), act="sigmoid", out_dtype=BF16, tn=g1.shape[1]),
                g2.astype(BF16))
    o = _wkv(rkv[0], wl, rkv[1], rkv[2], a, g, k_k, k_a, r_k.reshape(-1), lnx_g, lnx_b, batch)
    return _matmul_ln(o, w_o.astype(BF16), h, ln_g, ln_b, alpha)


def kernel(x, meta_tokens, ln_g, ln_b, ffn_w1, ffn_w3, ffn_w2, rw_mu, rw_w_rkv, rw_w_o, rw_w0, rw_w1,
           rw_w2, rw_a0, rw_a1, rw_a2, rw_g1, rw_g2, rw_k_k, rw_k_a, rw_r_k, rw_lnx_g, rw_lnx_b,
           fx_w_q, fx_w_o, fx_w_kvf, fx_b_f):
    batch, seq, d = x.shape
    depth = ln_g.shape[0]
    n_a = rw_mu.shape[0]
    alpha = (2 * depth) ** 0.25
    l_real = seq + N_META
    lp = -(-l_real // SEQ_ALIGN) * SEQ_ALIGN
    t = batch * lp
    nh = d // FX_HEAD

    meta = jnp.broadcast_to(meta_tokens.astype(x.dtype)[None], (batch, N_META, d))
    h = jnp.concatenate([meta, x, jnp.zeros((batch, lp - l_real, d), x.dtype)], axis=1).reshape(t, d)

    kv = c_col = c_row = None
    for l in range(depth):
        h = _ffn_ln(h, ffn_w1[l, 0].astype(BF16), ffn_w3[l, 0].astype(BF16), ffn_w2[l, 0].astype(BF16),
                    ln_g[l, 0], ln_b[l, 0], alpha)
        if l < n_a:
            h = _rwkv7_mix(h, batch, rw_mu[l], rw_w_rkv[l], rw_w_o[l], rw_w0[l], rw_w1[l], rw_w2[l],
                           rw_a0[l], rw_a1[l], rw_a2[l], rw_g1[l], rw_g2[l], rw_k_k[l], rw_k_a[l],
                           rw_r_k[l], rw_lnx_g[l], rw_lnx_b[l], ln_g[l, 1], ln_b[l, 1], alpha)
        else:
            j = l - n_a
            q = _matmul(h.astype(BF16), fx_w_q[j].astype(BF16), out_dtype=BF16)
            o = _fox_attention(q, kv, c_col, c_row, batch)
            h = _matmul_ln(o, fx_w_o[j].astype(BF16), h, ln_g[l, 1], ln_b[l, 1], alpha)
        h = _ffn_ln(h, ffn_w1[l, 1].astype(BF16), ffn_w3[l, 1].astype(BF16), ffn_w2[l, 1].astype(BF16),
                    ln_g[l, 2], ln_b[l, 2], alpha)
        if l == n_a - 1:
            kv = _matmul(h.astype(BF16), fx_w_kvf[:, :2 * d].astype(BF16), out_dtype=BF16)
            c = _fgate_cumsum(h, fx_w_kvf[:, 2 * d:], fx_b_f, batch)[:, :nh]
            c = jnp.transpose(c.reshape(batch, lp, nh), (0, 2, 1))
            c_col = c[..., None]
            c_row = c[:, :, None, :]
    return h.reshape(batch, lp, d)[:, N_META:l_real]
```

```python
import functools

import jax
import jax.numpy as jnp
from jax import lax
from jax.experimental import pallas as pl
from jax.experimental.pallas import tpu as pltpu

N_META = 16
RW_HEAD = 64
FX_HEAD = 128
LN_EPS = 1e-5
RW_GN_EPS = RW_HEAD * 1e-5
NEG_INF = -1e30

LANES = 128
SEQ_ALIGN = 128
WKV_CHUNK = 64
WKV_PAIRS_PER_STEP = 8
VMEM_LIMIT = 56 * 1024 * 1024

F32 = jnp.float32
BF16 = jnp.bfloat16


def _cparams(sem):
    return pltpu.CompilerParams(dimension_semantics=sem, vmem_limit_bytes=VMEM_LIMIT)


def _dot(a, b):
    return jnp.dot(a, b, preferred_element_type=F32)


def _dot_nt(a, b):
    return lax.dot_general(a, b, (((1,), (1,)), ((), ())), preferred_element_type=F32)


def _split(x):
    hi = x.astype(BF16)
    lo = (x - hi.astype(F32)).astype(BF16)
    return hi, lo


def _softplus(x):
    return jnp.maximum(x, 0.0) + jnp.log1p(jnp.exp(-jnp.abs(x)))


def _sigmoid(x):
    return 1.0 / (1.0 + jnp.exp(-x))


def _layer_norm(y, g, b):
    mu = jnp.mean(y, axis=-1, keepdims=True)
    yc = y - mu
    var = jnp.mean(yc * yc, axis=-1, keepdims=True)
    return yc * lax.rsqrt(var + LN_EPS) * g + b


def _mm_kernel(x_ref, w_ref, b_ref, o_ref, *, act):
    acc = _dot(x_ref[...], w_ref[...]) + b_ref[...]
    if act == "tanh":
        acc = jnp.tanh(acc)
    elif act == "sigmoid":
        acc = _sigmoid(acc)
    elif act == "decay":
        acc = -_softplus(-acc) - 0.5
    o_ref[...] = acc.astype(o_ref.dtype)


def _matmul(x, w, bias=None, act=None, out_dtype=F32, tm=None, tn=None, name="matmul"):
    t, k = x.shape
    n = w.shape[1]
    tm = tm or _pick_tm(t)
    tn = tn or min(n, 512)
    if bias is None:
        bias = jnp.zeros((1, n), F32)
    return pl.pallas_call(
        functools.partial(_mm_kernel, act=act),
        grid=(t // tm, n // tn),
        in_specs=[pl.BlockSpec((tm, k), lambda i, j: (i, 0)),
                  pl.BlockSpec((k, tn), lambda i, j: (0, j)),
                  pl.BlockSpec((1, tn), lambda i, j: (0, j))],
        out_specs=pl.BlockSpec((tm, tn), lambda i, j: (i, j)),
        out_shape=jax.ShapeDtypeStruct((t, n), out_dtype),
        name=name,
        compiler_params=_cparams(("parallel", "parallel")),
    )(x, w, bias.reshape(1, n).astype(F32))


def _pick_tm(t, cap=1088):
    for tm in range(min(cap, t), 0, -16):
        if t % tm == 0:
            return tm
    return t


def _bmm_kernel(x_ref, w_ref, o_ref):
    o_ref[0] = _dot(x_ref[0], w_ref[0]).astype(o_ref.dtype)


def _bmm(x, w, out_dtype=F32, tn=512):
    nb, t, k = x.shape
    n = w.shape[2]
    tm = _pick_tm(t)
    return pl.pallas_call(
        _bmm_kernel,
        grid=(nb, t // tm, n // tn),
        in_specs=[pl.BlockSpec((1, tm, k), lambda b, i, j: (b, i, 0)),
                  pl.BlockSpec((1, k, tn), lambda b, i, j: (b, 0, j))],
        out_specs=pl.BlockSpec((1, tm, tn), lambda b, i, j: (b, i, j)),
        out_shape=jax.ShapeDtypeStruct((nb, t, n), out_dtype),
        name="rkv_proj",
        compiler_params=_cparams(("parallel", "parallel", "parallel")),
    )(x, w)


def _mm_ln_kernel(x_ref, w_ref, h_ref, g_ref, b_ref, o_ref, *, alpha):
    y = alpha * h_ref[...] + _dot(x_ref[...], w_ref[...])
    o_ref[...] = _layer_norm(y, g_ref[...], b_ref[...])


def _matmul_ln(x, w, h, g, b, alpha, tm=None):
    t, k = x.shape
    n = w.shape[1]
    tm = tm or _pick_tm(t, 512)
    return pl.pallas_call(
        functools.partial(_mm_ln_kernel, alpha=alpha),
        grid=(t // tm,),
        in_specs=[pl.BlockSpec((tm, k), lambda i: (i, 0)),
                  pl.BlockSpec((k, n), lambda i: (0, 0)),
                  pl.BlockSpec((tm, n), lambda i: (i, 0)),
                  pl.BlockSpec((1, n), lambda i: (0, 0)),
                  pl.BlockSpec((1, n), lambda i: (0, 0))],
        out_specs=pl.BlockSpec((tm, n), lambda i: (i, 0)),
        out_shape=jax.ShapeDtypeStruct((t, n), F32),
        name="proj_ln",
        compiler_params=_cparams(("parallel",)),
    )(x, w, h, g.reshape(1, n), b.reshape(1, n))


def _ffn_ln_kernel(h_ref, w1_ref, w3_ref, w2_ref, g_ref, b_ref, o_ref, hb_ref, *, alpha):
    j = pl.program_id(1)

    @pl.when(j == 0)
    def _():
        hb_ref[...] = h_ref[...].astype(BF16)

    hb = hb_ref[...]
    u = _dot(hb, w1_ref[...])
    v = _dot(hb, w3_ref[...])
    act = (u * _sigmoid(u) * v).astype(BF16)
    part = _dot(act, w2_ref[...])

    @pl.when(j == 0)
    def _():
        o_ref[...] = part

    @pl.when(j > 0)
    def _():
        o_ref[...] += part

    @pl.when(j == pl.num_programs(1) - 1)
    def _():
        y = alpha * h_ref[...] + 0.5 * o_ref[...]
        o_ref[...] = _layer_norm(y, g_ref[...], b_ref[...])


def _ffn_ln(h, w1, w3, w2, g, b, alpha, tm=None, tf=512):
    t, d = h.shape
    f = w1.shape[1]
    tm = tm or _pick_tm(t, 544)
    return pl.pallas_call(
        functools.partial(_ffn_ln_kernel, alpha=alpha),
        grid=(t // tm, f // tf),
        in_specs=[pl.BlockSpec((tm, d), lambda i, j: (i, 0)),
                  pl.BlockSpec((d, tf), lambda i, j: (0, j)),
                  pl.BlockSpec((d, tf), lambda i, j: (0, j)),
                  pl.BlockSpec((tf, d), lambda i, j: (j, 0)),
                  pl.BlockSpec((1, d), lambda i, j: (0, 0)),
                  pl.BlockSpec((1, d), lambda i, j: (0, 0))],
        out_specs=pl.BlockSpec((tm, d), lambda i, j: (i, 0)),
        out_shape=jax.ShapeDtypeStruct((t, d), F32),
        scratch_shapes=[pltpu.VMEM((tm, d), BF16)],
        name="ffn_ln",
        compiler_params=_cparams(("parallel", "arbitrary")),
    )(h, w1, w3, w2, g.reshape(1, d), b.reshape(1, d))


def _cumsum_rows(x):
    n = x.shape[0]
    row = lax.broadcasted_iota(jnp.int32, x.shape, 0)
    s = 1
    while s < n:
        x = x + jnp.where(row >= s, pltpu.roll(x, s, 0), 0.0)
        s *= 2
    return x


def _mm1(a, b):
    return _dot(a.astype(BF16), b.astype(BF16))


def _wkv_chunk(tiles, params, states, c):
    m0, strict, incl, ones_bd, eye = c
    cc = tiles[0][0].shape[0]
    c2 = 2 * cc
    n = len(tiles)
    inv_n = 1.0 / RW_HEAD

    def stack2(x):
        return jnp.concatenate([jnp.where(m0, x, 0.0), jnp.where(m0, 0.0, x)], axis=0)

    def segsum(xs):
        parts = [_split(x) for x in xs]
        return [_dot(hi, ones_bd) + _dot(lo, ones_bd) for hi, lo in parts]

    kk0 = [t[2] * p[0] for t, p in zip(tiles, params)]
    nrm2 = segsum([x * x for x in kk0])
    kmod = [t[2] * (1.0 + (t[4] - 1.0) * p[1]) for t, p in zip(tiles, params)]
    bon = segsum([t[0] * km * p[2] for t, km, p in zip(tiles, kmod, params)])

    lhs, rhs, vs, upd_r, dec = [], [], [], [], []
    for i in range(n):
        r, wl, k, v, a, g = tiles[i]
        kk = kk0[i] / jnp.maximum(jnp.sqrt(nrm2[i]), 1e-12)
        b = kk * a
        lw = -jnp.exp(wl)
        lcum = _cumsum_rows(lw)
        ltot = lcum[cc - 1:cc, :]
        e_neg = jnp.exp(-lcum)
        e_rem = jnp.exp(ltot - lcum)
        kt = kk * jnp.exp(lcum - lw)
        rt = r * jnp.exp(lcum)
        khb = (kmod[i] * e_neg).astype(BF16)
        bhb = (b * e_neg).astype(BF16)
        lhs.append(jnp.concatenate([stack2(kt), stack2(rt)], axis=0).astype(BF16))
        rhs.append(jnp.concatenate([khb, khb, bhb, bhb, states[i].astype(BF16)], axis=0))
        vs.append(stack2(v))
        upd_r.append(jnp.concatenate([stack2(kmod[i] * e_rem), stack2(b * e_rem)], axis=0).astype(BF16))
        dec.append(jnp.exp(ltot))

    m = [_dot_nt(a_, b_) for a_, b_ in zip(lhs, rhs)]
    a_k = [jnp.where(strict, x[:c2, :c2], 0.0) for x in m]
    a_b = [jnp.where(strict, x[:c2, c2:2 * c2], 0.0) for x in m]
    bkb = [jnp.concatenate([jnp.where(incl, x[c2:, :c2], 0.0), jnp.where(incl, -x[c2:, c2:2 * c2], 0.0)],
                           axis=1).astype(BF16) for x in m]
    rhs_u = [x[:c2, 2 * c2:] + _mm1(ak, v_) for x, ak, v_ in zip(m, a_k, vs)]
    xinv = [eye - ab for ab in a_b]
    pw = [ab.astype(BF16) for ab in a_b]
    lvl = 2
    while lvl < cc:
        pw = [_dot(p_, p_) for p_ in pw]
        pw = [p_.astype(BF16) for p_ in pw]
        xinv = [x + _dot(x.astype(BF16), p_) for x, p_ in zip(xinv, pw)]
        lvl *= 2
    us = [_mm1(x, ru) for x, ru in zip(xinv, rhs_u)]
    ys = [x[c2:, 2 * c2:] + _dot(bk, jnp.concatenate([v_, u_], axis=0).astype(BF16))
          for x, bk, v_, u_ in zip(m, bkb, vs, us)]
    y = [x[:cc] + x[cc:] for x in ys]
    upd_l = [jnp.concatenate([v_, -u_], axis=0).T.astype(BF16) for v_, u_ in zip(vs, us)]
    s_new = [s0 * d_ + _dot(ul, ur) for s0, d_, ul, ur in zip(states, dec, upd_l, upd_r)]

    mu = [x * inv_n for x in segsum(y)]
    yc = [a_ - b_ for a_, b_ in zip(y, mu)]
    var = [x * inv_n for x in segsum([x * x for x in yc])]
    outs = []
    for i in range(n):
        yn = yc[i] * lax.rsqrt(var[i] + RW_GN_EPS) * params[i][3] + params[i][4]
        outs.append((yn + bon[i] * tiles[i][3]) * tiles[i][5])
    return outs, s_new


def _wkv_kernel(r_ref, w_ref, k_ref, v_ref, a_ref, g_ref, kk_ref, ka_ref, rk_ref, lg_ref, lb_ref,
                o_ref, s_ref, *, pairs):
    @pl.when(pl.program_id(2) == 0)
    def _():
        s_ref[...] = jnp.zeros_like(s_ref)

    cc = r_ref.shape[0]
    c2 = 2 * cc
    lane = lax.broadcasted_iota(jnp.int32, (cc, LANES), 1)
    m0 = lane < RW_HEAD
    row = lax.broadcasted_iota(jnp.int32, (c2, c2), 0)
    col = lax.broadcasted_iota(jnp.int32, (c2, c2), 1)
    same = (row >= cc) == (col >= cc)
    strict = same & (col < row)
    incl = same & (col <= row)
    lr = lax.broadcasted_iota(jnp.int32, (LANES, LANES), 0)
    lc = lax.broadcasted_iota(jnp.int32, (LANES, LANES), 1)
    ones_bd = jnp.where((lr >= RW_HEAD) == (lc >= RW_HEAD), 1.0, 0.0).astype(BF16)
    eye = jnp.where(row == col, 1.0, 0.0).astype(F32)
    consts = (m0, strict, incl, ones_bd, eye)

    sls = [slice(p * LANES, (p + 1) * LANES) for p in range(pairs)]
    tiles = [tuple(ref[:, sl] for ref in (r_ref, w_ref, k_ref, v_ref, a_ref, g_ref)) for sl in sls]
    params = [tuple(ref[:, sl] for ref in (kk_ref, ka_ref, rk_ref, lg_ref, lb_ref)) for sl in sls]
    outs, s_new = _wkv_chunk(tiles, params, [s_ref[p] for p in range(pairs)], consts)
    for p in range(pairs):
        s_ref[p] = s_new[p]
        o_ref[:, sls[p]] = outs[p].astype(o_ref.dtype)


def _wkv(r, wl, k, v, a, g, kkp, kap, rkp, lgp, lbp, batch, out_dtype=BF16):
    t, d = r.shape
    lp = t // batch
    cc = WKV_CHUNK
    nchunk = lp // cc
    pairs = min(WKV_PAIRS_PER_STEP, d // LANES)
    wblk = pairs * LANES
    tok = pl.BlockSpec((cc, wblk), lambda b, p, c: (b * nchunk + c, p))
    par = pl.BlockSpec((1, wblk), lambda b, p, c: (0, p))
    prm = [x.reshape(1, d) for x in (kkp, kap, rkp, lgp, lbp)]
    return pl.pallas_call(
        functools.partial(_wkv_kernel, pairs=pairs),
        grid=(batch, d // wblk, nchunk),
        in_specs=[tok] * 6 + [par] * 5,
        out_specs=tok,
        out_shape=jax.ShapeDtypeStruct((t, d), out_dtype),
        scratch_shapes=[pltpu.VMEM((pairs, LANES, LANES), F32)],
        name="wkv7",
        compiler_params=_cparams(("parallel", "parallel", "arbitrary")),
    )(r, wl, k, v, a, g, *prm)


def _fgate_kernel(h_ref, wh_ref, wl_ref, b_ref, o_ref, carry_ref):
    @pl.when(pl.program_id(1) == 0)
    def _():
        carry_ref[...] = jnp.zeros_like(carry_ref)

    hh, hl = _split(h_ref[...])
    z = _dot(hh, wh_ref[...]) + (_dot(hh, wl_ref[...]) + _dot(hl, wh_ref[...])) + b_ref[...]
    logf = -_softplus(-z)
    cs = _cumsum_rows(logf) + carry_ref[...]
    o_ref[...] = cs
    carry_ref[...] = cs[cs.shape[0] - 1:, :]


def _fgate_cumsum(h, wf, bf, batch, tm=128):
    t, d = h.shape
    lp = t // batch
    nt = lp // tm
    nh = wf.shape[1]
    wpad = jnp.zeros((d, LANES), F32).at[:, :nh].set(wf)
    bpad = jnp.zeros((1, LANES), F32).at[0, :nh].set(bf)
    wh, wl = _split(wpad)
    return pl.pallas_call(
        _fgate_kernel,
        grid=(batch, nt),
        in_specs=[pl.BlockSpec((tm, d), lambda b, i: (b * nt + i, 0)),
                  pl.BlockSpec((d, LANES), lambda b, i: (0, 0)),
                  pl.BlockSpec((d, LANES), lambda b, i: (0, 0)),
                  pl.BlockSpec((1, LANES), lambda b, i: (0, 0))],
        out_specs=pl.BlockSpec((tm, LANES), lambda b, i: (b * nt + i, 0)),
        out_shape=jax.ShapeDtypeStruct((t, LANES), F32),
        scratch_shapes=[pltpu.VMEM((1, LANES), F32)],
        name="fgate_cumsum",
        compiler_params=_cparams(("parallel", "arbitrary")),
    )(h, wh, wl, bpad)


def _fox_kernel(q_ref, k_ref, v_ref, cq_ref, ck_ref, o_ref, *, heads, first, tile, scale):
    lp = q_ref.shape[0]
    nbig = (lp - first) // tile
    hs = [slice(h * FX_HEAD, (h + 1) * FX_HEAD) for h in range(heads)]

    def kv_step(q, cq, carry, kstart, ksize, diag):
        s = [_dot_nt(q[h], k_ref[pl.ds(kstart, ksize), hs[h]]) for h in range(heads)]
        s = [s[h] * scale + cq[h] - ck_ref[0, h, :, pl.ds(kstart, ksize)] for h in range(heads)]
        if diag:
            keep = (lax.broadcasted_iota(jnp.int32, (ksize, ksize), 1)
                    <= lax.broadcasted_iota(jnp.int32, (ksize, ksize), 0))
            s = [jnp.where(keep, x, NEG_INF) for x in s]
        m_new = [jnp.maximum(carry[h][0], jnp.max(s[h], axis=-1, keepdims=True)) for h in range(heads)]
        alpha = [jnp.exp(carry[h][0] - m_new[h]) for h in range(heads)]
        p = [jnp.exp(s[h] - m_new[h]) for h in range(heads)]
        l_new = [alpha[h] * carry[h][1] + jnp.sum(p[h], axis=-1, keepdims=True) for h in range(heads)]
        pv = [_dot(p[h].astype(BF16), v_ref[pl.ds(kstart, ksize), hs[h]]) for h in range(heads)]
        return [(m_new[h], l_new[h], alpha[h] * carry[h][2] + pv[h]) for h in range(heads)]

    def q_tile(qstart, tq, nfull, with_first):
        q = [q_ref[pl.ds(qstart, tq), hs[h]] for h in range(heads)]
        cq = [cq_ref[0, h, pl.ds(qstart, tq), :] for h in range(heads)]
        carry = [(jnp.full((tq, 1), NEG_INF, F32), jnp.zeros((tq, 1), F32), jnp.zeros((tq, FX_HEAD), F32))
                 for _ in range(heads)]
        if with_first:
            carry = kv_step(q, cq, carry, 0, first, False)
        if nfull is not None:
            carry = lax.fori_loop(
                0, nfull,
                lambda j, c: kv_step(q, cq, c, pl.multiple_of(first + j * tile, LANES), tile, False), carry)
        carry = kv_step(q, cq, carry, qstart, tq, True)
        for h in range(heads):
            o_ref[pl.ds(qstart, tq), hs[h]] = (carry[h][2] / carry[h][1]).astype(o_ref.dtype)

    if first:
        q_tile(0, first, None, False)

    def big(i, _):
        q_tile(pl.multiple_of(first + i * tile, LANES), tile, i, first > 0)
        return 0

    lax.fori_loop(0, nbig, big, 0)


def _fox_attention(q, kv, c_col, c_row, batch, heads=4, tile=256):
    t, d = q.shape
    lp = t // batch
    nh = d // FX_HEAD
    ng = nh // heads
    wblk = heads * FX_HEAD
    first = lp % tile
    return pl.pallas_call(
        functools.partial(_fox_kernel, heads=heads, first=first, tile=tile, scale=FX_HEAD ** -0.5),
        grid=(batch, ng),
        in_specs=[pl.BlockSpec((lp, wblk), lambda b, g: (b, g)),
                  pl.BlockSpec((lp, wblk), lambda b, g: (b, g)),
                  pl.BlockSpec((lp, wblk), lambda b, g: (b, ng + g)),
                  pl.BlockSpec((1, heads, lp, 1), lambda b, g: (b, g, 0, 0)),
                  pl.BlockSpec((1, heads, 1, lp), lambda b, g: (b, g, 0, 0))],
        out_specs=pl.BlockSpec((lp, wblk), lambda b, g: (b, g)),
        out_shape=jax.ShapeDtypeStruct((t, d), BF16),
        name="fox_attention",
        compiler_params=_cparams(("parallel", "parallel")),
    )(q, kv, kv, c_col, c_row)


def _pad_cols(w, n):
    return jnp.zeros((w.shape[0], n), w.dtype).at[:, :w.shape[1]].set(w)


def _pad_rows(w, n):
    return jnp.zeros((n, w.shape[1]), w.dtype).at[:w.shape[0], :].set(w)


def _rwkv7_mix(h, batch, mu, w_rkv, w_o, w0, w1, w2, a0, a1, a2, g1, g2, k_k, k_a, r_k, lnx_g, lnx_b,
               ln_g, ln_b, alpha):
    t, d = h.shape
    lp = t // batch
    h3 = h.reshape(batch, lp, d)
    xx = jnp.pad(h3, ((0, 0), (1, 0), (0, 0)))[:, :-1] - h3
    xs = [(h3 + xx * mu[i]).reshape(t, d).astype(BF16) for i in range(6)]
    xr, xw, xk, xv, xa, xg = xs
    rkv = _bmm(jnp.stack([xr, xk, xv]), w_rkv.astype(BF16))
    lw = _pad_cols(w1, LANES)
    wl = _matmul(_matmul(xw, lw.astype(BF16), act="tanh", out_dtype=BF16, tn=LANES, name="w_lora1"),
                 _pad_rows(w2, LANES).astype(BF16), bias=w0, act="decay", name="w_lora2")
    la = _pad_cols(a1, LANES)
    a = _matmul(_matmul(xa, la.astype(BF16), out_dtype=BF16, tn=LANES, name="a_lora1"),
                _pad_rows(a2, LANES).astype(BF16), bias=a0, act="sigmoid", name="a_lora2")
    g = _matmul(_matmul(xg, g1.astype(BF16), act="sigmoid", out_dtype=BF16, tn=g1.shape[1], name="g_lora1"),
                g2.astype(BF16), name="g_lora2")
    o = _wkv(rkv[0], wl, rkv[1], rkv[2], a, g, k_k, k_a, r_k.reshape(-1), lnx_g, lnx_b, batch)
    return _matmul_ln(o, w_o.astype(BF16), h, ln_g, ln_b, alpha)


def kernel(x, meta_tokens, ln_g, ln_b, ffn_w1, ffn_w3, ffn_w2, rw_mu, rw_w_rkv, rw_w_o, rw_w0, rw_w1,
           rw_w2, rw_a0, rw_a1, rw_a2, rw_g1, rw_g2, rw_k_k, rw_k_a, rw_r_k, rw_lnx_g, rw_lnx_b,
           fx_w_q, fx_w_o, fx_w_kvf, fx_b_f):
    batch, seq, d = x.shape
    depth = ln_g.shape[0]
    n_a = rw_mu.shape[0]
    alpha = (2 * depth) ** 0.25
    l_real = seq + N_META
    lp = -(-l_real // SEQ_ALIGN) * SEQ_ALIGN
    t = batch * lp
    nh = d // FX_HEAD

    meta = jnp.broadcast_to(meta_tokens.astype(x.dtype)[None], (batch, N_META, d))
    h = jnp.concatenate([meta, x, jnp.zeros((batch, lp - l_real, d), x.dtype)], axis=1).reshape(t, d)

    kv = c_col = c_row = None
    for l in range(depth):
        h = _ffn_ln(h, ffn_w1[l, 0].astype(BF16), ffn_w3[l, 0].astype(BF16), ffn_w2[l, 0].astype(BF16),
                    ln_g[l, 0], ln_b[l, 0], alpha)
        if l < n_a:
            h = _rwkv7_mix(h, batch, rw_mu[l], rw_w_rkv[l], rw_w_o[l], rw_w0[l], rw_w1[l], rw_w2[l],
                           rw_a0[l], rw_a1[l], rw_a2[l], rw_g1[l], rw_g2[l], rw_k_k[l], rw_k_a[l],
                           rw_r_k[l], rw_lnx_g[l], rw_lnx_b[l], ln_g[l, 1], ln_b[l, 1], alpha)
        else:
            j = l - n_a
            q = _matmul(h.astype(BF16), fx_w_q[j].astype(BF16), out_dtype=BF16, name="q_proj")
            o = _fox_attention(q, kv, c_col, c_row, batch)
            h = _matmul_ln(o, fx_w_o[j].astype(BF16), h, ln_g[l, 1], ln_b[l, 1], alpha)
        h = _ffn_ln(h, ffn_w1[l, 1].astype(BF16), ffn_w3[l, 1].astype(BF16), ffn_w2[l, 1].astype(BF16),
                    ln_g[l, 2], ln_b[l, 2], alpha)
        if l == n_a - 1:
            kv = _matmul(h.astype(BF16), fx_w_kvf[:, :2 * d].astype(BF16), out_dtype=BF16, name="kv_proj")
            c = _fgate_cumsum(h, fx_w_kvf[:, 2 * d:], fx_b_f, batch)[:, :nh]
            c = jnp.transpose(c.reshape(batch, lp, nh), (0, 2, 1))
            c_col = c[..., None]
            c_row = c[:, :, None, :]
    return h.reshape(batch, lp, d)[:, N_META:l_real]
```

```python
import functools

import jax
import jax.numpy as jnp
from jax import lax
from jax.experimental import pallas as pl
from jax.experimental.pallas import tpu as pltpu

N_META = 16
RW_HEAD = 64
FX_HEAD = 128
LN_EPS = 1e-5
RW_GN_EPS = RW_HEAD * 1e-5
NEG_INF = -1e30

LANES = 128
SEQ_ALIGN = 128
WKV_CHUNK = 64
WKV_PAIRS_PER_STEP = 8
VMEM_LIMIT = 56 * 1024 * 1024

F32 = jnp.float32
BF16 = jnp.bfloat16


def _cparams(sem):
    return pltpu.CompilerParams(dimension_semantics=sem, vmem_limit_bytes=VMEM_LIMIT)


def _dot(a, b):
    return jnp.dot(a, b, preferred_element_type=F32)


def _dot_nt(a, b):
    return lax.dot_general(a, b, (((1,), (1,)), ((), ())), preferred_element_type=F32)


def _split(x):
    hi = x.astype(BF16)
    lo = (x - hi.astype(F32)).astype(BF16)
    return hi, lo


def _softplus(x):
    return jnp.maximum(x, 0.0) + jnp.log1p(jnp.exp(-jnp.abs(x)))


def _sigmoid(x):
    return 1.0 / (1.0 + jnp.exp(-x))


def _layer_norm(y, g, b):
    mu = jnp.mean(y, axis=-1, keepdims=True)
    yc = y - mu
    var = jnp.mean(yc * yc, axis=-1, keepdims=True)
    return yc * lax.rsqrt(var + LN_EPS) * g + b


def _mm_kernel(x_ref, w_ref, b_ref, o_ref, *scratch, act):
    if scratch:
        xb_ref, = scratch

        @pl.when(pl.program_id(1) == 0)
        def _():
            xb_ref[...] = x_ref[...].astype(BF16)

        x = xb_ref[...]
    else:
        x = x_ref[...]
    acc = _dot(x, w_ref[...]) + b_ref[...]
    if act == "tanh":
        acc = jnp.tanh(acc)
    elif act == "sigmoid":
        acc = _sigmoid(acc)
    elif act == "decay":
        acc = -_softplus(-acc) - 0.5
    o_ref[...] = acc.astype(o_ref.dtype)


def _matmul(x, w, bias=None, act=None, out_dtype=F32, tm=None, tn=None, xblk=0, n=None, name="matmul"):
    t = x.shape[0]
    k = w.shape[0]
    n = n or w.shape[1]
    tm = tm or _pick_tm(t)
    tn = tn or min(n, 512)
    if bias is None:
        bias = jnp.zeros((1, n), F32)
    cast = x.dtype != BF16
    return pl.pallas_call(
        functools.partial(_mm_kernel, act=act),
        grid=(t // tm, n // tn),
        in_specs=[pl.BlockSpec((tm, k), lambda i, j: (i, xblk)),
                  pl.BlockSpec((k, tn), lambda i, j: (0, j)),
                  pl.BlockSpec((1, tn), lambda i, j: (0, j))],
        out_specs=pl.BlockSpec((tm, tn), lambda i, j: (i, j)),
        out_shape=jax.ShapeDtypeStruct((t, n), out_dtype),
        scratch_shapes=[pltpu.VMEM((tm, k), BF16)] if cast else [],
        name=name,
        compiler_params=_cparams(("parallel", "arbitrary" if cast else "parallel")),
    )(x, w, bias.reshape(1, n).astype(F32))


def _pick_tm(t, cap=1088):
    for tm in range(min(cap, t), 0, -16):
        if t % tm == 0:
            return tm
    return t


PREV_ROWS = 8


def _rwkv_in_kernel(h_ref, prev_ref, mu_ref, w_ref, lw_ref, rkv_ref, lora_ref, xs_ref, *, lp, nj, r1):
    i = pl.program_id(0)
    j = pl.program_id(1)
    tm = h_ref.shape[0]

    @pl.when(j == 0)
    def _():
        h = h_ref[...]
        last = prev_ref[PREV_ROWS - 1:, :]
        last = jnp.where(lax.rem(i * tm, lp) == 0, 0.0, last)
        row = lax.broadcasted_iota(jnp.int32, h.shape, 0)
        xx = jnp.where(row == 0, last, pltpu.roll(h, 1, 0)) - h

        def mix(n):
            return (h + xx * mu_ref[n:n + 1, :]).astype(BF16)

        xs_ref[0] = mix(0)
        xs_ref[1] = mix(2)
        xs_ref[2] = mix(3)
        tw = jnp.tanh(_dot(mix(1), lw_ref[:, :r1]))
        ta = _dot(mix(4), lw_ref[:, r1:2 * r1])
        tg = _sigmoid(_dot(mix(5), lw_ref[:, 2 * r1:]))
        lora_ref[...] = jnp.concatenate([tw, ta, tg], axis=1).astype(BF16)

    rkv_ref[...] = _dot(xs_ref[j // nj], w_ref[...])


def _rwkv_in(h, mu, w_rkv, lw, batch, r1, tm=None, tn=512):
    t, d = h.shape
    lp = t // batch
    tm = tm or _pick_tm(lp, 544)
    nj = d // tn
    nl = lw.shape[1]
    pblk = tm // PREV_ROWS
    return pl.pallas_call(
        functools.partial(_rwkv_in_kernel, lp=lp, nj=nj, r1=r1),
        grid=(t // tm, 3 * nj),
        in_specs=[pl.BlockSpec((tm, d), lambda i, j: (i, 0)),
                  pl.BlockSpec((PREV_ROWS, d), lambda i, j: (jnp.maximum(i * pblk - 1, 0), 0)),
                  pl.BlockSpec((6, d), lambda i, j: (0, 0)),
                  pl.BlockSpec((None, d, tn), lambda i, j: (j // nj, 0, j % nj)),
                  pl.BlockSpec((d, nl), lambda i, j: (0, 0))],
        out_specs=[pl.BlockSpec((None, tm, tn), lambda i, j: (j // nj, i, j % nj)),
                   pl.BlockSpec((tm, nl), lambda i, j: (i, 0))],
        out_shape=[jax.ShapeDtypeStruct((3, t, d), F32), jax.ShapeDtypeStruct((t, nl), BF16)],
        scratch_shapes=[pltpu.VMEM((3, tm, d), BF16)],
        name="rwkv_in",
        compiler_params=_cparams(("parallel", "arbitrary")),
    )(h, h, mu, w_rkv, lw)


def _mm_ln_kernel(x_ref, w_ref, h_ref, g_ref, b_ref, o_ref, *, alpha):
    y = alpha * h_ref[...] + _dot(x_ref[...], w_ref[...])
    o_ref[...] = _layer_norm(y, g_ref[...], b_ref[...])


def _matmul_ln(x, w, h, g, b, alpha, tm=None):
    t, k = x.shape
    n = w.shape[1]
    tm = tm or _pick_tm(t, 512)
    return pl.pallas_call(
        functools.partial(_mm_ln_kernel, alpha=alpha),
        grid=(t // tm,),
        in_specs=[pl.BlockSpec((tm, k), lambda i: (i, 0)),
                  pl.BlockSpec((k, n), lambda i: (0, 0)),
                  pl.BlockSpec((tm, n), lambda i: (i, 0)),
                  pl.BlockSpec((1, n), lambda i: (0, 0)),
                  pl.BlockSpec((1, n), lambda i: (0, 0))],
        out_specs=pl.BlockSpec((tm, n), lambda i: (i, 0)),
        out_shape=jax.ShapeDtypeStruct((t, n), F32),
        name="proj_ln",
        compiler_params=_cparams(("parallel",)),
    )(x, w, h, g.reshape(1, n), b.reshape(1, n))


def _ffn_ln_kernel(h_ref, w1_ref, w3_ref, w2_ref, g_ref, b_ref, o_ref, hb_ref, *, alpha):
    j = pl.program_id(1)

    @pl.when(j == 0)
    def _():
        hb_ref[...] = h_ref[...].astype(BF16)
        o_ref[...] = jnp.zeros_like(o_ref)

    hb = hb_ref[...]
    u = _dot(hb, w1_ref[...])
    v = _dot(hb, w3_ref[...])
    act = (u * _sigmoid(u) * v).astype(BF16)
    o_ref[...] += _dot(act, w2_ref[...])

    @pl.when(j == pl.num_programs(1) - 1)
    def _():
        y = alpha * h_ref[...] + 0.5 * o_ref[...]
        o_ref[...] = _layer_norm(y, g_ref[...], b_ref[...])


def _ffn_ln(h, w1, w3, w2, l, s, g, b, alpha, tm=None, tf=512):
    t, d = h.shape
    f = w1.shape[-1]
    tm = tm or _pick_tm(t, 544)
    return pl.pallas_call(
        functools.partial(_ffn_ln_kernel, alpha=alpha),
        grid=(t // tm, f // tf),
        in_specs=[pl.BlockSpec((tm, d), lambda i, j: (i, 0)),
                  pl.BlockSpec((None, None, d, tf), lambda i, j: (l, s, 0, j)),
                  pl.BlockSpec((None, None, d, tf), lambda i, j: (l, s, 0, j)),
                  pl.BlockSpec((None, None, tf, d), lambda i, j: (l, s, j, 0)),
                  pl.BlockSpec((1, d), lambda i, j: (0, 0)),
                  pl.BlockSpec((1, d), lambda i, j: (0, 0))],
        out_specs=pl.BlockSpec((tm, d), lambda i, j: (i, 0)),
        out_shape=jax.ShapeDtypeStruct((t, d), F32),
        scratch_shapes=[pltpu.VMEM((tm, d), BF16)],
        name="ffn_ln",
        compiler_params=_cparams(("parallel", "arbitrary")),
    )(h, w1, w3, w2, g.reshape(1, d), b.reshape(1, d))


def _cumsum_rows(x):
    n = x.shape[0]
    row = lax.broadcasted_iota(jnp.int32, x.shape, 0)
    s = 1
    while s < n:
        x = x + jnp.where(row >= s, pltpu.roll(x, s, 0), 0.0)
        s *= 2
    return x


def _mm1(a, b):
    return _dot(a.astype(BF16), b.astype(BF16))


def _wkv_chunk(tiles, params, states, c):
    m0, strict, incl, ones_bd, eye = c
    cc = tiles[0][0].shape[0]
    c2 = 2 * cc
    n = len(tiles)
    inv_n = 1.0 / RW_HEAD

    def stack2(x):
        return jnp.concatenate([jnp.where(m0, x, 0.0), jnp.where(m0, 0.0, x)], axis=0)

    def segsum(xs):
        parts = [_split(x) for x in xs]
        return [_dot(hi, ones_bd) + _dot(lo, ones_bd) for hi, lo in parts]

    kk0 = [t[2] * p[0] for t, p in zip(tiles, params)]
    nrm2 = segsum([x * x for x in kk0])
    kmod = [t[2] * (1.0 + (t[4] - 1.0) * p[1]) for t, p in zip(tiles, params)]
    bon = segsum([t[0] * km * p[2] for t, km, p in zip(tiles, kmod, params)])

    lhs, rhs, vs, upd_r, dec = [], [], [], [], []
    for i in range(n):
        r, wl, k, v, a, g = tiles[i]
        kk = kk0[i] / jnp.maximum(jnp.sqrt(nrm2[i]), 1e-12)
        b = kk * a
        lw = -jnp.exp(wl)
        lcum = _cumsum_rows(lw)
        ltot = lcum[cc - 1:cc, :]
        e_neg = jnp.exp(-lcum)
        e_rem = jnp.exp(ltot - lcum)
        kt = kk * jnp.exp(lcum - lw)
        rt = r * jnp.exp(lcum)
        khb = (kmod[i] * e_neg).astype(BF16)
        bhb = (b * e_neg).astype(BF16)
        lhs.append(jnp.concatenate([stack2(kt), stack2(rt)], axis=0).astype(BF16))
        rhs.append(jnp.concatenate([khb, khb, bhb, bhb, states[i].astype(BF16)], axis=0))
        vs.append(stack2(v))
        upd_r.append(jnp.concatenate([stack2(kmod[i] * e_rem), stack2(b * e_rem)], axis=0).astype(BF16))
        dec.append(jnp.exp(ltot))

    m = [_dot_nt(a_, b_) for a_, b_ in zip(lhs, rhs)]
    a_k = [jnp.where(strict, x[:c2, :c2], 0.0) for x in m]
    a_b = [jnp.where(strict, x[:c2, c2:2 * c2], 0.0) for x in m]
    bkb = [jnp.concatenate([jnp.where(incl, x[c2:, :c2], 0.0), jnp.where(incl, -x[c2:, c2:2 * c2], 0.0)],
                           axis=1).astype(BF16) for x in m]
    rhs_u = [x[:c2, 2 * c2:] + _mm1(ak, v_) for x, ak, v_ in zip(m, a_k, vs)]
    xinv = [eye - ab for ab in a_b]
    pw = [ab.astype(BF16) for ab in a_b]
    lvl = 2
    while lvl < cc:
        pw = [_dot(p_, p_) for p_ in pw]
        pw = [p_.astype(BF16) for p_ in pw]
        xinv = [x + _dot(x.astype(BF16), p_) for x, p_ in zip(xinv, pw)]
        lvl *= 2
    us = [_mm1(x, ru) for x, ru in zip(xinv, rhs_u)]
    ys = [x[c2:, 2 * c2:] + _dot(bk, jnp.concatenate([v_, u_], axis=0).astype(BF16))
          for x, bk, v_, u_ in zip(m, bkb, vs, us)]
    y = [x[:cc] + x[cc:] for x in ys]
    upd_l = [jnp.concatenate([v_, -u_], axis=0).T.astype(BF16) for v_, u_ in zip(vs, us)]
    s_new = [s0 * d_ + _dot(ul, ur) for s0, d_, ul, ur in zip(states, dec, upd_l, upd_r)]

    mu = [x * inv_n for x in segsum(y)]
    yc = [a_ - b_ for a_, b_ in zip(y, mu)]
    var = [x * inv_n for x in segsum([x * x for x in yc])]
    outs = []
    for i in range(n):
        yn = yc[i] * lax.rsqrt(var[i] + RW_GN_EPS) * params[i][3] + params[i][4]
        outs.append((yn + bon[i] * tiles[i][3]) * tiles[i][5])
    return outs, s_new


def _wkv_kernel(r_ref, w_ref, k_ref, v_ref, a_ref, g_ref, kk_ref, ka_ref, rk_ref, lg_ref, lb_ref,
                o_ref, s_ref, *, pairs):
    @pl.when(pl.program_id(2) == 0)
    def _():
        s_ref[...] = jnp.zeros_like(s_ref)

    cc = r_ref.shape[0]
    c2 = 2 * cc
    lane = lax.broadcasted_iota(jnp.int32, (cc, LANES), 1)
    m0 = lane < RW_HEAD
    row = lax.broadcasted_iota(jnp.int32, (c2, c2), 0)
    col = lax.broadcasted_iota(jnp.int32, (c2, c2), 1)
    same = (row >= cc) == (col >= cc)
    strict = same & (col < row)
    incl = same & (col <= row)
    lr = lax.broadcasted_iota(jnp.int32, (LANES, LANES), 0)
    lc = lax.broadcasted_iota(jnp.int32, (LANES, LANES), 1)
    ones_bd = jnp.where((lr >= RW_HEAD) == (lc >= RW_HEAD), 1.0, 0.0).astype(BF16)
    eye = jnp.where(row == col, 1.0, 0.0).astype(F32)
    consts = (m0, strict, incl, ones_bd, eye)

    sls = [slice(p * LANES, (p + 1) * LANES) for p in range(pairs)]
    tiles = [tuple(ref[:, sl] for ref in (r_ref, w_ref, k_ref, v_ref, a_ref, g_ref)) for sl in sls]
    params = [tuple(ref[:, sl] for ref in (kk_ref, ka_ref, rk_ref, lg_ref, lb_ref)) for sl in sls]
    outs, s_new = _wkv_chunk(tiles, params, [s_ref[p] for p in range(pairs)], consts)
    for p in range(pairs):
        s_ref[p] = s_new[p]
        o_ref[:, sls[p]] = outs[p].astype(o_ref.dtype)


def _wkv(r, wl, k, v, a, g, kkp, kap, rkp, lgp, lbp, batch, out_dtype=BF16):
    t, d = r.shape
    lp = t // batch
    cc = WKV_CHUNK
    nchunk = lp // cc
    pairs = min(WKV_PAIRS_PER_STEP, d // LANES)
    wblk = pairs * LANES
    tok = pl.BlockSpec((cc, wblk), lambda b, p, c: (b * nchunk + c, p))
    par = pl.BlockSpec((1, wblk), lambda b, p, c: (0, p))
    prm = [x.reshape(1, d) for x in (kkp, kap, rkp, lgp, lbp)]
    return pl.pallas_call(
        functools.partial(_wkv_kernel, pairs=pairs),
        grid=(batch, d // wblk, nchunk),
        in_specs=[tok] * 6 + [par] * 5,
        out_specs=tok,
        out_shape=jax.ShapeDtypeStruct((t, d), out_dtype),
        scratch_shapes=[pltpu.VMEM((pairs, LANES, LANES), F32)],
        name="wkv7",
        compiler_params=_cparams(("parallel", "parallel", "arbitrary")),
    )(r, wl, k, v, a, g, *prm)


def _fgate_kernel(h_ref, wh_ref, wl_ref, b_ref, o_ref, carry_ref):
    @pl.when(pl.program_id(1) == 0)
    def _():
        carry_ref[...] = jnp.zeros_like(carry_ref)

    hh, hl = _split(h_ref[...])
    z = _dot(hh, wh_ref[...]) + (_dot(hh, wl_ref[...]) + _dot(hl, wh_ref[...])) + b_ref[...]
    logf = -_softplus(-z)
    cs = _cumsum_rows(logf) + carry_ref[...]
    o_ref[...] = cs
    carry_ref[...] = cs[cs.shape[0] - 1:, :]


def _fgate_cumsum(h, wf, bf, batch, tm=128):
    t, d = h.shape
    lp = t // batch
    nt = lp // tm
    nh = wf.shape[1]
    wpad = jnp.zeros((d, LANES), F32).at[:, :nh].set(wf)
    bpad = jnp.zeros((1, LANES), F32).at[0, :nh].set(bf)
    wh, wl = _split(wpad)
    return pl.pallas_call(
        _fgate_kernel,
        grid=(batch, nt),
        in_specs=[pl.BlockSpec((tm, d), lambda b, i: (b * nt + i, 0)),
                  pl.BlockSpec((d, LANES), lambda b, i: (0, 0)),
                  pl.BlockSpec((d, LANES), lambda b, i: (0, 0)),
                  pl.BlockSpec((1, LANES), lambda b, i: (0, 0))],
        out_specs=pl.BlockSpec((tm, LANES), lambda b, i: (b * nt + i, 0)),
        out_shape=jax.ShapeDtypeStruct((t, LANES), F32),
        scratch_shapes=[pltpu.VMEM((1, LANES), F32)],
        name="fgate_cumsum",
        compiler_params=_cparams(("parallel", "arbitrary")),
    )(h, wh, wl, bpad)


def _fox_kernel(q_ref, k_ref, v_ref, cq_ref, ck_ref, o_ref, *, heads, first, tile, scale):
    lp = q_ref.shape[0]
    nbig = (lp - first) // tile
    hs = [slice(h * FX_HEAD, (h + 1) * FX_HEAD) for h in range(heads)]

    def kv_step(q, cq, carry, kstart, ksize, diag):
        s = [_dot_nt(q[h], k_ref[pl.ds(kstart, ksize), hs[h]]) for h in range(heads)]
        s = [s[h] * scale + cq[h] - ck_ref[0, h, :, pl.ds(kstart, ksize)] for h in range(heads)]
        if diag:
            keep = (lax.broadcasted_iota(jnp.int32, (ksize, ksize), 1)
                    <= lax.broadcasted_iota(jnp.int32, (ksize, ksize), 0))
            s = [jnp.where(keep, x, NEG_INF) for x in s]
        m_new = [jnp.maximum(carry[h][0], jnp.max(s[h], axis=-1, keepdims=True)) for h in range(heads)]
        alpha = [jnp.exp(carry[h][0] - m_new[h]) for h in range(heads)]
        p = [jnp.exp(s[h] - m_new[h]) for h in range(heads)]
        l_new = [alpha[h] * carry[h][1] + jnp.sum(p[h], axis=-1, keepdims=True) for h in range(heads)]
        pv = [_dot(p[h].astype(BF16), v_ref[pl.ds(kstart, ksize), hs[h]]) for h in range(heads)]
        return [(m_new[h], l_new[h], alpha[h] * carry[h][2] + pv[h]) for h in range(heads)]

    def q_tile(qstart, tq, nfull, with_first):
        q = [q_ref[pl.ds(qstart, tq), hs[h]] for h in range(heads)]
        cq = [cq_ref[0, h, pl.ds(qstart, tq), :] for h in range(heads)]
        carry = [(jnp.full((tq, 1), NEG_INF, F32), jnp.zeros((tq, 1), F32), jnp.zeros((tq, FX_HEAD), F32))
                 for _ in range(heads)]
        if with_first:
            carry = kv_step(q, cq, carry, 0, first, False)
        if nfull is not None:
            carry = lax.fori_loop(
                0, nfull,
                lambda j, c: kv_step(q, cq, c, pl.multiple_of(first + j * tile, LANES), tile, False), carry)
        carry = kv_step(q, cq, carry, qstart, tq, True)
        for h in range(heads):
            o_ref[pl.ds(qstart, tq), hs[h]] = (carry[h][2] / carry[h][1]).astype(o_ref.dtype)

    if first:
        q_tile(0, first, None, False)

    def big(i, _):
        q_tile(pl.multiple_of(first + i * tile, LANES), tile, i, first > 0)
        return 0

    lax.fori_loop(0, nbig, big, 0)


def _fox_attention(q, kv, c_col, c_row, batch, heads=4, tile=256):
    t, d = q.shape
    lp = t // batch
    nh = d // FX_HEAD
    ng = nh // heads
    wblk = heads * FX_HEAD
    first = lp % tile
    return pl.pallas_call(
        functools.partial(_fox_kernel, heads=heads, first=first, tile=tile, scale=FX_HEAD ** -0.5),
        grid=(batch, ng),
        in_specs=[pl.BlockSpec((lp, wblk), lambda b, g: (b, g)),
                  pl.BlockSpec((lp, wblk), lambda b, g: (b, g)),
                  pl.BlockSpec((lp, wblk), lambda b, g: (b, ng + g)),
                  pl.BlockSpec((1, heads, lp, 1), lambda b, g: (b, g, 0, 0)),
                  pl.BlockSpec((1, heads, 1, lp), lambda b, g: (b, g, 0, 0))],
        out_specs=pl.BlockSpec((lp, wblk), lambda b, g: (b, g)),
        out_shape=jax.ShapeDtypeStruct((t, d), BF16),
        name="fox_attention",
        compiler_params=_cparams(("parallel", "parallel")),
    )(q, kv, kv, c_col, c_row)


def _pad_cols(w, n):
    return jnp.zeros((w.shape[0], n), w.dtype).at[:, :w.shape[1]].set(w)


def _pad_rows(w, n):
    return jnp.zeros((n, w.shape[1]), w.dtype).at[:w.shape[0], :].set(w)


def _rwkv7_mix(h, batch, mu, w_rkv, w_o, w0, w1, w2, a0, a1, a2, g1, g2, k_k, k_a, r_k, lnx_g, lnx_b,
               ln_g, ln_b, alpha):
    r1 = LANES
    lw = jnp.concatenate([_pad_cols(w1, r1), _pad_cols(a1, r1), g1], axis=1).astype(BF16)
    rkv, lora = _rwkv_in(h, mu, w_rkv.astype(BF16), lw, batch, r1)
    wl = _matmul(lora, _pad_rows(w2, r1).astype(BF16), bias=w0, act="decay", xblk=0, name="w_lora2")
    a = _matmul(lora, _pad_rows(a2, r1).astype(BF16), bias=a0, act="sigmoid", xblk=1, name="a_lora2")
    g = _matmul(lora, g2.astype(BF16), xblk=(2 * r1) // g2.shape[0], name="g_lora2")
    o = _wkv(rkv[0], wl, rkv[1], rkv[2], a, g, k_k, k_a, r_k.reshape(-1), lnx_g, lnx_b, batch)
    return _matmul_ln(o, w_o.astype(BF16), h, ln_g, ln_b, alpha)


def kernel(x, meta_tokens, ln_g, ln_b, ffn_w1, ffn_w3, ffn_w2, rw_mu, rw_w_rkv, rw_w_o, rw_w0, rw_w1,
           rw_w2, rw_a0, rw_a1, rw_a2, rw_g1, rw_g2, rw_k_k, rw_k_a, rw_r_k, rw_lnx_g, rw_lnx_b,
           fx_w_q, fx_w_o, fx_w_kvf, fx_b_f):
    batch, seq, d = x.shape
    depth = ln_g.shape[0]
    n_a = rw_mu.shape[0]
    alpha = (2 * depth) ** 0.25
    l_real = seq + N_META
    lp = -(-l_real // SEQ_ALIGN) * SEQ_ALIGN
    t = batch * lp
    nh = d // FX_HEAD

    meta = jnp.broadcast_to(meta_tokens.astype(x.dtype)[None], (batch, N_META, d))
    h = jnp.concatenate([meta, x, jnp.zeros((batch, lp - l_real, d), x.dtype)], axis=1).reshape(t, d)

    w1b, w3b, w2b = ffn_w1.astype(BF16), ffn_w3.astype(BF16), ffn_w2.astype(BF16)
    kv = c_col = c_row = None
    for l in range(depth):
        h = _ffn_ln(h, w1b, w3b, w2b, l, 0, ln_g[l, 0], ln_b[l, 0], alpha)
        if l < n_a:
            h = _rwkv7_mix(h, batch, rw_mu[l], rw_w_rkv[l], rw_w_o[l], rw_w0[l], rw_w1[l], rw_w2[l],
                           rw_a0[l], rw_a1[l], rw_a2[l], rw_g1[l], rw_g2[l], rw_k_k[l], rw_k_a[l],
                           rw_r_k[l], rw_lnx_g[l], rw_lnx_b[l], ln_g[l, 1], ln_b[l, 1], alpha)
        else:
            j = l - n_a
            q = _matmul(h, fx_w_q[j].astype(BF16), out_dtype=BF16, name="q_proj")
            o = _fox_attention(q, kv, c_col, c_row, batch)
            h = _matmul_ln(o, fx_w_o[j].astype(BF16), h, ln_g[l, 1], ln_b[l, 1], alpha)
        h = _ffn_ln(h, w1b, w3b, w2b, l, 1, ln_g[l, 2], ln_b[l, 2], alpha)
        if l == n_a - 1:
            kv = _matmul(h, fx_w_kvf.astype(BF16), n=2 * d, out_dtype=BF16, name="kv_proj")
            c = _fgate_cumsum(h, fx_w_kvf[:, 2 * d:], fx_b_f, batch)[:, :nh]
            c = jnp.transpose(c.reshape(batch, lp, nh), (0, 2, 1))
            c_col = c[..., None]
            c_row = c[:, :, None, :]
    return h.reshape(batch, lp, d)[:, N_META:l_real]
```

```python
import functools

import jax
import jax.numpy as jnp
from jax import lax
from jax.experimental import pallas as pl
from jax.experimental.pallas import tpu as pltpu

N_META = 16
RW_HEAD = 64
FX_HEAD = 128
LN_EPS = 1e-5
RW_GN_EPS = RW_HEAD * 1e-5
NEG_INF = -1e30

LANES = 128
SEQ_ALIGN = 128
WKV_CHUNK = 64
WKV_PAIRS_PER_STEP = 16
FOX_HEADS_PER_STEP = 4
VMEM_LIMIT = 56 * 1024 * 1024

F32 = jnp.float32
BF16 = jnp.bfloat16


def _cparams(sem):
    return pltpu.CompilerParams(dimension_semantics=sem, vmem_limit_bytes=VMEM_LIMIT)


def _dot(a, b):
    return jnp.dot(a, b, preferred_element_type=F32)


def _dot_nt(a, b):
    return lax.dot_general(a, b, (((1,), (1,)), ((), ())), preferred_element_type=F32)


def _split(x):
    hi = x.astype(BF16)
    lo = (x - hi.astype(F32)).astype(BF16)
    return hi, lo


def _softplus(x):
    return jnp.maximum(x, 0.0) + jnp.log1p(jnp.exp(-jnp.abs(x)))


def _sigmoid(x):
    return 1.0 / (1.0 + jnp.exp(-x))


def _layer_norm(y, g, b):
    mu = jnp.mean(y, axis=-1, keepdims=True)
    yc = y - mu
    var = jnp.mean(yc * yc, axis=-1, keepdims=True)
    return yc * lax.rsqrt(var + LN_EPS) * g + b


def _mm_kernel(x_ref, w_ref, b_ref, o_ref, *scratch, act):
    if scratch:
        xb_ref, = scratch

        @pl.when(pl.program_id(1) == 0)
        def _():
            xb_ref[...] = x_ref[...].astype(BF16)

        x = xb_ref[...]
    else:
        x = x_ref[...]
    acc = _dot(x, w_ref[...]) + b_ref[...]
    if act == "tanh":
        acc = jnp.tanh(acc)
    elif act == "sigmoid":
        acc = _sigmoid(acc)
    elif act == "decay":
        acc = -_softplus(-acc) - 0.5
    o_ref[...] = acc.astype(o_ref.dtype)


def _matmul(x, w, bias=None, act=None, out_dtype=F32, tm=None, tn=None, xblk=0, n=None, name="matmul"):
    t = x.shape[0]
    k = w.shape[0]
    n = n or w.shape[1]
    tm = tm or _pick_tm(t)
    tn = tn or min(n, 512)
    if bias is None:
        bias = jnp.zeros((1, n), F32)
    cast = x.dtype != BF16
    return pl.pallas_call(
        functools.partial(_mm_kernel, act=act),
        grid=(t // tm, n // tn),
        in_specs=[pl.BlockSpec((tm, k), lambda i, j: (i, xblk)),
                  pl.BlockSpec((k, tn), lambda i, j: (0, j)),
                  pl.BlockSpec((1, tn), lambda i, j: (0, j))],
        out_specs=pl.BlockSpec((tm, tn), lambda i, j: (i, j)),
        out_shape=jax.ShapeDtypeStruct((t, n), out_dtype),
        scratch_shapes=[pltpu.VMEM((tm, k), BF16)] if cast else [],
        name=name,
        compiler_params=_cparams(("parallel", "arbitrary" if cast else "parallel")),
    )(x, w, bias.reshape(1, n).astype(F32))


def _pick_tm(t, cap=1088):
    for tm in range(min(cap, t), 0, -16):
        if t % tm == 0:
            return tm
    return t


PREV_ROWS = 8


def _rwkv_in_kernel(h_ref, prev_ref, mu_ref, w_ref, lw_ref, rkv_ref, lora_ref, xs_ref, *, lp, nj, r1):
    i = pl.program_id(0)
    j = pl.program_id(1)
    tm = h_ref.shape[0]

    @pl.when(j == 0)
    def _():
        h = h_ref[...]
        last = prev_ref[PREV_ROWS - 1:, :]
        last = jnp.where(lax.rem(i * tm, lp) == 0, 0.0, last)
        row = lax.broadcasted_iota(jnp.int32, h.shape, 0)
        xx = jnp.where(row == 0, last, pltpu.roll(h, 1, 0)) - h

        def mix(n):
            return (h + xx * mu_ref[n:n + 1, :]).astype(BF16)

        xs_ref[0] = mix(0)
        xs_ref[1] = mix(2)
        xs_ref[2] = mix(3)
        tw = jnp.tanh(_dot(mix(1), lw_ref[:, :r1]))
        ta = _dot(mix(4), lw_ref[:, r1:2 * r1])
        tg = _sigmoid(_dot(mix(5), lw_ref[:, 2 * r1:]))
        lora_ref[...] = jnp.concatenate([tw, ta, tg], axis=1).astype(BF16)

    rkv_ref[...] = _dot(xs_ref[j // nj], w_ref[...])


def _rwkv_in(h, mu, w_rkv, lw, batch, r1, tm=None, tn=1024):
    t, d = h.shape
    lp = t // batch
    tm = tm or _pick_tm(lp, 544)
    nj = d // tn
    nl = lw.shape[1]
    pblk = tm // PREV_ROWS
    return pl.pallas_call(
        functools.partial(_rwkv_in_kernel, lp=lp, nj=nj, r1=r1),
        grid=(t // tm, 3 * nj),
        in_specs=[pl.BlockSpec((tm, d), lambda i, j: (i, 0)),
                  pl.BlockSpec((PREV_ROWS, d), lambda i, j: (jnp.maximum(i * pblk - 1, 0), 0)),
                  pl.BlockSpec((6, d), lambda i, j: (0, 0)),
                  pl.BlockSpec((None, d, tn), lambda i, j: (j // nj, 0, j % nj)),
                  pl.BlockSpec((d, nl), lambda i, j: (0, 0))],
        out_specs=[pl.BlockSpec((None, tm, tn), lambda i, j: (j // nj, i, j % nj)),
                   pl.BlockSpec((tm, nl), lambda i, j: (i, 0))],
        out_shape=[jax.ShapeDtypeStruct((3, t, d), F32), jax.ShapeDtypeStruct((t, nl), BF16)],
        scratch_shapes=[pltpu.VMEM((3, tm, d), BF16)],
        name="rwkv_in",
        compiler_params=_cparams(("parallel", "arbitrary")),
    )(h, h, mu, w_rkv, lw)


def _mm_ln_kernel(x_ref, w_ref, h_ref, g_ref, b_ref, o_ref, *, alpha):
    y = alpha * h_ref[...] + _dot(x_ref[...], w_ref[...])
    o_ref[...] = _layer_norm(y, g_ref[...], b_ref[...])


def _matmul_ln(x, w, h, g, b, alpha, tm=None):
    t, k = x.shape
    n = w.shape[1]
    tm = tm or _pick_tm(t, 512)
    return pl.pallas_call(
        functools.partial(_mm_ln_kernel, alpha=alpha),
        grid=(t // tm,),
        in_specs=[pl.BlockSpec((tm, k), lambda i: (i, 0)),
                  pl.BlockSpec((k, n), lambda i: (0, 0)),
                  pl.BlockSpec((tm, n), lambda i: (i, 0)),
                  pl.BlockSpec((1, n), lambda i: (0, 0)),
                  pl.BlockSpec((1, n), lambda i: (0, 0))],
        out_specs=pl.BlockSpec((tm, n), lambda i: (i, 0)),
        out_shape=jax.ShapeDtypeStruct((t, n), F32),
        name="proj_ln",
        compiler_params=_cparams(("parallel",)),
    )(x, w, h, g.reshape(1, n), b.reshape(1, n))


def _ffn_ln_kernel(h_ref, w1_ref, w3_ref, w2_ref, g_ref, b_ref, o_ref, hb_ref, *, alpha):
    j = pl.program_id(1)

    @pl.when(j == 0)
    def _():
        hb_ref[...] = h_ref[...].astype(BF16)
        o_ref[...] = jnp.zeros_like(o_ref)

    hb = hb_ref[...]
    u = _dot(hb, w1_ref[...])
    v = _dot(hb, w3_ref[...])
    act = (u * _sigmoid(u) * v).astype(BF16)
    o_ref[...] += _dot(act, w2_ref[...])

    @pl.when(j == pl.num_programs(1) - 1)
    def _():
        y = alpha * h_ref[...] + 0.5 * o_ref[...]
        o_ref[...] = _layer_norm(y, g_ref[...], b_ref[...])


def _ffn_ln(h, w1, w3, w2, l, s, g, b, alpha, tm=None, tf=512):
    t, d = h.shape
    f = w1.shape[-1]
    tm = tm or _pick_tm(t, 1088)
    once = pl.Buffered(1)
    return pl.pallas_call(
        functools.partial(_ffn_ln_kernel, alpha=alpha),
        grid=(t // tm, f // tf),
        in_specs=[pl.BlockSpec((tm, d), lambda i, j: (i, 0), pipeline_mode=once),
                  pl.BlockSpec((None, None, d, tf), lambda i, j: (l, s, 0, j)),
                  pl.BlockSpec((None, None, d, tf), lambda i, j: (l, s, 0, j)),
                  pl.BlockSpec((None, None, tf, d), lambda i, j: (l, s, j, 0)),
                  pl.BlockSpec((1, d), lambda i, j: (0, 0)),
                  pl.BlockSpec((1, d), lambda i, j: (0, 0))],
        out_specs=pl.BlockSpec((tm, d), lambda i, j: (i, 0), pipeline_mode=once),
        out_shape=jax.ShapeDtypeStruct((t, d), F32),
        scratch_shapes=[pltpu.VMEM((tm, d), BF16)],
        name="ffn_ln",
        compiler_params=_cparams(("parallel", "arbitrary")),
    )(h, w1, w3, w2, g.reshape(1, d), b.reshape(1, d))


def _cumsum_rows(x):
    n = x.shape[0]
    row = lax.broadcasted_iota(jnp.int32, x.shape, 0)
    s = 1
    while s < n:
        x = x + jnp.where(row >= s, pltpu.roll(x, s, 0), 0.0)
        s *= 2
    return x


def _mm1(a, b):
    return _dot(a.astype(BF16), b.astype(BF16))


def _wkv_chunk(tiles, params, states, c):
    m0, strict, incl, ones_bd, eye = c
    cc = tiles[0][0].shape[0]
    c2 = 2 * cc
    n = len(tiles)
    inv_n = 1.0 / RW_HEAD

    def stack2(x):
        return jnp.concatenate([jnp.where(m0, x, 0.0), jnp.where(m0, 0.0, x)], axis=0)

    def segsum(xs):
        return [_dot(x.astype(BF16), ones_bd) for x in xs]

    kk0 = [t[2] * p[0] for t, p in zip(tiles, params)]
    nrm2 = segsum([x * x for x in kk0])
    kmod = [t[2] * (1.0 + (t[4] - 1.0) * p[1]) for t, p in zip(tiles, params)]
    bon = segsum([t[0] * km * p[2] for t, km, p in zip(tiles, kmod, params)])

    lhs, rhs, vs, upd_r, dec = [], [], [], [], []
    for i in range(n):
        r, wl, k, v, a, g = tiles[i]
        kk = kk0[i] / jnp.maximum(jnp.sqrt(nrm2[i]), 1e-12)
        b = kk * a
        lw = -jnp.exp(wl)
        lcum = _cumsum_rows(lw)
        ltot = lcum[cc - 1:cc, :]
        e_neg = jnp.exp(-lcum)
        e_rem = jnp.exp(ltot - lcum)
        kt = kk * jnp.exp(lcum - lw)
        rt = r * jnp.exp(lcum)
        khb = (kmod[i] * e_neg).astype(BF16)
        bhb = (b * e_neg).astype(BF16)
        lhs.append(jnp.concatenate([stack2(kt), stack2(rt)], axis=0).astype(BF16))
        rhs.append(jnp.concatenate([khb, khb, bhb, bhb, states[i].astype(BF16)], axis=0))
        vs.append(stack2(v))
        upd_r.append(jnp.concatenate([stack2(kmod[i] * e_rem), stack2(b * e_rem)], axis=0).astype(BF16))
        dec.append(jnp.exp(ltot))

    m = [_dot_nt(a_, b_) for a_, b_ in zip(lhs, rhs)]
    a_k = [jnp.where(strict, x[:c2, :c2], 0.0) for x in m]
    a_b = [jnp.where(strict, x[:c2, c2:2 * c2], 0.0) for x in m]
    bkb = [jnp.concatenate([jnp.where(incl, x[c2:, :c2], 0.0), jnp.where(incl, -x[c2:, c2:2 * c2], 0.0)],
                           axis=1).astype(BF16) for x in m]
    rhs_u = [x[:c2, 2 * c2:] + _mm1(ak, v_) for x, ak, v_ in zip(m, a_k, vs)]
    xinv = [eye - ab for ab in a_b]
    pw = [ab.astype(BF16) for ab in a_b]
    pw = [_dot(p_, p_).astype(BF16) for p_ in pw]
    lvl = 4
    while lvl < cc:
        st = [_dot(jnp.concatenate([p_, x.astype(BF16)], axis=0), p_) for p_, x in zip(pw, xinv)]
        pw = [s_[:c2].astype(BF16) for s_ in st]
        xinv = [x + s_[c2:] for x, s_ in zip(xinv, st)]
        lvl *= 2
    xinv = [x + _dot(x.astype(BF16), p_) for x, p_ in zip(xinv, pw)]
    us = [_mm1(x, ru) for x, ru in zip(xinv, rhs_u)]
    ys = [x[c2:, 2 * c2:] + _dot(bk, jnp.concatenate([v_, u_], axis=0).astype(BF16))
          for x, bk, v_, u_ in zip(m, bkb, vs, us)]
    y = [x[:cc] + x[cc:] for x in ys]
    upd_l = [jnp.concatenate([v_, -u_], axis=0).T.astype(BF16) for v_, u_ in zip(vs, us)]
    s_new = [s0 * d_ + _dot(ul, ur) for s0, d_, ul, ur in zip(states, dec, upd_l, upd_r)]

    mu = [x * inv_n for x in segsum(y)]
    yc = [a_ - b_ for a_, b_ in zip(y, mu)]
    var = [x * inv_n for x in segsum([x * x for x in yc])]
    outs = []
    for i in range(n):
        yn = yc[i] * lax.rsqrt(var[i] + RW_GN_EPS) * params[i][3] + params[i][4]
        outs.append((yn + bon[i] * tiles[i][3]) * tiles[i][5])
    return outs, s_new


def _wkv_kernel(r_ref, w_ref, k_ref, v_ref, a_ref, g_ref, kk_ref, ka_ref, rk_ref, lg_ref, lb_ref,
                o_ref, s_ref, *, pairs):
    @pl.when(pl.program_id(2) == 0)
    def _():
        s_ref[...] = jnp.zeros_like(s_ref)

    cc = r_ref.shape[0]
    c2 = 2 * cc
    lane = lax.broadcasted_iota(jnp.int32, (cc, LANES), 1)
    m0 = lane < RW_HEAD
    row = lax.broadcasted_iota(jnp.int32, (c2, c2), 0)
    col = lax.broadcasted_iota(jnp.int32, (c2, c2), 1)
    same = (row >= cc) == (col >= cc)
    strict = same & (col < row)
    incl = same & (col <= row)
    lr = lax.broadcasted_iota(jnp.int32, (LANES, LANES), 0)
    lc = lax.broadcasted_iota(jnp.int32, (LANES, LANES), 1)
    ones_bd = jnp.where((lr >= RW_HEAD) == (lc >= RW_HEAD), 1.0, 0.0).astype(BF16)
    eye = jnp.where(row == col, 1.0, 0.0).astype(F32)
    consts = (m0, strict, incl, ones_bd, eye)

    sls = [slice(p * LANES, (p + 1) * LANES) for p in range(pairs)]
    tiles = [tuple(ref[:, sl] for ref in (r_ref, w_ref, k_ref, v_ref, a_ref, g_ref)) for sl in sls]
    params = [tuple(ref[:, sl] for ref in (kk_ref, ka_ref, rk_ref, lg_ref, lb_ref)) for sl in sls]
    outs, s_new = _wkv_chunk(tiles, params, [s_ref[p] for p in range(pairs)], consts)
    for p in range(pairs):
        s_ref[p] = s_new[p]
        o_ref[:, sls[p]] = outs[p].astype(o_ref.dtype)


def _wkv(rkv, wl, a, g, kkp, kap, rkp, lgp, lbp, batch, out_dtype=BF16):
    _, t, d = rkv.shape
    lp = t // batch
    cc = WKV_CHUNK
    nchunk = lp // cc
    pairs = min(WKV_PAIRS_PER_STEP, d // LANES)
    wblk = pairs * LANES
    tok = pl.BlockSpec((cc, wblk), lambda b, p, c: (b * nchunk + c, p))
    r_spec, k_spec, v_spec = [pl.BlockSpec((None, cc, wblk), lambda b, p, c, n=n: (n, b * nchunk + c, p))
                              for n in range(3)]
    par = pl.BlockSpec((1, wblk), lambda b, p, c: (0, p))
    prm = [x.reshape(1, d) for x in (kkp, kap, rkp, lgp, lbp)]
    return pl.pallas_call(
        functools.partial(_wkv_kernel, pairs=pairs),
        grid=(batch, d // wblk, nchunk),
        in_specs=[r_spec, tok, k_spec, v_spec, tok, tok] + [par] * 5,
        out_specs=tok,
        out_shape=jax.ShapeDtypeStruct((t, d), out_dtype),
        scratch_shapes=[pltpu.VMEM((pairs, LANES, LANES), F32)],
        name="wkv7",
        compiler_params=_cparams(("parallel", "parallel", "arbitrary")),
    )(rkv, wl, rkv, rkv, a, g, *prm)


def _fgate_kernel(h_ref, wh_ref, wl_ref, b_ref, o_ref, carry_ref):
    @pl.when(pl.program_id(1) == 0)
    def _():
        carry_ref[...] = jnp.zeros_like(carry_ref)

    hh, hl = _split(h_ref[...])
    z = _dot(hh, wh_ref[...]) + (_dot(hh, wl_ref[...]) + _dot(hl, wh_ref[...])) + b_ref[...]
    logf = -_softplus(-z)
    cs = _cumsum_rows(logf) + carry_ref[...]
    o_ref[...] = cs
    carry_ref[...] = cs[cs.shape[0] - 1:, :]


def _fgate_cumsum(h, wf, bf, batch, tm=128):
    t, d = h.shape
    lp = t // batch
    nt = lp // tm
    nh = wf.shape[1]
    wpad = jnp.zeros((d, LANES), F32).at[:, :nh].set(wf)
    bpad = jnp.zeros((1, LANES), F32).at[0, :nh].set(bf)
    wh, wl = _split(wpad)
    return pl.pallas_call(
        _fgate_kernel,
        grid=(batch, nt),
        in_specs=[pl.BlockSpec((tm, d), lambda b, i: (b * nt + i, 0)),
                  pl.BlockSpec((d, LANES), lambda b, i: (0, 0)),
                  pl.BlockSpec((d, LANES), lambda b, i: (0, 0)),
                  pl.BlockSpec((1, LANES), lambda b, i: (0, 0))],
        out_specs=pl.BlockSpec((tm, LANES), lambda b, i: (b * nt + i, 0)),
        out_shape=jax.ShapeDtypeStruct((t, LANES), F32),
        scratch_shapes=[pltpu.VMEM((1, LANES), F32)],
        name="fgate_cumsum",
        compiler_params=_cparams(("parallel", "arbitrary")),
    )(h, wh, wl, bpad)


def _fox_kernel(q_ref, k_ref, v_ref, cq_ref, ck_ref, o_ref, *, heads, first, tile, scale):
    lp = q_ref.shape[0]
    nbig = (lp - first) // tile
    hs = [slice(h * FX_HEAD, (h + 1) * FX_HEAD) for h in range(heads)]

    def kv_step(q, cq, carry, kstart, ksize, diag):
        s = [_dot_nt(q[h], k_ref[pl.ds(kstart, ksize), hs[h]]) for h in range(heads)]
        s = [s[h] * scale + cq[h] - ck_ref[0, h, :, pl.ds(kstart, ksize)] for h in range(heads)]
        if diag:
            keep = (lax.broadcasted_iota(jnp.int32, (ksize, ksize), 1)
                    <= lax.broadcasted_iota(jnp.int32, (ksize, ksize), 0))
            s = [jnp.where(keep, x, NEG_INF) for x in s]
        m_new = [jnp.maximum(carry[h][0], jnp.max(s[h], axis=-1, keepdims=True)) for h in range(heads)]
        alpha = [jnp.exp(carry[h][0] - m_new[h]) for h in range(heads)]
        p = [jnp.exp(s[h] - m_new[h]) for h in range(heads)]
        l_new = [alpha[h] * carry[h][1] + jnp.sum(p[h], axis=-1, keepdims=True) for h in range(heads)]
        pv = [_dot(p[h].astype(BF16), v_ref[pl.ds(kstart, ksize), hs[h]]) for h in range(heads)]
        return [(m_new[h], l_new[h], alpha[h] * carry[h][2] + pv[h]) for h in range(heads)]

    def q_tile(qstart, tq, nfull, with_first):
        q = [q_ref[pl.ds(qstart, tq), hs[h]] for h in range(heads)]
        cq = [cq_ref[0, 0, pl.ds(qstart, tq), h:h + 1] for h in range(heads)]
        carry = [(jnp.full((tq, 1), NEG_INF, F32), jnp.zeros((tq, 1), F32), jnp.zeros((tq, FX_HEAD), F32))
                 for _ in range(heads)]
        if with_first:
            carry = kv_step(q, cq, carry, 0, first, False)
        if nfull is not None:
            carry = lax.fori_loop(
                0, nfull,
                lambda j, c: kv_step(q, cq, c, pl.multiple_of(first + j * tile, LANES), tile, False), carry)
        carry = kv_step(q, cq, carry, qstart, tq, True)
        for h in range(heads):
            o_ref[pl.ds(qstart, tq), hs[h]] = (carry[h][2] / carry[h][1]).astype(o_ref.dtype)

    if first:
        q_tile(0, first, None, False)

    def big(i, _):
        q_tile(pl.multiple_of(first + i * tile, LANES), tile, i, first > 0)
        return 0

    lax.fori_loop(0, nbig, big, 0)


def _fox_attention(q, kv, c, batch, heads=FOX_HEADS_PER_STEP, tile=256):
    t, d = q.shape
    lp = t // batch
    nh = d // FX_HEAD
    ng = nh // heads
    wblk = heads * FX_HEAD
    first = lp % tile
    c_col = jnp.transpose(c.reshape(batch, lp, ng, heads), (0, 2, 1, 3))
    c_row = jnp.transpose(c, (0, 2, 1))[:, :, None, :]
    return pl.pallas_call(
        functools.partial(_fox_kernel, heads=heads, first=first, tile=tile, scale=FX_HEAD ** -0.5),
        grid=(batch, ng),
        in_specs=[pl.BlockSpec((lp, wblk), lambda b, g: (b, g)),
                  pl.BlockSpec((lp, wblk), lambda b, g: (b, g)),
                  pl.BlockSpec((lp, wblk), lambda b, g: (b, ng + g)),
                  pl.BlockSpec((1, 1, lp, heads), lambda b, g: (b, g, 0, 0)),
                  pl.BlockSpec((1, heads, 1, lp), lambda b, g: (b, g, 0, 0))],
        out_specs=pl.BlockSpec((lp, wblk), lambda b, g: (b, g)),
        out_shape=jax.ShapeDtypeStruct((t, d), BF16),
        name="fox_attention",
        compiler_params=_cparams(("parallel", "parallel")),
    )(q, kv, kv, c_col, c_row)


def _pad_cols(w, n):
    return jnp.zeros((w.shape[0], n), w.dtype).at[:, :w.shape[1]].set(w)


def _pad_rows(w, n):
    return jnp.zeros((n, w.shape[1]), w.dtype).at[:w.shape[0], :].set(w)


def _rwkv7_mix(h, batch, mu, w_rkv, w_o, w0, w1, w2, a0, a1, a2, g1, g2, k_k, k_a, r_k, lnx_g, lnx_b,
               ln_g, ln_b, alpha):
    r1 = LANES
    lw = jnp.concatenate([_pad_cols(w1, r1), _pad_cols(a1, r1), g1], axis=1).astype(BF16)
    rkv, lora = _rwkv_in(h, mu, w_rkv.astype(BF16), lw, batch, r1)
    wl = _matmul(lora, _pad_rows(w2, r1).astype(BF16), bias=w0, act="decay", xblk=0, name="w_lora2")
    a = _matmul(lora, _pad_rows(a2, r1).astype(BF16), bias=a0, act="sigmoid", xblk=1, name="a_lora2")
    g = _matmul(lora, g2.astype(BF16), xblk=(2 * r1) // g2.shape[0], name="g_lora2")
    o = _wkv(rkv, wl, a, g, k_k, k_a, r_k.reshape(-1), lnx_g, lnx_b, batch)
    return _matmul_ln(o, w_o.astype(BF16), h, ln_g, ln_b, alpha)


def kernel(x, meta_tokens, ln_g, ln_b, ffn_w1, ffn_w3, ffn_w2, rw_mu, rw_w_rkv, rw_w_o, rw_w0, rw_w1,
           rw_w2, rw_a0, rw_a1, rw_a2, rw_g1, rw_g2, rw_k_k, rw_k_a, rw_r_k, rw_lnx_g, rw_lnx_b,
           fx_w_q, fx_w_o, fx_w_kvf, fx_b_f):
    batch, seq, d = x.shape
    depth = ln_g.shape[0]
    n_a = rw_mu.shape[0]
    alpha = (2 * depth) ** 0.25
    l_real = seq + N_META
    lp = -(-l_real // SEQ_ALIGN) * SEQ_ALIGN
    t = batch * lp
    nh = d // FX_HEAD

    meta = jnp.broadcast_to(meta_tokens.astype(x.dtype)[None], (batch, N_META, d))
    h = jnp.concatenate([meta, x, jnp.zeros((batch, lp - l_real, d), x.dtype)], axis=1).reshape(t, d)

    w1b, w3b, w2b = ffn_w1.astype(BF16), ffn_w3.astype(BF16), ffn_w2.astype(BF16)
    kv = c = None
    for l in range(depth):
        h = _ffn_ln(h, w1b, w3b, w2b, l, 0, ln_g[l, 0], ln_b[l, 0], alpha)
        if l < n_a:
            h = _rwkv7_mix(h, batch, rw_mu[l], rw_w_rkv[l], rw_w_o[l], rw_w0[l], rw_w1[l], rw_w2[l],
                           rw_a0[l], rw_a1[l], rw_a2[l], rw_g1[l], rw_g2[l], rw_k_k[l], rw_k_a[l],
                           rw_r_k[l], rw_lnx_g[l], rw_lnx_b[l], ln_g[l, 1], ln_b[l, 1], alpha)
        else:
            j = l - n_a
            q = _matmul(h, fx_w_q[j].astype(BF16), out_dtype=BF16, name="q_proj")
            o = _fox_attention(q, kv, c, batch)
            h = _matmul_ln(o, fx_w_o[j].astype(BF16), h, ln_g[l, 1], ln_b[l, 1], alpha)
        h = _ffn_ln(h, w1b, w3b, w2b, l, 1, ln_g[l, 2], ln_b[l, 2], alpha)
        if l == n_a - 1:
            kv = _matmul(h, fx_w_kvf.astype(BF16), n=2 * d, out_dtype=BF16, name="kv_proj")
            c = _fgate_cumsum(h, fx_w_kvf[:, 2 * d:], fx_b_f, batch)[:, :nh].reshape(batch, lp, nh)
    return h.reshape(batch, lp, d)[:, N_META:l_real]
```

```python
import functools

import jax
import jax.numpy as jnp
from jax import lax
from jax.experimental import pallas as pl
from jax.experimental.pallas import tpu as pltpu

N_META = 16
RW_HEAD = 64
FX_HEAD = 128
LN_EPS = 1e-5
RW_GN_EPS = RW_HEAD * 1e-5
NEG_INF = -1e30
LOG2E = 1.4426950408889634

LANES = 128
SEQ_ALIGN = 128
WKV_CHUNK = 64
WKV_PAIRS_PER_STEP = 16
FOX_HEADS_PER_STEP = 8
VMEM_LIMIT = 56 * 1024 * 1024

F32 = jnp.float32
BF16 = jnp.bfloat16


def _cparams(sem):
    return pltpu.CompilerParams(dimension_semantics=sem, vmem_limit_bytes=VMEM_LIMIT)


def _dot(a, b):
    return jnp.dot(a, b, preferred_element_type=F32)


def _dot_nt(a, b):
    return lax.dot_general(a, b, (((1,), (1,)), ((), ())), preferred_element_type=F32)


def _split(x):
    hi = x.astype(BF16)
    lo = (x - hi.astype(F32)).astype(BF16)
    return hi, lo


def _softplus(x):
    return jnp.maximum(x, 0.0) + jnp.log1p(jnp.exp(-jnp.abs(x)))


def _sigmoid(x):
    return 1.0 / (1.0 + jnp.exp(-x))


def _layer_norm(y, g, b):
    mu = jnp.mean(y, axis=-1, keepdims=True)
    yc = y - mu
    var = jnp.mean(yc * yc, axis=-1, keepdims=True)
    return yc * lax.rsqrt(var + LN_EPS) * g + b


def _mm_kernel(x_ref, w_ref, b_ref, o_ref, *scratch, act, act_arg, transpose_out):
    if scratch:
        xb_ref, = scratch

        @pl.when(pl.program_id(1) == 0)
        def _():
            xb_ref[...] = x_ref[...].astype(BF16)

        x = xb_ref[...]
    else:
        x = x_ref[...]
    acc = _dot(x, w_ref[...]) + b_ref[...]
    if act == "scale":
        acc = acc * act_arg
    if act == "tanh":
        acc = jnp.tanh(acc)
    elif act == "sigmoid":
        acc = _sigmoid(acc)
    elif act == "decay":
        acc = -_softplus(-acc) - 0.5
    if transpose_out:
        acc = acc.T
    o_ref[...] = acc.astype(o_ref.dtype)


def _matmul(x, w, bias=None, act=None, act_arg=None, out_dtype=F32, tm=None, tn=None, xblk=0, n=None, woff=0,
            transpose_out=False, name="matmul"):
    t = x.shape[0]
    k = w.shape[0]
    n = n or w.shape[1]
    tm = tm or _pick_tm(t)
    tn = tn or min(n, 512)
    if bias is None:
        bias = jnp.zeros((1, n), F32)
    cast = x.dtype != BF16
    jo = woff // tn
    if transpose_out:
        out_spec, out_shape = pl.BlockSpec((tn, tm), lambda i, j: (j, i)), (n, t)
    else:
        out_spec, out_shape = pl.BlockSpec((tm, tn), lambda i, j: (i, j)), (t, n)
    return pl.pallas_call(
        functools.partial(_mm_kernel, act=act, act_arg=act_arg, transpose_out=transpose_out),
        grid=(t // tm, n // tn),
        in_specs=[pl.BlockSpec((tm, k), lambda i, j: (i, xblk)),
                  pl.BlockSpec((k, tn), lambda i, j: (0, j + jo)),
                  pl.BlockSpec((1, tn), lambda i, j: (0, j))],
        out_specs=out_spec,
        out_shape=jax.ShapeDtypeStruct(out_shape, out_dtype),
        scratch_shapes=[pltpu.VMEM((tm, k), BF16)] if cast else [],
        name=name,
        compiler_params=_cparams(("parallel", "arbitrary" if cast else "parallel")),
    )(x, w, bias.reshape(1, n).astype(F32))


def _pick_tm(t, cap=1088):
    for tm in range(min(cap, t), 0, -16):
        if t % tm == 0:
            return tm
    return t


PREV_ROWS = 8


def _rwkv_in_kernel(h_ref, prev_ref, mu_ref, w_ref, lw_ref, rkv_ref, lora_ref, xs_ref, *, lp, nj, r1):
    i = pl.program_id(0)
    j = pl.program_id(1)
    tm = h_ref.shape[0]

    @pl.when(j == 0)
    def _():
        h = h_ref[...]
        last = prev_ref[PREV_ROWS - 1:, :]
        last = jnp.where(lax.rem(i * tm, lp) == 0, 0.0, last)
        row = lax.broadcasted_iota(jnp.int32, h.shape, 0)
        xx = jnp.where(row == 0, last, pltpu.roll(h, 1, 0)) - h

        def mix(n):
            return (h + xx * mu_ref[n:n + 1, :]).astype(BF16)

        xs_ref[0] = mix(0)
        xs_ref[1] = mix(2)
        xs_ref[2] = mix(3)
        tw = jnp.tanh(_dot(mix(1), lw_ref[:, :r1]))
        ta = _dot(mix(4), lw_ref[:, r1:2 * r1])
        tg = _sigmoid(_dot(mix(5), lw_ref[:, 2 * r1:]))
        lora_ref[...] = jnp.concatenate([tw, ta, tg], axis=1).astype(BF16)

    rkv_ref[...] = _dot(xs_ref[j // nj], w_ref[...])


def _rwkv_in(h, mu, w_rkv, lw, batch, r1, tm=None, tn=1024):
    t, d = h.shape
    lp = t // batch
    tm = tm or _pick_tm(lp, 544)
    nj = d // tn
    nl = lw.shape[1]
    pblk = tm // PREV_ROWS
    return pl.pallas_call(
        functools.partial(_rwkv_in_kernel, lp=lp, nj=nj, r1=r1),
        grid=(t // tm, 3 * nj),
        in_specs=[pl.BlockSpec((tm, d), lambda i, j: (i, 0)),
                  pl.BlockSpec((PREV_ROWS, d), lambda i, j: (jnp.maximum(i * pblk - 1, 0), 0)),
                  pl.BlockSpec((6, d), lambda i, j: (0, 0)),
                  pl.BlockSpec((None, d, tn), lambda i, j: (j // nj, 0, j % nj)),
                  pl.BlockSpec((d, nl), lambda i, j: (0, 0))],
        out_specs=[pl.BlockSpec((None, tm, tn), lambda i, j: (j // nj, i, j % nj)),
                   pl.BlockSpec((tm, nl), lambda i, j: (i, 0))],
        out_shape=[jax.ShapeDtypeStruct((3, t, d), F32), jax.ShapeDtypeStruct((t, nl), BF16)],
        scratch_shapes=[pltpu.VMEM((3, tm, d), BF16)],
        name="rwkv_in",
        compiler_params=_cparams(("parallel", "arbitrary")),
    )(h, h, mu, w_rkv, lw)


def _mm_ln_kernel(x_ref, w_ref, h_ref, g_ref, b_ref, o_ref, *, alpha):
    y = alpha * h_ref[...] + _dot(x_ref[...], w_ref[...])
    o_ref[...] = _layer_norm(y, g_ref[...], b_ref[...])


def _matmul_ln(x, w, h, g, b, alpha, tm=None):
    t, k = x.shape
    n = w.shape[1]
    tm = tm or _pick_tm(t, 512)
    return pl.pallas_call(
        functools.partial(_mm_ln_kernel, alpha=alpha),
        grid=(t // tm,),
        in_specs=[pl.BlockSpec((tm, k), lambda i: (i, 0)),
                  pl.BlockSpec((k, n), lambda i: (0, 0)),
                  pl.BlockSpec((tm, n), lambda i: (i, 0)),
                  pl.BlockSpec((1, n), lambda i: (0, 0)),
                  pl.BlockSpec((1, n), lambda i: (0, 0))],
        out_specs=pl.BlockSpec((tm, n), lambda i: (i, 0)),
        out_shape=jax.ShapeDtypeStruct((t, n), F32),
        name="proj_ln",
        compiler_params=_cparams(("parallel",)),
    )(x, w, h, g.reshape(1, n), b.reshape(1, n))


def _ffn_ln_kernel(h_ref, w1_ref, w3_ref, w2_ref, g_ref, b_ref, o_ref, hb_ref, *, alpha):
    j = pl.program_id(1)

    @pl.when(j == 0)
    def _():
        hb_ref[...] = h_ref[...].astype(BF16)
        o_ref[...] = jnp.zeros_like(o_ref)

    hb = hb_ref[...]
    u = _dot(hb, w1_ref[...])
    v = _dot(hb, w3_ref[...])
    act = (u * _sigmoid(u) * v).astype(BF16)
    o_ref[...] += _dot(act, w2_ref[...])

    @pl.when(j == pl.num_programs(1) - 1)
    def _():
        y = alpha * h_ref[...] + 0.5 * o_ref[...]
        o_ref[...] = _layer_norm(y, g_ref[...], b_ref[...])


def _ffn_ln(h, w1, w3, w2, l, s, g, b, alpha, tm=None, tf=512):
    t, d = h.shape
    f = w1.shape[-1]
    tm = tm or _pick_tm(t, 544)
    return pl.pallas_call(
        functools.partial(_ffn_ln_kernel, alpha=alpha),
        grid=(t // tm, f // tf),
        in_specs=[pl.BlockSpec((tm, d), lambda i, j: (i, 0)),
                  pl.BlockSpec((None, None, d, tf), lambda i, j: (l, s, 0, j)),
                  pl.BlockSpec((None, None, d, tf), lambda i, j: (l, s, 0, j)),
                  pl.BlockSpec((None, None, tf, d), lambda i, j: (l, s, j, 0)),
                  pl.BlockSpec((1, d), lambda i, j: (0, 0)),
                  pl.BlockSpec((1, d), lambda i, j: (0, 0))],
        out_specs=pl.BlockSpec((tm, d), lambda i, j: (i, 0)),
        out_shape=jax.ShapeDtypeStruct((t, d), F32),
        scratch_shapes=[pltpu.VMEM((tm, d), BF16)],
        name="ffn_ln",
        compiler_params=_cparams(("parallel", "arbitrary")),
    )(h, w1, w3, w2, g.reshape(1, d), b.reshape(1, d))


def _cumsum_rows(x):
    n = x.shape[0]
    row = lax.broadcasted_iota(jnp.int32, x.shape, 0)
    s = 1
    while s < n:
        x = x + jnp.where(row >= s, pltpu.roll(x, s, 0), 0.0)
        s *= 2
    return x


def _mm1(a, b):
    return _dot(a.astype(BF16), b.astype(BF16))


def _wkv_chunk(tiles, params, states, c):
    m0, strict, incl, ones_bd, eye = c
    cc = tiles[0][0].shape[0]
    c2 = 2 * cc
    n = len(tiles)
    inv_n = 1.0 / RW_HEAD

    def stack2(x):
        return jnp.concatenate([jnp.where(m0, x, 0.0), jnp.where(m0, 0.0, x)], axis=0)

    def segsum(xs):
        return [_dot(x.astype(BF16), ones_bd) for x in xs]

    kk0 = [t[2] * p[0] for t, p in zip(tiles, params)]
    nrm2 = segsum([x * x for x in kk0])
    kmod = [t[2] * (1.0 + (t[4] - 1.0) * p[1]) for t, p in zip(tiles, params)]
    bon = segsum([t[0] * km * p[2] for t, km, p in zip(tiles, kmod, params)])

    lhs, rhs, vs, upd_r, dec = [], [], [], [], []
    for i in range(n):
        r, wl, k, v, a, g = tiles[i]
        kk = kk0[i] / jnp.maximum(jnp.sqrt(nrm2[i]), 1e-12)
        b = kk * a
        lw = -jnp.exp(wl)
        lcum = _cumsum_rows(lw)
        ltot = lcum[cc - 1:cc, :]
        e_neg = jnp.exp(-lcum)
        e_rem = jnp.exp(ltot - lcum)
        kt = kk * jnp.exp(lcum - lw)
        rt = r * jnp.exp(lcum)
        khb = (kmod[i] * e_neg).astype(BF16)
        bhb = (b * e_neg).astype(BF16)
        lhs.append(jnp.concatenate([stack2(kt), stack2(rt)], axis=0).astype(BF16))
        rhs.append(jnp.concatenate([khb, khb, bhb, bhb, states[i].astype(BF16)], axis=0))
        vs.append(stack2(v))
        upd_r.append(jnp.concatenate([stack2(kmod[i] * e_rem), stack2(b * e_rem)], axis=0).astype(BF16))
        dec.append(jnp.exp(ltot))

    m = [_dot_nt(a_, b_) for a_, b_ in zip(lhs, rhs)]
    a_k = [jnp.where(strict, x[:c2, :c2], 0.0) for x in m]
    a_b = [jnp.where(strict, x[:c2, c2:2 * c2], 0.0) for x in m]
    bkb = [jnp.concatenate([jnp.where(incl, x[c2:, :c2], 0.0), jnp.where(incl, -x[c2:, c2:2 * c2], 0.0)],
                           axis=1).astype(BF16) for x in m]
    rhs_u = [x[:c2, 2 * c2:] + _mm1(ak, v_) for x, ak, v_ in zip(m, a_k, vs)]
    xinv = [eye - ab for ab in a_b]
    pw = [ab.astype(BF16) for ab in a_b]
    pw = [_dot(p_, p_).astype(BF16) for p_ in pw]
    lvl = 4
    while lvl < cc:
        st = [_dot(jnp.concatenate([p_, x.astype(BF16)], axis=0), p_) for p_, x in zip(pw, xinv)]
        pw = [s_[:c2].astype(BF16) for s_ in st]
        xinv = [x + s_[c2:] for x, s_ in zip(xinv, st)]
        lvl *= 2
    xinv = [x + _dot(x.astype(BF16), p_) for x, p_ in zip(xinv, pw)]
    us = [_mm1(x, ru) for x, ru in zip(xinv, rhs_u)]
    ys = [x[c2:, 2 * c2:] + _dot(bk, jnp.concatenate([v_, u_], axis=0).astype(BF16))
          for x, bk, v_, u_ in zip(m, bkb, vs, us)]
    y = [x[:cc] + x[cc:] for x in ys]
    upd_l = [jnp.concatenate([v_, -u_], axis=0).T.astype(BF16) for v_, u_ in zip(vs, us)]
    s_new = [s0 * d_ + _dot(ul, ur) for s0, d_, ul, ur in zip(states, dec, upd_l, upd_r)]

    mu = [x * inv_n for x in segsum(y)]
    yc = [a_ - b_ for a_, b_ in zip(y, mu)]
    var = [x * inv_n for x in segsum([x * x for x in yc])]
    outs = []
    for i in range(n):
        yn = yc[i] * lax.rsqrt(var[i] + RW_GN_EPS) * params[i][3] + params[i][4]
        outs.append((yn + bon[i] * tiles[i][3]) * tiles[i][5])
    return outs, s_new


def _wkv_kernel(r_ref, w_ref, k_ref, v_ref, a_ref, g_ref, kk_ref, ka_ref, rk_ref, lg_ref, lb_ref,
                o_ref, s_ref, *, pairs):
    @pl.when(pl.program_id(2) == 0)
    def _():
        s_ref[...] = jnp.zeros_like(s_ref)

    cc = r_ref.shape[0]
    c2 = 2 * cc
    lane = lax.broadcasted_iota(jnp.int32, (cc, LANES), 1)
    m0 = lane < RW_HEAD
    row = lax.broadcasted_iota(jnp.int32, (c2, c2), 0)
    col = lax.broadcasted_iota(jnp.int32, (c2, c2), 1)
    same = (row >= cc) == (col >= cc)
    strict = same & (col < row)
    incl = same & (col <= row)
    lr = lax.broadcasted_iota(jnp.int32, (LANES, LANES), 0)
    lc = lax.broadcasted_iota(jnp.int32, (LANES, LANES), 1)
    ones_bd = jnp.where((lr >= RW_HEAD) == (lc >= RW_HEAD), 1.0, 0.0).astype(BF16)
    eye = jnp.where(row == col, 1.0, 0.0).astype(F32)
    consts = (m0, strict, incl, ones_bd, eye)

    sls = [slice(p * LANES, (p + 1) * LANES) for p in range(pairs)]
    tiles = [tuple(ref[:, sl] for ref in (r_ref, w_ref, k_ref, v_ref, a_ref, g_ref)) for sl in sls]
    params = [tuple(ref[:, sl] for ref in (kk_ref, ka_ref, rk_ref, lg_ref, lb_ref)) for sl in sls]
    outs, s_new = _wkv_chunk(tiles, params, [s_ref[p] for p in range(pairs)], consts)
    for p in range(pairs):
        s_ref[p] = s_new[p]
        o_ref[:, sls[p]] = outs[p].astype(o_ref.dtype)


def _wkv(rkv, wl, a, g, kkp, kap, rkp, lgp, lbp, batch, out_dtype=BF16):
    _, t, d = rkv.shape
    lp = t // batch
    cc = WKV_CHUNK
    nchunk = lp // cc
    pairs = min(WKV_PAIRS_PER_STEP, d // LANES)
    wblk = pairs * LANES
    tok = pl.BlockSpec((cc, wblk), lambda b, p, c: (b * nchunk + c, p))
    r_spec, k_spec, v_spec = [pl.BlockSpec((None, cc, wblk), lambda b, p, c, n=n: (n, b * nchunk + c, p))
                              for n in range(3)]
    par = pl.BlockSpec((1, wblk), lambda b, p, c: (0, p))
    prm = [x.reshape(1, d) for x in (kkp, kap, rkp, lgp, lbp)]
    return pl.pallas_call(
        functools.partial(_wkv_kernel, pairs=pairs),
        grid=(batch, d // wblk, nchunk),
        in_specs=[r_spec, tok, k_spec, v_spec, tok, tok] + [par] * 5,
        out_specs=tok,
        out_shape=jax.ShapeDtypeStruct((t, d), out_dtype),
        scratch_shapes=[pltpu.VMEM((pairs, LANES, LANES), F32)],
        name="wkv7",
        compiler_params=_cparams(("parallel", "parallel", "arbitrary")),
    )(rkv, wl, rkv, rkv, a, g, *prm)


def _fgate_kernel(h_ref, wh_ref, wl_ref, b_ref, o_ref, carry_ref):
    @pl.when(pl.program_id(1) == 0)
    def _():
        carry_ref[...] = jnp.zeros_like(carry_ref)

    hh, hl = _split(h_ref[...])
    z = _dot(hh, wh_ref[...]) + (_dot(hh, wl_ref[...]) + _dot(hl, wh_ref[...])) + b_ref[...]
    logf = -_softplus(-z) * LOG2E
    cs = _cumsum_rows(logf) + carry_ref[...]
    o_ref[...] = cs
    carry_ref[...] = cs[cs.shape[0] - 1:, :]


def _fgate_cumsum(h, wf, bf, batch, tm=128):
    t, d = h.shape
    lp = t // batch
    nt = lp // tm
    nh = wf.shape[1]
    wpad = jnp.zeros((d, LANES), F32).at[:, :nh].set(wf)
    bpad = jnp.zeros((1, LANES), F32).at[0, :nh].set(bf)
    wh, wl = _split(wpad)
    return pl.pallas_call(
        _fgate_kernel,
        grid=(batch, nt),
        in_specs=[pl.BlockSpec((tm, d), lambda b, i: (b * nt + i, 0)),
                  pl.BlockSpec((d, LANES), lambda b, i: (0, 0)),
                  pl.BlockSpec((d, LANES), lambda b, i: (0, 0)),
                  pl.BlockSpec((1, LANES), lambda b, i: (0, 0))],
        out_specs=pl.BlockSpec((tm, LANES), lambda b, i: (b * nt + i, 0)),
        out_shape=jax.ShapeDtypeStruct((t, LANES), F32),
        scratch_shapes=[pltpu.VMEM((1, LANES), F32)],
        name="fgate_cumsum",
        compiler_params=_cparams(("parallel", "arbitrary")),
    )(h, wh, wl, bpad)


def _fox_kernel(q_ref, k_ref, vt_ref, cc_ref, cr_ref, o_ref, *, heads, first, tile):
    lp = q_ref.shape[0]
    nbig = (lp - first) // tile
    hs = [slice(h * FX_HEAD, (h + 1) * FX_HEAD) for h in range(heads)]

    def scores(q, kstart, ksize):
        return [_dot_nt(k_ref[pl.ds(kstart, ksize), hs[h]], q[h]) for h in range(heads)]

    def update(s, cq, carry, kstart, ksize, diag):
        s = [s[h] + cq[h] - cc_ref[0, 0, pl.ds(kstart, ksize), h:h + 1] for h in range(heads)]
        if diag:
            keep = (lax.broadcasted_iota(jnp.int32, (ksize, ksize), 0)
                    <= lax.broadcasted_iota(jnp.int32, (ksize, ksize), 1))
            s = [jnp.where(keep, x, NEG_INF) for x in s]
        m_new = [jnp.maximum(carry[h][0], jnp.max(s[h], axis=0, keepdims=True)) for h in range(heads)]
        alpha = [jnp.exp2(carry[h][0] - m_new[h]) for h in range(heads)]
        p = [jnp.exp2(s[h] - m_new[h]) for h in range(heads)]
        l_new = [alpha[h] * carry[h][1] + jnp.sum(p[h], axis=0, keepdims=True) for h in range(heads)]
        pv = [_dot(vt_ref[hs[h], pl.ds(kstart, ksize)], p[h].astype(BF16)) for h in range(heads)]
        return [(m_new[h], l_new[h], alpha[h] * carry[h][2] + pv[h]) for h in range(heads)]

    def q_tile(qstart, tq, nfull, with_first):
        q = [q_ref[pl.ds(qstart, tq), hs[h]] for h in range(heads)]
        cq = [cr_ref[0, h, :, pl.ds(qstart, tq)] for h in range(heads)]
        carry = [(jnp.full((1, tq), NEG_INF, F32), jnp.zeros((1, tq), F32), jnp.zeros((FX_HEAD, tq), F32))
                 for _ in range(heads)]
        def step(c, kstart, ksize, diag):
            return update(scores(q, kstart, ksize), cq, c, kstart, ksize, diag)

        def step2(j, c):
            k0 = pl.multiple_of(first + 2 * j * tile, LANES)
            k1 = pl.multiple_of(first + (2 * j + 1) * tile, LANES)
            s0, s1 = scores(q, k0, tile), scores(q, k1, tile)
            return update(s1, cq, update(s0, cq, c, k0, tile, False), k1, tile, False)

        if nfull is not None:
            carry = lax.fori_loop(0, lax.shift_right_logical(nfull, 1), step2, carry)
            carry = lax.fori_loop(
                0, nfull & 1,
                lambda _, c: step(c, pl.multiple_of(first + (nfull - 1) * tile, LANES), tile, False), carry)
        s_first = scores(q, 0, first) if with_first else None
        s_diag = scores(q, qstart, tq)
        if with_first:
            carry = update(s_first, cq, carry, 0, first, False)
        carry = update(s_diag, cq, carry, qstart, tq, True)
        for h in range(heads):
            o_ref[pl.ds(qstart, tq), hs[h]] = (carry[h][2] / carry[h][1]).T.astype(o_ref.dtype)

    if first:
        q_tile(0, first, None, False)

    def big(i, _):
        q_tile(pl.multiple_of(first + i * tile, LANES), tile, i, first > 0)
        return 0

    lax.fori_loop(0, nbig, big, 0)


FOX_Q_SCALE = FX_HEAD ** -0.5 * LOG2E


def _fox_attention(q, k, vt, c, batch, heads=FOX_HEADS_PER_STEP, tile=256):
    t, d = q.shape
    lp = t // batch
    nh = d // FX_HEAD
    ng = nh // heads
    wblk = heads * FX_HEAD
    first = lp % tile
    c_col = jnp.transpose(c.reshape(batch, lp, ng, heads), (0, 2, 1, 3))
    c_row = jnp.transpose(c, (0, 2, 1))[:, :, None, :]
    return pl.pallas_call(
        functools.partial(_fox_kernel, heads=heads, first=first, tile=tile),
        grid=(batch, ng),
        in_specs=[pl.BlockSpec((lp, wblk), lambda b, g: (b, g)),
                  pl.BlockSpec((lp, wblk), lambda b, g: (b, g)),
                  pl.BlockSpec((wblk, lp), lambda b, g: (g, b)),
                  pl.BlockSpec((1, 1, lp, heads), lambda b, g: (b, g, 0, 0)),
                  pl.BlockSpec((1, heads, 1, lp), lambda b, g: (b, g, 0, 0))],
        out_specs=pl.BlockSpec((lp, wblk), lambda b, g: (b, g)),
        out_shape=jax.ShapeDtypeStruct((t, d), BF16),
        name="fox_attention",
        compiler_params=_cparams(("parallel", "parallel")),
    )(q, k, vt, c_col, c_row)


def _pad_cols(w, n):
    return jnp.zeros((w.shape[0], n), w.dtype).at[:, :w.shape[1]].set(w)


def _pad_rows(w, n):
    return jnp.zeros((n, w.shape[1]), w.dtype).at[:w.shape[0], :].set(w)


def _rwkv7_mix(h, batch, mu, w_rkv, w_o, w0, w1, w2, a0, a1, a2, g1, g2, k_k, k_a, r_k, lnx_g, lnx_b,
               ln_g, ln_b, alpha):
    r1 = LANES
    lw = jnp.concatenate([_pad_cols(w1, r1), _pad_cols(a1, r1), g1], axis=1).astype(BF16)
    rkv, lora = _rwkv_in(h, mu, w_rkv.astype(BF16), lw, batch, r1)
    wl = _matmul(lora, _pad_rows(w2, r1).astype(BF16), bias=w0, act="decay", xblk=0, name="w_lora2")
    a = _matmul(lora, _pad_rows(a2, r1).astype(BF16), bias=a0, act="sigmoid", xblk=1, name="a_lora2")
    g = _matmul(lora, g2.astype(BF16), xblk=(2 * r1) // g2.shape[0], name="g_lora2")
    o = _wkv(rkv, wl, a, g, k_k, k_a, r_k.reshape(-1), lnx_g, lnx_b, batch)
    return _matmul_ln(o, w_o.astype(BF16), h, ln_g, ln_b, alpha)


def kernel(x, meta_tokens, ln_g, ln_b, ffn_w1, ffn_w3, ffn_w2, rw_mu, rw_w_rkv, rw_w_o, rw_w0, rw_w1,
           rw_w2, rw_a0, rw_a1, rw_a2, rw_g1, rw_g2, rw_k_k, rw_k_a, rw_r_k, rw_lnx_g, rw_lnx_b,
           fx_w_q, fx_w_o, fx_w_kvf, fx_b_f):
    batch, seq, d = x.shape
    depth = ln_g.shape[0]
    n_a = rw_mu.shape[0]
    alpha = (2 * depth) ** 0.25
    l_real = seq + N_META
    lp = -(-l_real // SEQ_ALIGN) * SEQ_ALIGN
    t = batch * lp
    nh = d // FX_HEAD

    meta = jnp.broadcast_to(meta_tokens.astype(x.dtype)[None], (batch, N_META, d))
    h = jnp.concatenate([meta, x, jnp.zeros((batch, lp - l_real, d), x.dtype)], axis=1).reshape(t, d)

    w1b, w3b, w2b = ffn_w1.astype(BF16), ffn_w3.astype(BF16), ffn_w2.astype(BF16)
    k_s = vt_s = c = None
    for l in range(depth):
        h = _ffn_ln(h, w1b, w3b, w2b, l, 0, ln_g[l, 0], ln_b[l, 0], alpha)
        if l < n_a:
            h = _rwkv7_mix(h, batch, rw_mu[l], rw_w_rkv[l], rw_w_o[l], rw_w0[l], rw_w1[l], rw_w2[l],
                           rw_a0[l], rw_a1[l], rw_a2[l], rw_g1[l], rw_g2[l], rw_k_k[l], rw_k_a[l],
                           rw_r_k[l], rw_lnx_g[l], rw_lnx_b[l], ln_g[l, 1], ln_b[l, 1], alpha)
        else:
            j = l - n_a
            q = _matmul(h, fx_w_q[j].astype(BF16), act="scale", act_arg=FOX_Q_SCALE, out_dtype=BF16,
                        name="q_proj")
            o = _fox_attention(q, k_s, vt_s, c, batch)
            h = _matmul_ln(o, fx_w_o[j].astype(BF16), h, ln_g[l, 1], ln_b[l, 1], alpha)
        h = _ffn_ln(h, w1b, w3b, w2b, l, 1, ln_g[l, 2], ln_b[l, 2], alpha)
        if l == n_a - 1:
            w_kvf = fx_w_kvf.astype(BF16)
            k_s = _matmul(h, w_kvf, n=d, out_dtype=BF16, name="k_proj")
            vt_s = _matmul(h, w_kvf, n=d, woff=d, out_dtype=BF16, tm=512, transpose_out=True, name="vt_proj")
            c = _fgate_cumsum(h, fx_w_kvf[:, 2 * d:], fx_b_f, batch)[:, :nh].reshape(batch, lp, nh)
    return h.reshape(batch, lp, d)[:, N_META:l_real]
```

```python
import functools

import jax
import jax.numpy as jnp
from jax import lax
from jax.experimental import pallas as pl
from jax.experimental.pallas import tpu as pltpu

N_META = 16
RW_HEAD = 64
FX_HEAD = 128
LN_EPS = 1e-5
RW_GN_EPS = RW_HEAD * 1e-5
NEG_INF = -1e30
LOG2E = 1.4426950408889634

LANES = 128
SEQ_ALIGN = 128
WKV_CHUNK = 64
WKV_PAIRS_PER_STEP = 16
FOX_HEADS_PER_STEP = 8
VMEM_LIMIT = 56 * 1024 * 1024

F32 = jnp.float32
BF16 = jnp.bfloat16


def _cparams(sem):
    return pltpu.CompilerParams(dimension_semantics=sem, vmem_limit_bytes=VMEM_LIMIT)


def _dot(a, b):
    return jnp.dot(a, b, preferred_element_type=F32)


def _dot_nt(a, b):
    return lax.dot_general(a, b, (((1,), (1,)), ((), ())), preferred_element_type=F32)


def _split(x):
    hi = x.astype(BF16)
    lo = (x - hi.astype(F32)).astype(BF16)
    return hi, lo


def _softplus(x):
    return jnp.maximum(x, 0.0) + jnp.log1p(jnp.exp(-jnp.abs(x)))


def _sigmoid(x):
    return 1.0 / (1.0 + jnp.exp(-x))


def _layer_norm(y, g, b):
    mu = jnp.mean(y, axis=-1, keepdims=True)
    yc = y - mu
    var = jnp.mean(yc * yc, axis=-1, keepdims=True)
    return yc * lax.rsqrt(var + LN_EPS) * g + b


def _mm_kernel(x_ref, w_ref, b_ref, o_ref, *scratch, act, act_arg, transpose_out):
    if scratch:
        xb_ref, = scratch

        @pl.when(pl.program_id(1) == 0)
        def _():
            xb_ref[...] = x_ref[...].astype(BF16)

        x = xb_ref[...]
    else:
        x = x_ref[...]
    acc = _dot(x, w_ref[...]) + b_ref[...]
    if act == "scale":
        acc = acc * act_arg
    if act == "tanh":
        acc = jnp.tanh(acc)
    elif act == "sigmoid":
        acc = _sigmoid(acc)
    elif act == "decay":
        acc = -_softplus(-acc) - 0.5
    if transpose_out:
        acc = acc.T
    o_ref[...] = acc.astype(o_ref.dtype)


def _matmul(x, w, bias=None, act=None, act_arg=None, out_dtype=F32, tm=None, tn=None, xblk=0, n=None, woff=0,
            transpose_out=False, name="matmul"):
    t = x.shape[0]
    k = w.shape[0]
    n = n or w.shape[1]
    tm = tm or _pick_tm(t)
    tn = tn or min(n, 512)
    if bias is None:
        bias = jnp.zeros((1, n), F32)
    cast = x.dtype != BF16
    jo = woff // tn
    if transpose_out:
        out_spec, out_shape = pl.BlockSpec((tn, tm), lambda i, j: (j, i)), (n, t)
    else:
        out_spec, out_shape = pl.BlockSpec((tm, tn), lambda i, j: (i, j)), (t, n)
    return pl.pallas_call(
        functools.partial(_mm_kernel, act=act, act_arg=act_arg, transpose_out=transpose_out),
        grid=(t // tm, n // tn),
        in_specs=[pl.BlockSpec((tm, k), lambda i, j: (i, xblk)),
                  pl.BlockSpec((k, tn), lambda i, j: (0, j + jo)),
                  pl.BlockSpec((1, tn), lambda i, j: (0, j))],
        out_specs=out_spec,
        out_shape=jax.ShapeDtypeStruct(out_shape, out_dtype),
        scratch_shapes=[pltpu.VMEM((tm, k), BF16)] if cast else [],
        name=name,
        compiler_params=_cparams(("parallel", "arbitrary" if cast else "parallel")),
    )(x, w, bias.reshape(1, n).astype(F32))


def _pick_tm(t, cap=1088):
    for tm in range(min(cap, t), 0, -16):
        if t % tm == 0:
            return tm
    return t


PREV_ROWS = 8


def _rwkv_in_kernel(h_ref, prev_ref, mu_ref, w_ref, lw_ref, rkv_ref, lora_ref, xs_ref, *, lp, nj, r1):
    i = pl.program_id(0)
    j = pl.program_id(1)
    tm = h_ref.shape[0]

    @pl.when(j == 0)
    def _():
        h = h_ref[...]
        last = prev_ref[PREV_ROWS - 1:, :]
        last = jnp.where(lax.rem(i * tm, lp) == 0, 0.0, last)
        row = lax.broadcasted_iota(jnp.int32, h.shape, 0)
        xx = jnp.where(row == 0, last, pltpu.roll(h, 1, 0)) - h

        def mix(n):
            return (h + xx * mu_ref[n:n + 1, :]).astype(BF16)

        xs_ref[0] = mix(0)
        xs_ref[1] = mix(2)
        xs_ref[2] = mix(3)
        tw = jnp.tanh(_dot(mix(1), lw_ref[:, :r1]))
        ta = _dot(mix(4), lw_ref[:, r1:2 * r1])
        tg = _sigmoid(_dot(mix(5), lw_ref[:, 2 * r1:]))
        lora_ref[...] = jnp.concatenate([tw, ta, tg], axis=1).astype(BF16)

    rkv_ref[...] = _dot(xs_ref[j // nj], w_ref[...])


def _rwkv_in(h, mu, w_rkv, lw, batch, r1, tm=None, tn=1024):
    t, d = h.shape
    lp = t // batch
    tm = tm or _pick_tm(lp, 544)
    nj = d // tn
    nl = lw.shape[1]
    pblk = tm // PREV_ROWS
    return pl.pallas_call(
        functools.partial(_rwkv_in_kernel, lp=lp, nj=nj, r1=r1),
        grid=(t // tm, 3 * nj),
        in_specs=[pl.BlockSpec((tm, d), lambda i, j: (i, 0)),
                  pl.BlockSpec((PREV_ROWS, d), lambda i, j: (jnp.maximum(i * pblk - 1, 0), 0)),
                  pl.BlockSpec((6, d), lambda i, j: (0, 0)),
                  pl.BlockSpec((None, d, tn), lambda i, j: (j // nj, 0, j % nj)),
                  pl.BlockSpec((d, nl), lambda i, j: (0, 0))],
        out_specs=[pl.BlockSpec((None, tm, tn), lambda i, j: (j // nj, i, j % nj)),
                   pl.BlockSpec((tm, nl), lambda i, j: (i, 0))],
        out_shape=[jax.ShapeDtypeStruct((3, t, d), F32), jax.ShapeDtypeStruct((t, nl), BF16)],
        scratch_shapes=[pltpu.VMEM((3, tm, d), BF16)],
        name="rwkv_in",
        compiler_params=_cparams(("parallel", "arbitrary")),
    )(h, h, mu, w_rkv, lw)


def _mm_ln_kernel(x_ref, w_ref, h_ref, g_ref, b_ref, o_ref, *, alpha):
    y = alpha * h_ref[...] + _dot(x_ref[...], w_ref[...])
    o_ref[...] = _layer_norm(y, g_ref[...], b_ref[...])


def _matmul_ln(x, w, h, g, b, alpha, tm=None):
    t, k = x.shape
    n = w.shape[1]
    tm = tm or _pick_tm(t, 512)
    return pl.pallas_call(
        functools.partial(_mm_ln_kernel, alpha=alpha),
        grid=(t // tm,),
        in_specs=[pl.BlockSpec((tm, k), lambda i: (i, 0)),
                  pl.BlockSpec((k, n), lambda i: (0, 0)),
                  pl.BlockSpec((tm, n), lambda i: (i, 0)),
                  pl.BlockSpec((1, n), lambda i: (0, 0)),
                  pl.BlockSpec((1, n), lambda i: (0, 0))],
        out_specs=pl.BlockSpec((tm, n), lambda i: (i, 0)),
        out_shape=jax.ShapeDtypeStruct((t, n), F32),
        name="proj_ln",
        compiler_params=_cparams(("parallel",)),
    )(x, w, h, g.reshape(1, n), b.reshape(1, n))


def _ffn_ln_kernel(h_ref, w1_ref, w3_ref, w2_ref, g_ref, b_ref, *rest, alpha):
    o_ref, hb_ref = rest[-2 if len(rest) == 2 else 3], rest[-1]
    j = pl.program_id(1)

    @pl.when(j == 0)
    def _():
        hb_ref[...] = h_ref[...].astype(BF16)
        o_ref[...] = jnp.zeros_like(o_ref)

    if len(rest) > 2:
        for src, dst in zip(rest[:3], rest[4:7]):
            dst[...] = src[...].astype(BF16)
    hb = hb_ref[...]
    u = _dot(hb, w1_ref[...])
    v = _dot(hb, w3_ref[...])
    act = (u * _sigmoid(u) * v).astype(BF16)
    o_ref[...] += _dot(act, w2_ref[...])

    @pl.when(j == pl.num_programs(1) - 1)
    def _():
        y = alpha * h_ref[...] + 0.5 * o_ref[...]
        o_ref[...] = _layer_norm(y, g_ref[...], b_ref[...])


def _ffn_ln(h, w1, w3, w2, g, b, alpha, nxt=None, tm=None, tf=512):
    t, d = h.shape
    f = w1.shape[-1]
    tm = tm or _pick_tm(t, 544)
    ni, nj = t // tm, f // tf
    in_specs = [pl.BlockSpec((tm, d), lambda i, j: (i, 0)),
                pl.BlockSpec((d, tf), lambda i, j: (0, j)),
                pl.BlockSpec((d, tf), lambda i, j: (0, j)),
                pl.BlockSpec((tf, d), lambda i, j: (j, 0)),
                pl.BlockSpec((1, d), lambda i, j: (0, 0)),
                pl.BlockSpec((1, d), lambda i, j: (0, 0))]
    out_specs = [pl.BlockSpec((tm, d), lambda i, j: (i, 0))]
    out_shape = [jax.ShapeDtypeStruct((t, d), F32)]
    args = [h, w1, w3, w2, g.reshape(1, d), b.reshape(1, d)]
    if nxt is not None:
        n1, n3, n2, l, s = nxt
        dr = d // ni
        assert dr * ni == d and dr % LANES == 0
        in_specs += [pl.BlockSpec((None, None, dr, tf), lambda i, j: (l, s, i, j)),
                     pl.BlockSpec((None, None, dr, tf), lambda i, j: (l, s, i, j)),
                     pl.BlockSpec((None, None, tf, dr), lambda i, j: (l, s, j, i))]
        out_specs += [pl.BlockSpec((dr, tf), lambda i, j: (i, j)),
                      pl.BlockSpec((dr, tf), lambda i, j: (i, j)),
                      pl.BlockSpec((tf, dr), lambda i, j: (j, i))]
        out_shape += [jax.ShapeDtypeStruct((d, f), BF16), jax.ShapeDtypeStruct((d, f), BF16),
                      jax.ShapeDtypeStruct((f, d), BF16)]
        args += [n1, n3, n2]
    out = pl.pallas_call(
        functools.partial(_ffn_ln_kernel, alpha=alpha),
        grid=(ni, nj),
        in_specs=in_specs,
        out_specs=out_specs,
        out_shape=out_shape,
        scratch_shapes=[pltpu.VMEM((tm, d), BF16)],
        name="ffn_ln",
        compiler_params=_cparams(("parallel", "arbitrary")),
    )(*args)
    return out[0], tuple(out[1:])


def _cumsum_rows(x):
    n = x.shape[0]
    row = lax.broadcasted_iota(jnp.int32, x.shape, 0)
    s = 1
    while s < n:
        x = x + jnp.where(row >= s, pltpu.roll(x, s, 0), 0.0)
        s *= 2
    return x


def _mm1(a, b):
    return _dot(a.astype(BF16), b.astype(BF16))


def _wkv_chunk(tiles, params, states, c):
    m0, strict, incl, ones_bd = c
    cc = tiles[0][0].shape[0]
    c2 = 2 * cc
    n = len(tiles)
    inv_n = 1.0 / RW_HEAD

    def stack2(x):
        return jnp.concatenate([jnp.where(m0, x, 0.0), jnp.where(m0, 0.0, x)], axis=0)

    def segsum(xs):
        return [_dot(x.astype(BF16), ones_bd) for x in xs]

    kk0 = [t[2] * p[0] for t, p in zip(tiles, params)]
    nrm2 = segsum([x * x for x in kk0])
    kmod = [t[2] * (1.0 + (t[4] - 1.0) * p[1]) for t, p in zip(tiles, params)]
    bon = segsum([t[0] * km * p[2] for t, km, p in zip(tiles, kmod, params)])

    lhs, rhs, vs, upd_r, dec = [], [], [], [], []
    for i in range(n):
        r, wl, k, v, a, g = tiles[i]
        kk = kk0[i] / jnp.maximum(jnp.sqrt(nrm2[i]), 1e-12)
        b = kk * a
        lw = -jnp.exp(wl)
        lcum = _cumsum_rows(lw)
        ltot = lcum[cc - 1:cc, :]
        e_neg = jnp.exp(-lcum)
        e_rem = jnp.exp(ltot - lcum)
        kt = kk * jnp.exp(lcum - lw)
        rt = r * jnp.exp(lcum)
        khb = (kmod[i] * e_neg).astype(BF16)
        bhb = (b * e_neg).astype(BF16)
        lhs.append(jnp.concatenate([stack2(kt), stack2(rt)], axis=0).astype(BF16))
        rhs.append(jnp.concatenate([khb, khb, bhb, bhb, states[i].astype(BF16)], axis=0))
        vs.append(stack2(v))
        upd_r.append(jnp.concatenate([stack2(kmod[i] * e_rem), stack2(b * e_rem)], axis=0).astype(BF16))
        dec.append(jnp.exp(ltot))

    m = [_dot_nt(a_, b_) for a_, b_ in zip(lhs, rhs)]
    a_k = [jnp.where(strict, x[:c2, :c2], 0.0) for x in m]
    a_b = [jnp.where(strict, x[:c2, c2:2 * c2], 0.0) for x in m]
    bkb = [jnp.concatenate([jnp.where(incl, x[c2:, :c2], 0.0), jnp.where(incl, -x[c2:, c2:2 * c2], 0.0)],
                           axis=1).astype(BF16) for x in m]
    rhs_u = [x[:c2, 2 * c2:] + _mm1(ak, v_) for x, ak, v_ in zip(m, a_k, vs)]
    pw = [ab.astype(BF16) for ab in a_b]
    st = [_dot(p_, jnp.concatenate([p_, ru.astype(BF16)], axis=1)) for p_, ru in zip(pw, rhs_u)]
    pw = [s_[:, :c2].astype(BF16) for s_ in st]
    us = [ru - s_[:, c2:] for ru, s_ in zip(rhs_u, st)]
    lvl = 4
    while lvl < cc:
        st = [_dot(p_, jnp.concatenate([p_, u_.astype(BF16)], axis=1)) for p_, u_ in zip(pw, us)]
        pw = [s_[:, :c2].astype(BF16) for s_ in st]
        us = [u_ + s_[:, c2:] for u_, s_ in zip(us, st)]
        lvl *= 2
    us = [u_ + _dot(p_, u_.astype(BF16)) for u_, p_ in zip(us, pw)]
    ys = [x[c2:, 2 * c2:] + _dot(bk, jnp.concatenate([v_, u_], axis=0).astype(BF16))
          for x, bk, v_, u_ in zip(m, bkb, vs, us)]
    y = [x[:cc] + x[cc:] for x in ys]
    upd_l = [jnp.concatenate([v_, -u_], axis=0).T.astype(BF16) for v_, u_ in zip(vs, us)]
    s_new = [s0 * d_ + _dot(ul, ur) for s0, d_, ul, ur in zip(states, dec, upd_l, upd_r)]

    mu = [x * inv_n for x in segsum(y)]
    yc = [a_ - b_ for a_, b_ in zip(y, mu)]
    var = [x * inv_n for x in segsum([x * x for x in yc])]
    outs = []
    for i in range(n):
        yn = yc[i] * lax.rsqrt(var[i] + RW_GN_EPS) * params[i][3] + params[i][4]
        outs.append((yn + bon[i] * tiles[i][3]) * tiles[i][5])
    return outs, s_new


def _wkv_kernel(r_ref, k_ref, v_ref, lora_ref, w2_ref, a2_ref, g2_ref, w0_ref, a0_ref,
                kk_ref, ka_ref, rk_ref, lg_ref, lb_ref, o_ref, s_ref, *, pairs, r1):
    @pl.when(pl.program_id(2) == 0)
    def _():
        s_ref[...] = jnp.zeros_like(s_ref)

    lo = lora_ref[...]
    w_all = -_softplus(-(_dot(lo[:, :r1], w2_ref[...]) + w0_ref[...])) - 0.5
    a_all = _sigmoid(_dot(lo[:, r1:2 * r1], a2_ref[...]) + a0_ref[...])
    g_all = _dot(lo[:, 2 * r1:], g2_ref[...])

    cc = r_ref.shape[0]
    c2 = 2 * cc
    lane = lax.broadcasted_iota(jnp.int32, (cc, LANES), 1)
    m0 = lane < RW_HEAD
    row = lax.broadcasted_iota(jnp.int32, (c2, c2), 0)
    col = lax.broadcasted_iota(jnp.int32, (c2, c2), 1)
    same = (row >= cc) == (col >= cc)
    strict = same & (col < row)
    incl = same & (col <= row)
    lr = lax.broadcasted_iota(jnp.int32, (LANES, LANES), 0)
    lc = lax.broadcasted_iota(jnp.int32, (LANES, LANES), 1)
    ones_bd = jnp.where((lr >= RW_HEAD) == (lc >= RW_HEAD), 1.0, 0.0).astype(BF16)
    consts = (m0, strict, incl, ones_bd)

    sls = [slice(p * LANES, (p + 1) * LANES) for p in range(pairs)]
    tiles = [(r_ref[:, sl], w_all[:, sl], k_ref[:, sl], v_ref[:, sl], a_all[:, sl], g_all[:, sl]) for sl in sls]
    params = [tuple(ref[:, sl] for ref in (kk_ref, ka_ref, rk_ref, lg_ref, lb_ref)) for sl in sls]
    outs, s_new = _wkv_chunk(tiles, params, [s_ref[p] for p in range(pairs)], consts)
    for p in range(pairs):
        s_ref[p] = s_new[p]
        o_ref[:, sls[p]] = outs[p].astype(o_ref.dtype)


def _wkv(rkv, lora, w2, a2, g2, w0, a0, kkp, kap, rkp, lgp, lbp, batch, r1, out_dtype=BF16):
    _, t, d = rkv.shape
    nl = lora.shape[1]
    lp = t // batch
    cc = WKV_CHUNK
    nchunk = lp // cc
    pairs = min(WKV_PAIRS_PER_STEP, d // LANES)
    wblk = pairs * LANES
    tok = pl.BlockSpec((cc, wblk), lambda b, p, c: (b * nchunk + c, p))
    r_spec, k_spec, v_spec = [pl.BlockSpec((None, cc, wblk), lambda b, p, c, n=n: (n, b * nchunk + c, p))
                              for n in range(3)]
    lo_spec = pl.BlockSpec((cc, nl), lambda b, p, c: (b * nchunk + c, 0))
    up = [pl.BlockSpec((w.shape[0], wblk), lambda b, p, c: (0, p)) for w in (w2, a2, g2)]
    par = pl.BlockSpec((1, wblk), lambda b, p, c: (0, p))
    prm = [x.reshape(1, d) for x in (w0, a0, kkp, kap, rkp, lgp, lbp)]
    return pl.pallas_call(
        functools.partial(_wkv_kernel, pairs=pairs, r1=r1),
        grid=(batch, d // wblk, nchunk),
        in_specs=[r_spec, k_spec, v_spec, lo_spec] + up + [par] * 7,
        out_specs=tok,
        out_shape=jax.ShapeDtypeStruct((t, d), out_dtype),
        scratch_shapes=[pltpu.VMEM((pairs, LANES, LANES), F32)],
        name="wkv7",
        compiler_params=_cparams(("parallel", "parallel", "arbitrary")),
    )(rkv, rkv, rkv, lora, w2, a2, g2, *prm)


def _fgate_kernel(h_ref, wh_ref, wl_ref, b_ref, o_ref, carry_ref):
    @pl.when(pl.program_id(1) == 0)
    def _():
        carry_ref[...] = jnp.zeros_like(carry_ref)

    hh, hl = _split(h_ref[...])
    z = _dot(hh, wh_ref[...]) + (_dot(hh, wl_ref[...]) + _dot(hl, wh_ref[...])) + b_ref[...]
    logf = -_softplus(-z) * LOG2E
    cs = _cumsum_rows(logf) + carry_ref[...]
    o_ref[...] = cs
    carry_ref[...] = cs[cs.shape[0] - 1:, :]


def _fgate_cumsum(h, wf, bf, batch, tm=128):
    t, d = h.shape
    lp = t // batch
    nt = lp // tm
    nh = wf.shape[1]
    wpad = jnp.zeros((d, LANES), F32).at[:, :nh].set(wf)
    bpad = jnp.zeros((1, LANES), F32).at[0, :nh].set(bf)
    wh, wl = _split(wpad)
    return pl.pallas_call(
        _fgate_kernel,
        grid=(batch, nt),
        in_specs=[pl.BlockSpec((tm, d), lambda b, i: (b * nt + i, 0)),
                  pl.BlockSpec((d, LANES), lambda b, i: (0, 0)),
                  pl.BlockSpec((d, LANES), lambda b, i: (0, 0)),
                  pl.BlockSpec((1, LANES), lambda b, i: (0, 0))],
        out_specs=pl.BlockSpec((tm, LANES), lambda b, i: (b * nt + i, 0)),
        out_shape=jax.ShapeDtypeStruct((t, LANES), F32),
        scratch_shapes=[pltpu.VMEM((1, LANES), F32)],
        name="fgate_cumsum",
        compiler_params=_cparams(("parallel", "arbitrary")),
    )(h, wh, wl, bpad)


def _fox_kernel(q_ref, k_ref, vt_ref, cc_ref, cr_ref, o_ref, *, heads, first, tile):
    lp = q_ref.shape[0]
    nbig = (lp - first) // tile
    hs = [slice(h * FX_HEAD, (h + 1) * FX_HEAD) for h in range(heads)]

    def scores(q, kstart, ksize):
        return [_dot_nt(k_ref[pl.ds(kstart, ksize), hs[h]], q[h]) for h in range(heads)]

    def update(s, cq, carry, kstart, ksize, diag):
        s = [s[h] + cq[h] - cc_ref[0, 0, pl.ds(kstart, ksize), h:h + 1] for h in range(heads)]
        if diag:
            keep = (lax.broadcasted_iota(jnp.int32, (ksize, ksize), 0)
                    <= lax.broadcasted_iota(jnp.int32, (ksize, ksize), 1))
            s = [jnp.where(keep, x, NEG_INF) for x in s]
        m_new = [jnp.maximum(carry[h][0], jnp.max(s[h], axis=0, keepdims=True)) for h in range(heads)]
        alpha = [jnp.exp2(carry[h][0] - m_new[h]) for h in range(heads)]
        p = [jnp.exp2(s[h] - m_new[h]) for h in range(heads)]
        l_new = [alpha[h] * carry[h][1] + jnp.sum(p[h], axis=0, keepdims=True) for h in range(heads)]
        pv = [_dot(vt_ref[hs[h], pl.ds(kstart, ksize)], p[h].astype(BF16)) for h in range(heads)]
        return [(m_new[h], l_new[h], alpha[h] * carry[h][2] + pv[h]) for h in range(heads)]

    def q_tile(qstart, tq, nfull, with_first):
        q = [q_ref[pl.ds(qstart, tq), hs[h]] for h in range(heads)]
        cq = [cr_ref[0, h, :, pl.ds(qstart, tq)] for h in range(heads)]
        carry = [(jnp.full((1, tq), NEG_INF, F32), jnp.zeros((1, tq), F32), jnp.zeros((FX_HEAD, tq), F32))
                 for _ in range(heads)]
        def step(c, kstart, ksize, diag):
            return update(scores(q, kstart, ksize), cq, c, kstart, ksize, diag)

        def step2(j, c):
            k0 = pl.multiple_of(first + 2 * j * tile, LANES)
            k1 = pl.multiple_of(first + (2 * j + 1) * tile, LANES)
            s0, s1 = scores(q, k0, tile), scores(q, k1, tile)
            return update(s1, cq, update(s0, cq, c, k0, tile, False), k1, tile, False)

        if nfull is not None:
            carry = lax.fori_loop(0, lax.shift_right_logical(nfull, 1), step2, carry)
            carry = lax.fori_loop(
                0, nfull & 1,
                lambda _, c: step(c, pl.multiple_of(first + (nfull - 1) * tile, LANES), tile, False), carry)
        s_first = scores(q, 0, first) if with_first else None
        s_diag = scores(q, qstart, tq)
        if with_first:
            carry = update(s_first, cq, carry, 0, first, False)
        carry = update(s_diag, cq, carry, qstart, tq, True)
        for h in range(heads):
            o_ref[pl.ds(qstart, tq), hs[h]] = (carry[h][2] / carry[h][1]).T.astype(o_ref.dtype)

    if first:
        q_tile(0, first, None, False)

    def big(i, _):
        q_tile(pl.multiple_of(first + i * tile, LANES), tile, i, first > 0)
        return 0

    lax.fori_loop(0, nbig, big, 0)


FOX_Q_SCALE = FX_HEAD ** -0.5 * LOG2E


def _fox_attention(q, k, vt, c, batch, heads=FOX_HEADS_PER_STEP, tile=256):
    t, d = q.shape
    lp = t // batch
    nh = d // FX_HEAD
    ng = nh // heads
    wblk = heads * FX_HEAD
    first = lp % tile
    c_col = jnp.transpose(c.reshape(batch, lp, ng, heads), (0, 2, 1, 3))
    c_row = jnp.transpose(c, (0, 2, 1))[:, :, None, :]
    return pl.pallas_call(
        functools.partial(_fox_kernel, heads=heads, first=first, tile=tile),
        grid=(batch, ng),
        in_specs=[pl.BlockSpec((lp, wblk), lambda b, g: (b, g)),
                  pl.BlockSpec((lp, wblk), lambda b, g: (b, g)),
                  pl.BlockSpec((wblk, lp), lambda b, g: (g, b)),
                  pl.BlockSpec((1, 1, lp, heads), lambda b, g: (b, g, 0, 0)),
                  pl.BlockSpec((1, heads, 1, lp), lambda b, g: (b, g, 0, 0))],
        out_specs=pl.BlockSpec((lp, wblk), lambda b, g: (b, g)),
        out_shape=jax.ShapeDtypeStruct((t, d), BF16),
        name="fox_attention",
        compiler_params=_cparams(("parallel", "parallel")),
    )(q, k, vt, c_col, c_row)


def _pad_cols(w, n):
    return jnp.zeros((w.shape[0], n), w.dtype).at[:, :w.shape[1]].set(w)


def _pad_rows(w, n):
    return jnp.zeros((n, w.shape[1]), w.dtype).at[:w.shape[0], :].set(w)


def _rwkv7_mix(h, batch, mu, w_rkv, w_o, w0, w1, w2, a0, a1, a2, g1, g2, k_k, k_a, r_k, lnx_g, lnx_b,
               ln_g, ln_b, alpha):
    r1 = LANES
    lw = jnp.concatenate([_pad_cols(w1, r1), _pad_cols(a1, r1), g1], axis=1).astype(BF16)
    rkv, lora = _rwkv_in(h, mu, w_rkv.astype(BF16), lw, batch, r1)
    o = _wkv(rkv, lora, _pad_rows(w2, r1).astype(BF16), _pad_rows(a2, r1).astype(BF16), g2.astype(BF16),
             w0, a0, k_k, k_a, r_k.reshape(-1), lnx_g, lnx_b, batch, r1)
    return _matmul_ln(o, w_o.astype(BF16), h, ln_g, ln_b, alpha)


def kernel(x, meta_tokens, ln_g, ln_b, ffn_w1, ffn_w3, ffn_w2, rw_mu, rw_w_rkv, rw_w_o, rw_w0, rw_w1,
           rw_w2, rw_a0, rw_a1, rw_a2, rw_g1, rw_g2, rw_k_k, rw_k_a, rw_r_k, rw_lnx_g, rw_lnx_b,
           fx_w_q, fx_w_o, fx_w_kvf, fx_b_f):
    batch, seq, d = x.shape
    depth = ln_g.shape[0]
    n_a = rw_mu.shape[0]
    alpha = (2 * depth) ** 0.25
    l_real = seq + N_META
    lp = -(-l_real // SEQ_ALIGN) * SEQ_ALIGN
    t = batch * lp
    nh = d // FX_HEAD

    meta = jnp.broadcast_to(meta_tokens.astype(x.dtype)[None], (batch, N_META, d))
    h = jnp.concatenate([meta, x, jnp.zeros((batch, lp - l_real, d), x.dtype)], axis=1).reshape(t, d)

    wb = (ffn_w1[0, 0].astype(BF16), ffn_w3[0, 0].astype(BF16), ffn_w2[0, 0].astype(BF16))

    def ffn(h, wb, l, s):
        last = l == depth - 1 and s == 1
        nxt = None if last else (ffn_w1, ffn_w3, ffn_w2, l + s, 1 - s)
        return _ffn_ln(h, *wb, ln_g[l, 2 * s], ln_b[l, 2 * s], alpha, nxt=nxt)

    k_s = vt_s = c = None
    for l in range(depth):
        h, wb = ffn(h, wb, l, 0)
        if l < n_a:
            h = _rwkv7_mix(h, batch, rw_mu[l], rw_w_rkv[l], rw_w_o[l], rw_w0[l], rw_w1[l], rw_w2[l],
                           rw_a0[l], rw_a1[l], rw_a2[l], rw_g1[l], rw_g2[l], rw_k_k[l], rw_k_a[l],
                           rw_r_k[l], rw_lnx_g[l], rw_lnx_b[l], ln_g[l, 1], ln_b[l, 1], alpha)
        else:
            j = l - n_a
            q = _matmul(h, fx_w_q[j].astype(BF16), act="scale", act_arg=FOX_Q_SCALE, out_dtype=BF16,
                        tn=1024, name="q_proj")
            o = _fox_attention(q, k_s, vt_s, c, batch)
            h = _matmul_ln(o, fx_w_o[j].astype(BF16), h, ln_g[l, 1], ln_b[l, 1], alpha)
        h, wb = ffn(h, wb, l, 1)
        if l == n_a - 1:
            w_kvf = fx_w_kvf.astype(BF16)
            k_s = _matmul(h, w_kvf, n=d, out_dtype=BF16, tn=1024, name="k_proj")
            vt_s = _matmul(h, w_kvf, n=d, woff=d, out_dtype=BF16, tm=512, tn=1024, transpose_out=True,
                           name="vt_proj")
            c = _fgate_cumsum(h, fx_w_kvf[:, 2 * d:], fx_b_f, batch)[:, :nh].reshape(batch, lp, nh)
    return h.reshape(batch, lp, d)[:, N_META:l_real]
```

```python
import functools

import jax
import jax.numpy as jnp
from jax import lax
from jax.experimental import pallas as pl
from jax.experimental.pallas import tpu as pltpu

N_META = 16
RW_HEAD = 64
FX_HEAD = 128
LN_EPS = 1e-5
RW_GN_EPS = RW_HEAD * 1e-5
NEG_INF = -1e30
LOG2E = 1.4426950408889634

LANES = 128
SEQ_ALIGN = 128
WKV_CHUNK = 64
WKV_PAIRS_PER_STEP = 16
FOX_HEADS_PER_STEP = 8
VMEM_LIMIT = 56 * 1024 * 1024

F32 = jnp.float32
BF16 = jnp.bfloat16


def _cparams(sem):
    return pltpu.CompilerParams(dimension_semantics=sem, vmem_limit_bytes=VMEM_LIMIT)


def _dot(a, b):
    return jnp.dot(a, b, preferred_element_type=F32)


def _dot_nt(a, b):
    return lax.dot_general(a, b, (((1,), (1,)), ((), ())), preferred_element_type=F32)


def _split(x):
    hi = x.astype(BF16)
    lo = (x - hi.astype(F32)).astype(BF16)
    return hi, lo


def _softplus(x):
    return jnp.maximum(x, 0.0) + jnp.log1p(jnp.exp(-jnp.abs(x)))


def _sigmoid(x):
    return 1.0 / (1.0 + jnp.exp(-x))


def _layer_norm(y, g, b):
    mu = jnp.mean(y, axis=-1, keepdims=True)
    yc = y - mu
    var = jnp.mean(yc * yc, axis=-1, keepdims=True)
    return yc * lax.rsqrt(var + LN_EPS) * g + b


def _mm_kernel(x_ref, w_ref, b_ref, o_ref, *scratch, act, act_arg, transpose_out):
    if scratch:
        xb_ref, = scratch

        @pl.when(pl.program_id(1) == 0)
        def _():
            xb_ref[...] = x_ref[...].astype(BF16)

        x = xb_ref[...]
    else:
        x = x_ref[...]
    acc = _dot(x, w_ref[...]) + b_ref[...]
    if act == "scale":
        acc = acc * act_arg
    if act == "tanh":
        acc = jnp.tanh(acc)
    elif act == "sigmoid":
        acc = _sigmoid(acc)
    elif act == "decay":
        acc = -_softplus(-acc) - 0.5
    if transpose_out:
        acc = acc.T
    o_ref[...] = acc.astype(o_ref.dtype)


def _matmul(x, w, bias=None, act=None, act_arg=None, out_dtype=F32, tm=None, tn=None, xblk=0, n=None, woff=0,
            transpose_out=False, name="matmul"):
    t = x.shape[0]
    k = w.shape[0]
    n = n or w.shape[1]
    tm = tm or _pick_tm(t)
    tn = tn or min(n, 512)
    if bias is None:
        bias = jnp.zeros((1, n), F32)
    cast = x.dtype != BF16
    jo = woff // tn
    if transpose_out:
        out_spec, out_shape = pl.BlockSpec((tn, tm), lambda i, j: (j, i)), (n, t)
    else:
        out_spec, out_shape = pl.BlockSpec((tm, tn), lambda i, j: (i, j)), (t, n)
    return pl.pallas_call(
        functools.partial(_mm_kernel, act=act, act_arg=act_arg, transpose_out=transpose_out),
        grid=(t // tm, n // tn),
        in_specs=[pl.BlockSpec((tm, k), lambda i, j: (i, xblk)),
                  pl.BlockSpec((k, tn), lambda i, j: (0, j + jo)),
                  pl.BlockSpec((1, tn), lambda i, j: (0, j))],
        out_specs=out_spec,
        out_shape=jax.ShapeDtypeStruct(out_shape, out_dtype),
        scratch_shapes=[pltpu.VMEM((tm, k), BF16)] if cast else [],
        name=name,
        compiler_params=_cparams(("parallel", "arbitrary" if cast else "parallel")),
    )(x, w, bias.reshape(1, n).astype(F32))


def _pick_tm(t, cap=1088):
    for tm in range(min(cap, t), 0, -16):
        if t % tm == 0:
            return tm
    return t


PREV_ROWS = 8


def _rwkv_in_kernel(h_ref, prev_ref, mu_ref, w_ref, lw_ref, rkv_ref, lora_ref, xs_ref, *, lp, nj, r1):
    i = pl.program_id(0)
    j = pl.program_id(1)
    tm = h_ref.shape[0]

    @pl.when(j == 0)
    def _():
        h = h_ref[...]
        last = prev_ref[PREV_ROWS - 1:, :]
        last = jnp.where(lax.rem(i * tm, lp) == 0, 0.0, last)
        row = lax.broadcasted_iota(jnp.int32, h.shape, 0)
        xx = jnp.where(row == 0, last, pltpu.roll(h, 1, 0)) - h

        def mix(n):
            return (h + xx * mu_ref[n:n + 1, :]).astype(BF16)

        xs_ref[0] = mix(0)
        xs_ref[1] = mix(2)
        xs_ref[2] = mix(3)
        tw = jnp.tanh(_dot(mix(1), lw_ref[:, :r1]))
        ta = _dot(mix(4), lw_ref[:, r1:2 * r1])
        tg = _sigmoid(_dot(mix(5), lw_ref[:, 2 * r1:]))
        lora_ref[...] = jnp.concatenate([tw, ta, tg], axis=1).astype(BF16)

    rkv_ref[...] = _dot(xs_ref[j // nj], w_ref[...])


def _rwkv_in(h, mu, w_rkv, lw, batch, r1, tm=None, tn=1024):
    t, d = h.shape
    lp = t // batch
    tm = tm or _pick_tm(lp, 544)
    nj = d // tn
    nl = lw.shape[1]
    pblk = tm // PREV_ROWS
    return pl.pallas_call(
        functools.partial(_rwkv_in_kernel, lp=lp, nj=nj, r1=r1),
        grid=(t // tm, 3 * nj),
        in_specs=[pl.BlockSpec((tm, d), lambda i, j: (i, 0)),
                  pl.BlockSpec((PREV_ROWS, d), lambda i, j: (jnp.maximum(i * pblk - 1, 0), 0)),
                  pl.BlockSpec((6, d), lambda i, j: (0, 0)),
                  pl.BlockSpec((None, d, tn), lambda i, j: (j // nj, 0, j % nj)),
                  pl.BlockSpec((d, nl), lambda i, j: (0, 0))],
        out_specs=[pl.BlockSpec((None, tm, tn), lambda i, j: (j // nj, i, j % nj)),
                   pl.BlockSpec((tm, nl), lambda i, j: (i, 0))],
        out_shape=[jax.ShapeDtypeStruct((3, t, d), F32), jax.ShapeDtypeStruct((t, nl), BF16)],
        scratch_shapes=[pltpu.VMEM((3, tm, d), BF16)],
        name="rwkv_in",
        compiler_params=_cparams(("parallel", "arbitrary")),
    )(h, h, mu, w_rkv, lw)


def _mm_ln_kernel(x_ref, w_ref, h_ref, g_ref, b_ref, o_ref, *, alpha, splits):
    rs = x_ref.shape[0] // splits
    rows = [slice(n * rs, (n + 1) * rs) for n in range(splits)]
    ys = [alpha * h_ref[r, :] + _dot(x_ref[r, :], w_ref[...]) for r in rows]
    for r, y in zip(rows, ys):
        o_ref[r, :] = _layer_norm(y, g_ref[...], b_ref[...])


def _matmul_ln(x, w, h, g, b, alpha, tm=None):
    t, k = x.shape
    n = w.shape[1]
    tm = tm or _pick_tm(t, 512)
    return pl.pallas_call(
        functools.partial(_mm_ln_kernel, alpha=alpha, splits=4 if tm % 64 == 0 else 1),
        grid=(t // tm,),
        in_specs=[pl.BlockSpec((tm, k), lambda i: (i, 0)),
                  pl.BlockSpec((k, n), lambda i: (0, 0)),
                  pl.BlockSpec((tm, n), lambda i: (i, 0)),
                  pl.BlockSpec((1, n), lambda i: (0, 0)),
                  pl.BlockSpec((1, n), lambda i: (0, 0))],
        out_specs=pl.BlockSpec((tm, n), lambda i: (i, 0)),
        out_shape=jax.ShapeDtypeStruct((t, n), F32),
        name="proj_ln",
        compiler_params=_cparams(("parallel",)),
    )(x, w, h, g.reshape(1, n), b.reshape(1, n))


def _ffn_ln_kernel(h_ref, w1_ref, w3_ref, w2_ref, g_ref, b_ref, *rest, alpha):
    o_ref, hb_ref = rest[-2 if len(rest) == 2 else 3], rest[-1]
    j = pl.program_id(1)

    @pl.when(j == 0)
    def _():
        hb_ref[...] = h_ref[...].astype(BF16)
        o_ref[...] = jnp.zeros_like(o_ref)

    def cast_next_weights():
        for src, dst in zip(rest[:3], rest[4:7]) if len(rest) > 2 else ():
            dst[...] = src[...].astype(BF16)

    def partial_out(rows):
        hb = hb_ref[rows, :]
        u = _dot(hb, w1_ref[...])
        v = _dot(hb, w3_ref[...])
        return _dot((u * _sigmoid(u) * v).astype(BF16), w2_ref[...])

    last = pl.num_programs(1) - 1
    tm = hb_ref.shape[0]

    @pl.when(j < last)
    def _():
        cast_next_weights()
        o_ref[...] += partial_out(slice(0, tm))

    @pl.when(j == last)
    def _():
        cast_next_weights()
        for rows in (slice(0, tm // 2), slice(tm // 2, tm)):
            y = alpha * h_ref[rows, :] + 0.5 * (o_ref[rows, :] + partial_out(rows))
            o_ref[rows, :] = _layer_norm(y, g_ref[...], b_ref[...])


def _ffn_ln(h, w1, w3, w2, g, b, alpha, nxt=None, tm=None, tf=512):
    t, d = h.shape
    f = w1.shape[-1]
    tm = tm or _pick_tm(t, 544)
    ni, nj = t // tm, f // tf
    in_specs = [pl.BlockSpec((tm, d), lambda i, j: (i, 0)),
                pl.BlockSpec((d, tf), lambda i, j: (0, j)),
                pl.BlockSpec((d, tf), lambda i, j: (0, j)),
                pl.BlockSpec((tf, d), lambda i, j: (j, 0)),
                pl.BlockSpec((1, d), lambda i, j: (0, 0)),
                pl.BlockSpec((1, d), lambda i, j: (0, 0))]
    out_specs = [pl.BlockSpec((tm, d), lambda i, j: (i, 0))]
    out_shape = [jax.ShapeDtypeStruct((t, d), F32)]
    args = [h, w1, w3, w2, g.reshape(1, d), b.reshape(1, d)]
    if nxt is not None:
        n1, n3, n2, l, s = nxt
        dr = d // ni
        assert dr * ni == d and dr % LANES == 0
        in_specs += [pl.BlockSpec((None, None, dr, tf), lambda i, j: (l, s, i, j)),
                     pl.BlockSpec((None, None, dr, tf), lambda i, j: (l, s, i, j)),
                     pl.BlockSpec((None, None, tf, dr), lambda i, j: (l, s, j, i))]
        out_specs += [pl.BlockSpec((dr, tf), lambda i, j: (i, j)),
                      pl.BlockSpec((dr, tf), lambda i, j: (i, j)),
                      pl.BlockSpec((tf, dr), lambda i, j: (j, i))]
        out_shape += [jax.ShapeDtypeStruct((d, f), BF16), jax.ShapeDtypeStruct((d, f), BF16),
                      jax.ShapeDtypeStruct((f, d), BF16)]
        args += [n1, n3, n2]
    out = pl.pallas_call(
        functools.partial(_ffn_ln_kernel, alpha=alpha),
        grid=(ni, nj),
        in_specs=in_specs,
        out_specs=out_specs,
        out_shape=out_shape,
        scratch_shapes=[pltpu.VMEM((tm, d), BF16)],
        name="ffn_ln",
        compiler_params=_cparams(("parallel", "arbitrary")),
    )(*args)
    return out[0], tuple(out[1:])


def _cumsum_rows(x):
    n = x.shape[0]
    row = lax.broadcasted_iota(jnp.int32, x.shape, 0)
    s = 1
    while s < n:
        x = x + jnp.where(row >= s, pltpu.roll(x, s, 0), 0.0)
        s *= 2
    return x


def _mm1(a, b):
    return _dot(a.astype(BF16), b.astype(BF16))


def _wkv_chunk(tiles, params, states, c):
    m0, strict, incl, ones_bd = c
    cc = tiles[0][0].shape[0]
    c2 = 2 * cc
    n = len(tiles)
    inv_n = 1.0 / RW_HEAD

    def stack2(x):
        return jnp.concatenate([jnp.where(m0, x, 0.0), jnp.where(m0, 0.0, x)], axis=0)

    def segsum(xs):
        return [_dot(x.astype(BF16), ones_bd) for x in xs]

    kk0 = [t[2] * p[0] for t, p in zip(tiles, params)]
    nrm2 = segsum([x * x for x in kk0])
    kmod = [t[2] * (1.0 + (t[4] - 1.0) * p[1]) for t, p in zip(tiles, params)]
    bon = segsum([t[0] * km * p[2] for t, km, p in zip(tiles, kmod, params)])

    lhs, rhs, vs, upd_r, dec = [], [], [], [], []
    for i in range(n):
        r, wl, k, v, a, g = tiles[i]
        kk = kk0[i] / jnp.maximum(jnp.sqrt(nrm2[i]), 1e-12)
        b = kk * a
        lw = -jnp.exp(wl)
        lcum = _cumsum_rows(lw)
        ltot = lcum[cc - 1:cc, :]
        e_neg = jnp.exp(-lcum)
        e_rem = jnp.exp(ltot - lcum)
        kt = kk * jnp.exp(lcum - lw)
        rt = r * jnp.exp(lcum)
        khb = (kmod[i] * e_neg).astype(BF16)
        bhb = (b * e_neg).astype(BF16)
        lhs.append(jnp.concatenate([stack2(kt), stack2(rt)], axis=0).astype(BF16))
        rhs.append(jnp.concatenate([khb, khb, bhb, bhb, states[i].astype(BF16)], axis=0))
        vs.append(stack2(v))
        upd_r.append(jnp.concatenate([stack2(kmod[i] * e_rem), stack2(b * e_rem)], axis=0).astype(BF16))
        dec.append(jnp.exp(ltot))

    m = [_dot_nt(a_, b_) for a_, b_ in zip(lhs, rhs)]
    a_k = [jnp.where(strict, x[:c2, :c2], 0.0) for x in m]
    a_b = [jnp.where(strict, x[:c2, c2:2 * c2], 0.0) for x in m]
    bkb = [jnp.concatenate([jnp.where(incl, x[c2:, :c2], 0.0), jnp.where(incl, -x[c2:, c2:2 * c2], 0.0)],
                           axis=1).astype(BF16) for x in m]
    rhs_u = [x[:c2, 2 * c2:] + _mm1(ak, v_) for x, ak, v_ in zip(m, a_k, vs)]
    pw = [ab.astype(BF16) for ab in a_b]
    st = [_dot(p_, jnp.concatenate([p_, ru.astype(BF16)], axis=1)) for p_, ru in zip(pw, rhs_u)]
    pw = [s_[:, :c2].astype(BF16) for s_ in st]
    us = [ru - s_[:, c2:] for ru, s_ in zip(rhs_u, st)]
    lvl = 4
    while lvl < cc:
        st = [_dot(p_, jnp.concatenate([p_, u_.astype(BF16)], axis=1)) for p_, u_ in zip(pw, us)]
        pw = [s_[:, :c2].astype(BF16) for s_ in st]
        us = [u_ + s_[:, c2:] for u_, s_ in zip(us, st)]
        lvl *= 2
    us = [u_ + _dot(p_, u_.astype(BF16)) for u_, p_ in zip(us, pw)]
    ys = [x[c2:, 2 * c2:] + _dot(bk, jnp.concatenate([v_, u_], axis=0).astype(BF16))
          for x, bk, v_, u_ in zip(m, bkb, vs, us)]
    y = [x[:cc] + x[cc:] for x in ys]
    upd_l = [jnp.concatenate([v_, -u_], axis=0).T.astype(BF16) for v_, u_ in zip(vs, us)]
    s_new = [s0 * d_ + _dot(ul, ur) for s0, d_, ul, ur in zip(states, dec, upd_l, upd_r)]

    mu = [x * inv_n for x in segsum(y)]
    yc = [a_ - b_ for a_, b_ in zip(y, mu)]
    var = [x * inv_n for x in segsum([x * x for x in yc])]
    outs = []
    for i in range(n):
        yn = yc[i] * lax.rsqrt(var[i] + RW_GN_EPS) * params[i][3] + params[i][4]
        outs.append((yn + bon[i] * tiles[i][3]) * tiles[i][5])
    return outs, s_new


def _wkv_kernel(r_ref, k_ref, v_ref, lora_ref, w2_ref, a2_ref, g2_ref, w0_ref, a0_ref,
                kk_ref, ka_ref, rk_ref, lg_ref, lb_ref, o_ref, s_ref, *, pairs, r1):
    @pl.when(pl.program_id(2) == 0)
    def _():
        s_ref[...] = jnp.zeros_like(s_ref)

    lo = lora_ref[...]
    w_all = -_softplus(-(_dot(lo[:, :r1], w2_ref[...]) + w0_ref[...])) - 0.5
    a_all = _sigmoid(_dot(lo[:, r1:2 * r1], a2_ref[...]) + a0_ref[...])
    g_all = _dot(lo[:, 2 * r1:], g2_ref[...])

    cc = r_ref.shape[0]
    c2 = 2 * cc
    lane = lax.broadcasted_iota(jnp.int32, (cc, LANES), 1)
    m0 = lane < RW_HEAD
    row = lax.broadcasted_iota(jnp.int32, (c2, c2), 0)
    col = lax.broadcasted_iota(jnp.int32, (c2, c2), 1)
    same = (row >= cc) == (col >= cc)
    strict = same & (col < row)
    incl = same & (col <= row)
    lr = lax.broadcasted_iota(jnp.int32, (LANES, LANES), 0)
    lc = lax.broadcasted_iota(jnp.int32, (LANES, LANES), 1)
    ones_bd = jnp.where((lr >= RW_HEAD) == (lc >= RW_HEAD), 1.0, 0.0).astype(BF16)
    consts = (m0, strict, incl, ones_bd)

    sls = [slice(p * LANES, (p + 1) * LANES) for p in range(pairs)]
    tiles = [(r_ref[:, sl], w_all[:, sl], k_ref[:, sl], v_ref[:, sl], a_all[:, sl], g_all[:, sl]) for sl in sls]
    params = [tuple(ref[:, sl] for ref in (kk_ref, ka_ref, rk_ref, lg_ref, lb_ref)) for sl in sls]
    outs, s_new = _wkv_chunk(tiles, params, [s_ref[p] for p in range(pairs)], consts)
    for p in range(pairs):
        s_ref[p] = s_new[p]
        o_ref[:, sls[p]] = outs[p].astype(o_ref.dtype)


def _wkv(rkv, lora, w2, a2, g2, w0, a0, kkp, kap, rkp, lgp, lbp, batch, r1, out_dtype=BF16):
    _, t, d = rkv.shape
    nl = lora.shape[1]
    lp = t // batch
    cc = WKV_CHUNK
    nchunk = lp // cc
    pairs = min(WKV_PAIRS_PER_STEP, d // LANES)
    wblk = pairs * LANES
    tok = pl.BlockSpec((cc, wblk), lambda b, p, c: (b * nchunk + c, p))
    r_spec, k_spec, v_spec = [pl.BlockSpec((None, cc, wblk), lambda b, p, c, n=n: (n, b * nchunk + c, p))
                              for n in range(3)]
    lo_spec = pl.BlockSpec((cc, nl), lambda b, p, c: (b * nchunk + c, 0))
    up = [pl.BlockSpec((w.shape[0], wblk), lambda b, p, c: (0, p)) for w in (w2, a2, g2)]
    par = pl.BlockSpec((1, wblk), lambda b, p, c: (0, p))
    prm = [x.reshape(1, d) for x in (w0, a0, kkp, kap, rkp, lgp, lbp)]
    return pl.pallas_call(
        functools.partial(_wkv_kernel, pairs=pairs, r1=r1),
        grid=(batch, d // wblk, nchunk),
        in_specs=[r_spec, k_spec, v_spec, lo_spec] + up + [par] * 7,
        out_specs=tok,
        out_shape=jax.ShapeDtypeStruct((t, d), out_dtype),
        scratch_shapes=[pltpu.VMEM((pairs, LANES, LANES), F32)],
        name="wkv7",
        compiler_params=_cparams(("parallel", "parallel", "arbitrary")),
    )(rkv, rkv, rkv, lora, w2, a2, g2, *prm)


def _fgate_kernel(h_ref, wh_ref, wl_ref, b_ref, o_ref, carry_ref):
    @pl.when(pl.program_id(1) == 0)
    def _():
        carry_ref[...] = jnp.zeros_like(carry_ref)

    hh, hl = _split(h_ref[...])
    z = _dot(hh, wh_ref[...]) + (_dot(hh, wl_ref[...]) + _dot(hl, wh_ref[...])) + b_ref[...]
    logf = -_softplus(-z) * LOG2E
    cs = _cumsum_rows(logf) + carry_ref[...]
    o_ref[...] = cs
    carry_ref[...] = cs[cs.shape[0] - 1:, :]


def _fgate_cumsum(h, wf, bf, batch, tm=128):
    t, d = h.shape
    lp = t // batch
    nt = lp // tm
    nh = wf.shape[1]
    wpad = jnp.zeros((d, LANES), F32).at[:, :nh].set(wf)
    bpad = jnp.zeros((1, LANES), F32).at[0, :nh].set(bf)
    wh, wl = _split(wpad)
    return pl.pallas_call(
        _fgate_kernel,
        grid=(batch, nt),
        in_specs=[pl.BlockSpec((tm, d), lambda b, i: (b * nt + i, 0)),
                  pl.BlockSpec((d, LANES), lambda b, i: (0, 0)),
                  pl.BlockSpec((d, LANES), lambda b, i: (0, 0)),
                  pl.BlockSpec((1, LANES), lambda b, i: (0, 0))],
        out_specs=pl.BlockSpec((tm, LANES), lambda b, i: (b * nt + i, 0)),
        out_shape=jax.ShapeDtypeStruct((t, LANES), F32),
        scratch_shapes=[pltpu.VMEM((1, LANES), F32)],
        name="fgate_cumsum",
        compiler_params=_cparams(("parallel", "arbitrary")),
    )(h, wh, wl, bpad)


def _fox_kernel(q_ref, k_ref, vt_ref, cc_ref, cr_ref, o_ref, *, heads, first, tile):
    lp = q_ref.shape[0]
    nbig = (lp - first) // tile
    hs = [slice(h * FX_HEAD, (h + 1) * FX_HEAD) for h in range(heads)]

    def scores(q, kstart, ksize):
        return [_dot_nt(k_ref[pl.ds(kstart, ksize), hs[h]], q[h]) for h in range(heads)]

    def update(s, cq, carry, kstart, ksize, diag):
        s = [s[h] + cq[h] - cc_ref[0, 0, pl.ds(kstart, ksize), h:h + 1] for h in range(heads)]
        if diag:
            keep = (lax.broadcasted_iota(jnp.int32, (ksize, ksize), 0)
                    <= lax.broadcasted_iota(jnp.int32, (ksize, ksize), 1))
            s = [jnp.where(keep, x, NEG_INF) for x in s]
        m_new = [jnp.maximum(carry[h][0], jnp.max(s[h], axis=0, keepdims=True)) for h in range(heads)]
        alpha = [jnp.exp2(carry[h][0] - m_new[h]) for h in range(heads)]
        p = [jnp.exp2(s[h] - m_new[h]) for h in range(heads)]
        l_new = [alpha[h] * carry[h][1] + jnp.sum(p[h], axis=0, keepdims=True) for h in range(heads)]
        pv = [_dot(vt_ref[hs[h], pl.ds(kstart, ksize)], p[h].astype(BF16)) for h in range(heads)]
        return [(m_new[h], l_new[h], alpha[h] * carry[h][2] + pv[h]) for h in range(heads)]

    def q_tile(qstart, tq, nfull, with_first):
        q = [q_ref[pl.ds(qstart, tq), hs[h]] for h in range(heads)]
        cq = [cr_ref[0, h, :, pl.ds(qstart, tq)] for h in range(heads)]
        carry = [(jnp.full((1, tq), NEG_INF, F32), jnp.zeros((1, tq), F32), jnp.zeros((FX_HEAD, tq), F32))
                 for _ in range(heads)]
        def step(c, kstart, ksize, diag):
            return update(scores(q, kstart, ksize), cq, c, kstart, ksize, diag)

        def step2(j, c):
            k0 = pl.multiple_of(first + 2 * j * tile, LANES)
            k1 = pl.multiple_of(first + (2 * j + 1) * tile, LANES)
            s0, s1 = scores(q, k0, tile), scores(q, k1, tile)
            return update(s1, cq, update(s0, cq, c, k0, tile, False), k1, tile, False)

        if nfull is not None:
            carry = lax.fori_loop(0, lax.shift_right_logical(nfull, 1), step2, carry)
            carry = lax.fori_loop(
                0, nfull & 1,
                lambda _, c: step(c, pl.multiple_of(first + (nfull - 1) * tile, LANES), tile, False), carry)
        s_first = scores(q, 0, first) if with_first else None
        s_diag = scores(q, qstart, tq)
        if with_first:
            carry = update(s_first, cq, carry, 0, first, False)
        carry = update(s_diag, cq, carry, qstart, tq, True)
        for h in range(heads):
            o_ref[pl.ds(qstart, tq), hs[h]] = (carry[h][2] / carry[h][1]).T.astype(o_ref.dtype)

    if first:
        q_tile(0, first, None, False)

    def big(i, _):
        q_tile(pl.multiple_of(first + i * tile, LANES), tile, i, first > 0)
        return 0

    lax.fori_loop(0, nbig, big, 0)


FOX_Q_SCALE = FX_HEAD ** -0.5 * LOG2E


def _fox_attention(q, k, vt, c, batch, heads=FOX_HEADS_PER_STEP, tile=256):
    t, d = q.shape
    lp = t // batch
    nh = d // FX_HEAD
    ng = nh // heads
    wblk = heads * FX_HEAD
    first = lp % tile
    c_col = jnp.transpose(c.reshape(batch, lp, ng, heads), (0, 2, 1, 3))
    c_row = jnp.transpose(c, (0, 2, 1))[:, :, None, :]
    return pl.pallas_call(
        functools.partial(_fox_kernel, heads=heads, first=first, tile=tile),
        grid=(batch, ng),
        in_specs=[pl.BlockSpec((lp, wblk), lambda b, g: (b, g)),
                  pl.BlockSpec((lp, wblk), lambda b, g: (b, g)),
                  pl.BlockSpec((wblk, lp), lambda b, g: (g, b)),
                  pl.BlockSpec((1, 1, lp, heads), lambda b, g: (b, g, 0, 0)),
                  pl.BlockSpec((1, heads, 1, lp), lambda b, g: (b, g, 0, 0))],
        out_specs=pl.BlockSpec((lp, wblk), lambda b, g: (b, g)),
        out_shape=jax.ShapeDtypeStruct((t, d), BF16),
        name="fox_attention",
        compiler_params=_cparams(("parallel", "parallel")),
    )(q, k, vt, c_col, c_row)


def _pad_cols(w, n):
    return jnp.zeros((w.shape[0], n), w.dtype).at[:, :w.shape[1]].set(w)


def _pad_rows(w, n):
    return jnp.zeros((n, w.shape[1]), w.dtype).at[:w.shape[0], :].set(w)


def _rwkv7_mix(h, batch, mu, w_rkv, w_o, w0, w1, w2, a0, a1, a2, g1, g2, k_k, k_a, r_k, lnx_g, lnx_b,
               ln_g, ln_b, alpha):
    r1 = LANES
    lw = jnp.concatenate([_pad_cols(w1, r1), _pad_cols(a1, r1), g1], axis=1).astype(BF16)
    rkv, lora = _rwkv_in(h, mu, w_rkv.astype(BF16), lw, batch, r1)
    o = _wkv(rkv, lora, _pad_rows(w2, r1).astype(BF16), _pad_rows(a2, r1).astype(BF16), g2.astype(BF16),
             w0, a0, k_k, k_a, r_k.reshape(-1), lnx_g, lnx_b, batch, r1)
    return _matmul_ln(o, w_o.astype(BF16), h, ln_g, ln_b, alpha)


def kernel(x, meta_tokens, ln_g, ln_b, ffn_w1, ffn_w3, ffn_w2, rw_mu, rw_w_rkv, rw_w_o, rw_w0, rw_w1,
           rw_w2, rw_a0, rw_a1, rw_a2, rw_g1, rw_g2, rw_k_k, rw_k_a, rw_r_k, rw_lnx_g, rw_lnx_b,
           fx_w_q, fx_w_o, fx_w_kvf, fx_b_f):
    batch, seq, d = x.shape
    depth = ln_g.shape[0]
    n_a = rw_mu.shape[0]
    alpha = (2 * depth) ** 0.25
    l_real = seq + N_META
    lp = -(-l_real // SEQ_ALIGN) * SEQ_ALIGN
    t = batch * lp
    nh = d // FX_HEAD

    meta = jnp.broadcast_to(meta_tokens.astype(x.dtype)[None], (batch, N_META, d))
    h = jnp.concatenate([meta, x, jnp.zeros((batch, lp - l_real, d), x.dtype)], axis=1).reshape(t, d)

    wb = (ffn_w1[0, 0].astype(BF16), ffn_w3[0, 0].astype(BF16), ffn_w2[0, 0].astype(BF16))

    def ffn(h, wb, l, s):
        last = l == depth - 1 and s == 1
        nxt = None if last else (ffn_w1, ffn_w3, ffn_w2, l + s, 1 - s)
        return _ffn_ln(h, *wb, ln_g[l, 2 * s], ln_b[l, 2 * s], alpha, nxt=nxt)

    k_s = vt_s = c = None
    for l in range(depth):
        h, wb = ffn(h, wb, l, 0)
        if l < n_a:
            h = _rwkv7_mix(h, batch, rw_mu[l], rw_w_rkv[l], rw_w_o[l], rw_w0[l], rw_w1[l], rw_w2[l],
                           rw_a0[l], rw_a1[l], rw_a2[l], rw_g1[l], rw_g2[l], rw_k_k[l], rw_k_a[l],
                           rw_r_k[l], rw_lnx_g[l], rw_lnx_b[l], ln_g[l, 1], ln_b[l, 1], alpha)
        else:
            j = l - n_a
            q = _matmul(h, fx_w_q[j].astype(BF16), act="scale", act_arg=FOX_Q_SCALE, out_dtype=BF16,
                        tn=1024, name="q_proj")
            o = _fox_attention(q, k_s, vt_s, c, batch)
            h = _matmul_ln(o, fx_w_o[j].astype(BF16), h, ln_g[l, 1], ln_b[l, 1], alpha)
        h, wb = ffn(h, wb, l, 1)
        if l == n_a - 1:
            w_kvf = fx_w_kvf[:, :2 * d].astype(BF16)
            k_s = _matmul(h, w_kvf, n=d, out_dtype=BF16, tn=1024, name="k_proj")
            vt_s = _matmul(h, w_kvf, n=d, woff=d, out_dtype=BF16, tm=512, tn=1024, transpose_out=True,
                           name="vt_proj")
            c = _fgate_cumsum(h, fx_w_kvf[:, 2 * d:], fx_b_f, batch)[:, :nh].reshape(batch, lp, nh)
    return h.reshape(batch, lp, d)[:, N_META:l_real]
```

```python
import functools

import jax
import jax.numpy as jnp
from jax import lax
from jax.experimental import pallas as pl
from jax.experimental.pallas import tpu as pltpu

N_META = 16
RW_HEAD = 64
FX_HEAD = 128
LN_EPS = 1e-5
RW_GN_EPS = RW_HEAD * 1e-5
NEG_INF = -1e30
LOG2E = 1.4426950408889634

LANES = 128
SEQ_ALIGN = 128
WKV_CHUNK = 64
WKV_PAIRS_PER_STEP = 16
FOX_HEADS_PER_STEP = 8
VMEM_LIMIT = 56 * 1024 * 1024

F32 = jnp.float32
BF16 = jnp.bfloat16


def _cparams(sem):
    return pltpu.CompilerParams(dimension_semantics=sem, vmem_limit_bytes=VMEM_LIMIT)


def _dot(a, b):
    return jnp.dot(a, b, preferred_element_type=F32)


def _dot_nt(a, b):
    return lax.dot_general(a, b, (((1,), (1,)), ((), ())), preferred_element_type=F32)


def _split(x):
    hi = x.astype(BF16)
    lo = (x - hi.astype(F32)).astype(BF16)
    return hi, lo


def _softplus(x):
    return jnp.maximum(x, 0.0) + jnp.log1p(jnp.exp(-jnp.abs(x)))


def _sigmoid(x):
    return 1.0 / (1.0 + jnp.exp(-x))


def _layer_norm(y, g, b):
    mu = jnp.mean(y, axis=-1, keepdims=True)
    yc = y - mu
    var = jnp.mean(yc * yc, axis=-1, keepdims=True)
    return yc * lax.rsqrt(var + LN_EPS) * g + b


def _mm_kernel(x_ref, w_ref, b_ref, o_ref, *scratch, act, act_arg, transpose_out):
    if scratch:
        xb_ref, = scratch

        @pl.when(pl.program_id(1) == 0)
        def _():
            xb_ref[...] = x_ref[...].astype(BF16)

        x = xb_ref[...]
    else:
        x = x_ref[...]
    acc = _dot(x, w_ref[...]) + b_ref[...]
    if act == "scale":
        acc = acc * act_arg
    if act == "tanh":
        acc = jnp.tanh(acc)
    elif act == "sigmoid":
        acc = _sigmoid(acc)
    elif act == "decay":
        acc = -_softplus(-acc) - 0.5
    if transpose_out:
        acc = acc.T
    o_ref[...] = acc.astype(o_ref.dtype)


def _matmul(x, w, bias=None, act=None, act_arg=None, out_dtype=F32, tm=None, tn=None, xblk=0, n=None, woff=0,
            transpose_out=False, name="matmul"):
    t = x.shape[0]
    k = w.shape[0]
    n = n or w.shape[1]
    tm = tm or _pick_tm(t)
    tn = tn or min(n, 512)
    if bias is None:
        bias = jnp.zeros((1, n), F32)
    cast = x.dtype != BF16
    jo = woff // tn
    if transpose_out:
        out_spec, out_shape = pl.BlockSpec((tn, tm), lambda i, j: (j, i)), (n, t)
    else:
        out_spec, out_shape = pl.BlockSpec((tm, tn), lambda i, j: (i, j)), (t, n)
    return pl.pallas_call(
        functools.partial(_mm_kernel, act=act, act_arg=act_arg, transpose_out=transpose_out),
        grid=(t // tm, n // tn),
        in_specs=[pl.BlockSpec((tm, k), lambda i, j: (i, xblk)),
                  pl.BlockSpec((k, tn), lambda i, j: (0, j + jo)),
                  pl.BlockSpec((1, tn), lambda i, j: (0, j))],
        out_specs=out_spec,
        out_shape=jax.ShapeDtypeStruct(out_shape, out_dtype),
        scratch_shapes=[pltpu.VMEM((tm, k), BF16)] if cast else [],
        name=name,
        compiler_params=_cparams(("parallel", "arbitrary" if cast else "parallel")),
    )(x, w, bias.reshape(1, n).astype(F32))


def _pick_tm(t, cap=1088):
    for tm in range(min(cap, t), 0, -16):
        if t % tm == 0:
            return tm
    return t


PREV_ROWS = 8


def _rwkv_in_kernel(h_ref, prev_ref, mu_ref, w_ref, lw_ref, rkv_ref, lora_ref, xs_ref, *, lp, nj, r1):
    i = pl.program_id(0)
    j = pl.program_id(1)
    tm = h_ref.shape[0]

    @pl.when(j == 0)
    def _():
        h = h_ref[...]
        last = prev_ref[PREV_ROWS - 1:, :]
        last = jnp.where(lax.rem(i * tm, lp) == 0, 0.0, last)
        row = lax.broadcasted_iota(jnp.int32, h.shape, 0)
        xx = jnp.where(row == 0, last, pltpu.roll(h, 1, 0)) - h

        def mix(n):
            return (h + xx * mu_ref[n:n + 1, :]).astype(BF16)

        xs_ref[0] = mix(0)
        xs_ref[1] = mix(2)
        xs_ref[2] = mix(3)
        tw = jnp.tanh(_dot(mix(1), lw_ref[:, :r1]))
        ta = _dot(mix(4), lw_ref[:, r1:2 * r1])
        tg = _sigmoid(_dot(mix(5), lw_ref[:, 2 * r1:]))
        lora_ref[...] = jnp.concatenate([tw, ta, tg], axis=1).astype(BF16)

    rkv_ref[...] = _dot(xs_ref[j // nj], w_ref[...])


def _rwkv_in(h, mu, w_rkv, lw, batch, r1, tm=None, tn=512):
    t, d = h.shape
    lp = t // batch
    tm = tm or _pick_tm(lp, 1088)
    nj = d // tn
    nl = lw.shape[1]
    pblk = tm // PREV_ROWS
    return pl.pallas_call(
        functools.partial(_rwkv_in_kernel, lp=lp, nj=nj, r1=r1),
        grid=(t // tm, 3 * nj),
        in_specs=[pl.BlockSpec((tm, d), lambda i, j: (i, 0)),
                  pl.BlockSpec((PREV_ROWS, d), lambda i, j: (jnp.maximum(i * pblk - 1, 0), 0)),
                  pl.BlockSpec((6, d), lambda i, j: (0, 0)),
                  pl.BlockSpec((None, d, tn), lambda i, j: (j // nj, 0, j % nj)),
                  pl.BlockSpec((d, nl), lambda i, j: (0, 0))],
        out_specs=[pl.BlockSpec((None, tm, tn), lambda i, j: (j // nj, i, j % nj)),
                   pl.BlockSpec((tm, nl), lambda i, j: (i, 0))],
        out_shape=[jax.ShapeDtypeStruct((3, t, d), F32), jax.ShapeDtypeStruct((t, nl), BF16)],
        scratch_shapes=[pltpu.VMEM((3, tm, d), BF16)],
        name="rwkv_in",
        compiler_params=_cparams(("parallel", "arbitrary")),
    )(h, h, mu, w_rkv, lw)


def _mm_ln_kernel(x_ref, w_ref, h_ref, g_ref, b_ref, o_ref, *, alpha, splits):
    rs = x_ref.shape[0] // splits
    rows = [slice(n * rs, (n + 1) * rs) for n in range(splits)]
    ys = [alpha * h_ref[r, :] + _dot(x_ref[r, :], w_ref[...]) for r in rows]
    for r, y in zip(rows, ys):
        o_ref[r, :] = _layer_norm(y, g_ref[...], b_ref[...])


def _matmul_ln(x, w, h, g, b, alpha, tm=None):
    t, k = x.shape
    n = w.shape[1]
    tm = tm or _pick_tm(t, 512)
    return pl.pallas_call(
        functools.partial(_mm_ln_kernel, alpha=alpha, splits=4 if tm % 64 == 0 else 1),
        grid=(t // tm,),
        in_specs=[pl.BlockSpec((tm, k), lambda i: (i, 0)),
                  pl.BlockSpec((k, n), lambda i: (0, 0)),
                  pl.BlockSpec((tm, n), lambda i: (i, 0)),
                  pl.BlockSpec((1, n), lambda i: (0, 0)),
                  pl.BlockSpec((1, n), lambda i: (0, 0))],
        out_specs=pl.BlockSpec((tm, n), lambda i: (i, 0)),
        out_shape=jax.ShapeDtypeStruct((t, n), F32),
        name="proj_ln",
        compiler_params=_cparams(("parallel",)),
    )(x, w, h, g.reshape(1, n), b.reshape(1, n))


def _ffn_ln_kernel(h_ref, w1_ref, w3_ref, w2_ref, g_ref, b_ref, *rest, alpha):
    o_ref, hb_ref = rest[-2 if len(rest) == 2 else 3], rest[-1]
    j = pl.program_id(1)

    @pl.when(j == 0)
    def _():
        hb_ref[...] = h_ref[...].astype(BF16)
        o_ref[...] = jnp.zeros_like(o_ref)

    def cast_next_weights():
        for src, dst in zip(rest[:3], rest[4:7]) if len(rest) > 2 else ():
            dst[...] = src[...].astype(BF16)

    def partial_out(rows):
        hb = hb_ref[rows, :]
        u = _dot(hb, w1_ref[...])
        v = _dot(hb, w3_ref[...])
        return _dot((u * _sigmoid(u) * v).astype(BF16), w2_ref[...])

    last = pl.num_programs(1) - 1
    tm = hb_ref.shape[0]

    @pl.when(j < last)
    def _():
        cast_next_weights()
        o_ref[...] += partial_out(slice(0, tm))

    @pl.when(j == last)
    def _():
        cast_next_weights()
        for rows in (slice(0, tm // 2), slice(tm // 2, tm)):
            y = alpha * h_ref[rows, :] + 0.5 * (o_ref[rows, :] + partial_out(rows))
            o_ref[rows, :] = _layer_norm(y, g_ref[...], b_ref[...])


def _ffn_ln(h, w1, w3, w2, g, b, alpha, nxt=None, tm=None, tf=256):
    t, d = h.shape
    f = w1.shape[-1]
    tm = tm or _pick_tm(t, 1088)
    ni, nj = t // tm, f // tf
    in_specs = [pl.BlockSpec((tm, d), lambda i, j: (i, 0)),
                pl.BlockSpec((d, tf), lambda i, j: (0, j)),
                pl.BlockSpec((d, tf), lambda i, j: (0, j)),
                pl.BlockSpec((tf, d), lambda i, j: (j, 0)),
                pl.BlockSpec((1, d), lambda i, j: (0, 0)),
                pl.BlockSpec((1, d), lambda i, j: (0, 0))]
    out_specs = [pl.BlockSpec((tm, d), lambda i, j: (i, 0))]
    out_shape = [jax.ShapeDtypeStruct((t, d), F32)]
    args = [h, w1, w3, w2, g.reshape(1, d), b.reshape(1, d)]
    if nxt is not None:
        n1, n3, n2, l, s = nxt
        dr = d // ni
        assert dr * ni == d and dr % LANES == 0
        in_specs += [pl.BlockSpec((None, None, dr, tf), lambda i, j: (l, s, i, j)),
                     pl.BlockSpec((None, None, dr, tf), lambda i, j: (l, s, i, j)),
                     pl.BlockSpec((None, None, tf, dr), lambda i, j: (l, s, j, i))]
        out_specs += [pl.BlockSpec((dr, tf), lambda i, j: (i, j)),
                      pl.BlockSpec((dr, tf), lambda i, j: (i, j)),
                      pl.BlockSpec((tf, dr), lambda i, j: (j, i))]
        out_shape += [jax.ShapeDtypeStruct((d, f), BF16), jax.ShapeDtypeStruct((d, f), BF16),
                      jax.ShapeDtypeStruct((f, d), BF16)]
        args += [n1, n3, n2]
    out = pl.pallas_call(
        functools.partial(_ffn_ln_kernel, alpha=alpha),
        grid=(ni, nj),
        in_specs=in_specs,
        out_specs=out_specs,
        out_shape=out_shape,
        scratch_shapes=[pltpu.VMEM((tm, d), BF16)],
        name="ffn_ln",
        compiler_params=_cparams(("parallel", "arbitrary")),
    )(*args)
    return out[0], tuple(out[1:])


def _cumsum_rows(x):
    n = x.shape[0]
    row = lax.broadcasted_iota(jnp.int32, x.shape, 0)
    s = 1
    while s < n:
        x = x + jnp.where(row >= s, pltpu.roll(x, s, 0), 0.0)
        s *= 2
    return x


def _mm1(a, b):
    return _dot(a.astype(BF16), b.astype(BF16))


def _wkv_chunk(tiles, params, states, c):
    m0, strict, incl, ones_bd = c
    cc = tiles[0][0].shape[0]
    c2 = 2 * cc
    n = len(tiles)
    inv_n = 1.0 / RW_HEAD

    def stack2(x):
        return jnp.concatenate([jnp.where(m0, x, 0.0), jnp.where(m0, 0.0, x)], axis=0)

    def segsum(xs):
        return [_dot(x.astype(BF16), ones_bd) for x in xs]

    kk0 = [t[2] * p[0] for t, p in zip(tiles, params)]
    nrm2 = segsum([x * x for x in kk0])
    kmod = [t[2] * (1.0 + (t[4] - 1.0) * p[1]) for t, p in zip(tiles, params)]
    bon = segsum([t[0] * km * p[2] for t, km, p in zip(tiles, kmod, params)])

    lhs, rhs, vs, upd_r, dec = [], [], [], [], []
    for i in range(n):
        r, wl, k, v, a, g = tiles[i]
        kk = kk0[i] / jnp.maximum(jnp.sqrt(nrm2[i]), 1e-12)
        b = kk * a
        lw = -jnp.exp(wl)
        lcum = _cumsum_rows(lw)
        ltot = lcum[cc - 1:cc, :]
        e_neg = jnp.exp(-lcum)
        e_rem = jnp.exp(ltot - lcum)
        kt = kk * jnp.exp(lcum - lw)
        rt = r * jnp.exp(lcum)
        khb = (kmod[i] * e_neg).astype(BF16)
        bhb = (b * e_neg).astype(BF16)
        lhs.append(jnp.concatenate([stack2(kt), stack2(rt)], axis=0).astype(BF16))
        rhs.append(jnp.concatenate([khb, khb, bhb, bhb, states[i].astype(BF16)], axis=0))
        vs.append(stack2(v))
        upd_r.append(jnp.concatenate([stack2(kmod[i] * e_rem), stack2(b * e_rem)], axis=0).astype(BF16))
        dec.append(jnp.exp(ltot))

    m = [_dot_nt(a_, b_) for a_, b_ in zip(lhs, rhs)]
    a_k = [jnp.where(strict, x[:c2, :c2], 0.0) for x in m]
    a_b = [jnp.where(strict, x[:c2, c2:2 * c2], 0.0) for x in m]
    bkb = [jnp.concatenate([jnp.where(incl, x[c2:, :c2], 0.0), jnp.where(incl, -x[c2:, c2:2 * c2], 0.0)],
                           axis=1).astype(BF16) for x in m]
    rhs_u = [x[:c2, 2 * c2:] + _mm1(ak, v_) for x, ak, v_ in zip(m, a_k, vs)]
    pw = [ab.astype(BF16) for ab in a_b]
    st = [_dot(p_, jnp.concatenate([p_, ru.astype(BF16)], axis=1)) for p_, ru in zip(pw, rhs_u)]
    pw = [s_[:, :c2].astype(BF16) for s_ in st]
    us = [ru - s_[:, c2:] for ru, s_ in zip(rhs_u, st)]
    lvl = 4
    while lvl < cc:
        st = [_dot(p_, jnp.concatenate([p_, u_.astype(BF16)], axis=1)) for p_, u_ in zip(pw, us)]
        pw = [s_[:, :c2].astype(BF16) for s_ in st]
        us = [u_ + s_[:, c2:] for u_, s_ in zip(us, st)]
        lvl *= 2
    us = [u_ + _dot(p_, u_.astype(BF16)) for u_, p_ in zip(us, pw)]
    ys = [x[c2:, 2 * c2:] + _dot(bk, jnp.concatenate([v_, u_], axis=0).astype(BF16))
          for x, bk, v_, u_ in zip(m, bkb, vs, us)]
    y = [x[:cc] + x[cc:] for x in ys]
    upd_l = [jnp.concatenate([v_, -u_], axis=0).T.astype(BF16) for v_, u_ in zip(vs, us)]
    s_new = [s0 * d_ + _dot(ul, ur) for s0, d_, ul, ur in zip(states, dec, upd_l, upd_r)]

    mu = [x * inv_n for x in segsum(y)]
    yc = [a_ - b_ for a_, b_ in zip(y, mu)]
    var = [x * inv_n for x in segsum([x * x for x in yc])]
    outs = []
    for i in range(n):
        yn = yc[i] * lax.rsqrt(var[i] + RW_GN_EPS) * params[i][3] + params[i][4]
        outs.append((yn + bon[i] * tiles[i][3]) * tiles[i][5])
    return outs, s_new


def _wkv_kernel(r_ref, k_ref, v_ref, lora_ref, w2_ref, a2_ref, g2_ref, w0_ref, a0_ref,
                kk_ref, ka_ref, rk_ref, lg_ref, lb_ref, o_ref, s_ref, *, pairs, r1):
    @pl.when(pl.program_id(2) == 0)
    def _():
        s_ref[...] = jnp.zeros_like(s_ref)

    lo = lora_ref[...]

    cc = r_ref.shape[0]
    c2 = 2 * cc
    lane = lax.broadcasted_iota(jnp.int32, (cc, LANES), 1)
    m0 = lane < RW_HEAD
    row = lax.broadcasted_iota(jnp.int32, (c2, c2), 0)
    col = lax.broadcasted_iota(jnp.int32, (c2, c2), 1)
    same = (row >= cc) == (col >= cc)
    strict = same & (col < row)
    incl = same & (col <= row)
    lr = lax.broadcasted_iota(jnp.int32, (LANES, LANES), 0)
    lc = lax.broadcasted_iota(jnp.int32, (LANES, LANES), 1)
    ones_bd = jnp.where((lr >= RW_HEAD) == (lc >= RW_HEAD), 1.0, 0.0).astype(BF16)
    consts = (m0, strict, incl, ones_bd)

    w_all = -_softplus(-(_dot(lo[:, :r1], w2_ref[...]) + w0_ref[...])) - 0.5
    a_all = _sigmoid(_dot(lo[:, r1:2 * r1], a2_ref[...]) + a0_ref[...])
    g_all = _dot(lo[:, 2 * r1:], g2_ref[...])

    sls = [slice(p * LANES, (p + 1) * LANES) for p in range(pairs)]
    tiles = [(r_ref[:, sl], w_all[:, sl], k_ref[:, sl], v_ref[:, sl], a_all[:, sl], g_all[:, sl]) for sl in sls]
    params = [tuple(ref[:, sl] for ref in (kk_ref, ka_ref, rk_ref, lg_ref, lb_ref)) for sl in sls]
    outs, s_new = _wkv_chunk(tiles, params, [s_ref[p] for p in range(pairs)], consts)
    for p in range(pairs):
        s_ref[p] = s_new[p]
        o_ref[:, sls[p]] = outs[p].astype(o_ref.dtype)


def _wkv(rkv, lora, w2, a2, g2, w0, a0, kkp, kap, rkp, lgp, lbp, batch, r1, out_dtype=BF16):
    _, t, d = rkv.shape
    nl = lora.shape[1]
    lp = t // batch
    cc = WKV_CHUNK
    nchunk = lp // cc
    pairs = min(WKV_PAIRS_PER_STEP, d // LANES)
    wblk = pairs * LANES
    tok = pl.BlockSpec((cc, wblk), lambda b, p, c: (b * nchunk + c, p))
    r_spec, k_spec, v_spec = [pl.BlockSpec((None, cc, wblk), lambda b, p, c, n=n: (n, b * nchunk + c, p))
                              for n in range(3)]
    lo_spec = pl.BlockSpec((cc, nl), lambda b, p, c: (b * nchunk + c, 0))
    up = [pl.BlockSpec((w.shape[0], wblk), lambda b, p, c: (0, p)) for w in (w2, a2, g2)]
    par = pl.BlockSpec((1, wblk), lambda b, p, c: (0, p))
    prm = [x.reshape(1, d) for x in (w0, a0, kkp, kap, rkp, lgp, lbp)]
    return pl.pallas_call(
        functools.partial(_wkv_kernel, pairs=pairs, r1=r1),
        grid=(batch, d // wblk, nchunk),
        in_specs=[r_spec, k_spec, v_spec, lo_spec] + up + [par] * 7,
        out_specs=tok,
        out_shape=jax.ShapeDtypeStruct((t, d), out_dtype),
        scratch_shapes=[pltpu.VMEM((pairs, LANES, LANES), F32)],
        name="wkv7",
        compiler_params=_cparams(("parallel", "parallel", "arbitrary")),
    )(rkv, rkv, rkv, lora, w2, a2, g2, *prm)


def _fgate_kernel(h_ref, wh_ref, wl_ref, b_ref, o_ref, carry_ref):
    @pl.when(pl.program_id(1) == 0)
    def _():
        carry_ref[...] = jnp.zeros_like(carry_ref)

    hh, hl = _split(h_ref[...])
    z = _dot(hh, wh_ref[...]) + (_dot(hh, wl_ref[...]) + _dot(hl, wh_ref[...])) + b_ref[...]
    logf = -_softplus(-z) * LOG2E
    cs = _cumsum_rows(logf) + carry_ref[...]
    o_ref[...] = cs
    carry_ref[...] = cs[cs.shape[0] - 1:, :]


def _fgate_cumsum(h, wf, bf, batch, tm=544):
    t, d = h.shape
    lp = t // batch
    nt = lp // tm
    nh = wf.shape[1]
    wpad = jnp.zeros((d, LANES), F32).at[:, :nh].set(wf)
    bpad = jnp.zeros((1, LANES), F32).at[0, :nh].set(bf)
    wh, wl = _split(wpad)
    return pl.pallas_call(
        _fgate_kernel,
        grid=(batch, nt),
        in_specs=[pl.BlockSpec((tm, d), lambda b, i: (b * nt + i, 0)),
                  pl.BlockSpec((d, LANES), lambda b, i: (0, 0)),
                  pl.BlockSpec((d, LANES), lambda b, i: (0, 0)),
                  pl.BlockSpec((1, LANES), lambda b, i: (0, 0))],
        out_specs=pl.BlockSpec((tm, LANES), lambda b, i: (b * nt + i, 0)),
        out_shape=jax.ShapeDtypeStruct((t, LANES), F32),
        scratch_shapes=[pltpu.VMEM((1, LANES), F32)],
        name="fgate_cumsum",
        compiler_params=_cparams(("parallel", "arbitrary")),
    )(h, wh, wl, bpad)


def _fox_kernel(q_ref, k_ref, vt_ref, cc_ref, cr_ref, o_ref, *, heads, first, tile):
    lp = q_ref.shape[0]
    nbig = (lp - first) // tile
    hs = [slice(h * FX_HEAD, (h + 1) * FX_HEAD) for h in range(heads)]

    def scores(q, kstart, ksize):
        return [_dot_nt(k_ref[pl.ds(kstart, ksize), hs[h]], q[h]) for h in range(heads)]

    def update(s, cq, carry, kstart, ksize, diag):
        s = [s[h] + cq[h] - cc_ref[0, 0, pl.ds(kstart, ksize), h:h + 1] for h in range(heads)]
        if diag:
            keep = (lax.broadcasted_iota(jnp.int32, (ksize, ksize), 0)
                    <= lax.broadcasted_iota(jnp.int32, (ksize, ksize), 1))
            s = [jnp.where(keep, x, NEG_INF) for x in s]
        m_new = [jnp.maximum(carry[h][0], jnp.max(s[h], axis=0, keepdims=True)) for h in range(heads)]
        alpha = [jnp.exp2(carry[h][0] - m_new[h]) for h in range(heads)]
        p = [jnp.exp2(s[h] - m_new[h]) for h in range(heads)]
        l_new = [alpha[h] * carry[h][1] + jnp.sum(p[h], axis=0, keepdims=True) for h in range(heads)]
        pv = [_dot(vt_ref[hs[h], pl.ds(kstart, ksize)], p[h].astype(BF16)) for h in range(heads)]
        return [(m_new[h], l_new[h], alpha[h] * carry[h][2] + pv[h]) for h in range(heads)]

    def q_tile(qstart, tq, nfull, with_first):
        q = [q_ref[pl.ds(qstart, tq), hs[h]] for h in range(heads)]
        cq = [cr_ref[0, h, :, pl.ds(qstart, tq)] for h in range(heads)]
        carry = [(jnp.full((1, tq), NEG_INF, F32), jnp.zeros((1, tq), F32), jnp.zeros((FX_HEAD, tq), F32))
                 for _ in range(heads)]
        def step(c, kstart, ksize, diag):
            return update(scores(q, kstart, ksize), cq, c, kstart, ksize, diag)

        def step2(j, c):
            k0 = pl.multiple_of(first + 2 * j * tile, LANES)
            k1 = pl.multiple_of(first + (2 * j + 1) * tile, LANES)
            s0, s1 = scores(q, k0, tile), scores(q, k1, tile)
            return update(s1, cq, update(s0, cq, c, k0, tile, False), k1, tile, False)

        if nfull is not None:
            carry = lax.fori_loop(0, lax.shift_right_logical(nfull, 1), step2, carry)
            carry = lax.fori_loop(
                0, nfull & 1,
                lambda _, c: step(c, pl.multiple_of(first + (nfull - 1) * tile, LANES), tile, False), carry)
        s_first = scores(q, 0, first) if with_first else None
        s_diag = scores(q, qstart, tq)
        if with_first:
            carry = update(s_first, cq, carry, 0, first, False)
        carry = update(s_diag, cq, carry, qstart, tq, True)
        for h in range(heads):
            o_ref[pl.ds(qstart, tq), hs[h]] = (carry[h][2] / carry[h][1]).T.astype(o_ref.dtype)

    if first:
        q_tile(0, first, None, False)

    def big(i, _):
        q_tile(pl.multiple_of(first + i * tile, LANES), tile, i, first > 0)
        return 0

    lax.fori_loop(0, nbig, big, 0)


FOX_Q_SCALE = FX_HEAD ** -0.5 * LOG2E


def _fox_attention(q, k, vt, c, batch, heads=FOX_HEADS_PER_STEP, tile=256):
    t, d = q.shape
    lp = t // batch
    nh = d // FX_HEAD
    ng = nh // heads
    wblk = heads * FX_HEAD
    first = lp % tile
    c_col = jnp.transpose(c.reshape(batch, lp, ng, heads), (0, 2, 1, 3))
    c_row = jnp.transpose(c, (0, 2, 1))[:, :, None, :]
    return pl.pallas_call(
        functools.partial(_fox_kernel, heads=heads, first=first, tile=tile),
        grid=(batch, ng),
        in_specs=[pl.BlockSpec((lp, wblk), lambda b, g: (b, g)),
                  pl.BlockSpec((lp, wblk), lambda b, g: (b, g)),
                  pl.BlockSpec((wblk, lp), lambda b, g: (g, b)),
                  pl.BlockSpec((1, 1, lp, heads), lambda b, g: (b, g, 0, 0)),
                  pl.BlockSpec((1, heads, 1, lp), lambda b, g: (b, g, 0, 0))],
        out_specs=pl.BlockSpec((lp, wblk), lambda b, g: (b, g)),
        out_shape=jax.ShapeDtypeStruct((t, d), BF16),
        name="fox_attention",
        compiler_params=_cparams(("parallel", "parallel")),
    )(q, k, vt, c_col, c_row)


def _pad_cols(w, n):
    return jnp.zeros((w.shape[0], n), w.dtype).at[:, :w.shape[1]].set(w)


def _pad_rows(w, n):
    return jnp.zeros((n, w.shape[1]), w.dtype).at[:w.shape[0], :].set(w)


def _rwkv7_mix(h, batch, mu, w_rkv, w_o, w0, w1, w2, a0, a1, a2, g1, g2, k_k, k_a, r_k, lnx_g, lnx_b,
               ln_g, ln_b, alpha):
    r1 = LANES
    lw = jnp.concatenate([_pad_cols(w1, r1), _pad_cols(a1, r1), g1], axis=1).astype(BF16)
    rkv, lora = _rwkv_in(h, mu, w_rkv.astype(BF16), lw, batch, r1)
    o = _wkv(rkv, lora, _pad_rows(w2, r1).astype(BF16), _pad_rows(a2, r1).astype(BF16), g2.astype(BF16),
             w0, a0, k_k, k_a, r_k.reshape(-1), lnx_g, lnx_b, batch, r1)
    return _matmul_ln(o, w_o.astype(BF16), h, ln_g, ln_b, alpha)


def kernel(x, meta_tokens, ln_g, ln_b, ffn_w1, ffn_w3, ffn_w2, rw_mu, rw_w_rkv, rw_w_o, rw_w0, rw_w1,
           rw_w2, rw_a0, rw_a1, rw_a2, rw_g1, rw_g2, rw_k_k, rw_k_a, rw_r_k, rw_lnx_g, rw_lnx_b,
           fx_w_q, fx_w_o, fx_w_kvf, fx_b_f):
    batch, seq, d = x.shape
    depth = ln_g.shape[0]
    n_a = rw_mu.shape[0]
    alpha = (2 * depth) ** 0.25
    l_real = seq + N_META
    lp = -(-l_real // SEQ_ALIGN) * SEQ_ALIGN
    t = batch * lp
    nh = d // FX_HEAD

    meta = jnp.broadcast_to(meta_tokens.astype(x.dtype)[None], (batch, N_META, d))
    h = jnp.concatenate([meta, x, jnp.zeros((batch, lp - l_real, d), x.dtype)], axis=1).reshape(t, d)

    wb = (ffn_w1[0, 0].astype(BF16), ffn_w3[0, 0].astype(BF16), ffn_w2[0, 0].astype(BF16))

    def ffn(h, wb, l, s):
        last = l == depth - 1 and s == 1
        nxt = None if last else (ffn_w1, ffn_w3, ffn_w2, l + s, 1 - s)
        return _ffn_ln(h, *wb, ln_g[l, 2 * s], ln_b[l, 2 * s], alpha, nxt=nxt)

    k_s = vt_s = c = None
    for l in range(depth):
        h, wb = ffn(h, wb, l, 0)
        if l < n_a:
            h = _rwkv7_mix(h, batch, rw_mu[l], rw_w_rkv[l], rw_w_o[l], rw_w0[l], rw_w1[l], rw_w2[l],
                           rw_a0[l], rw_a1[l], rw_a2[l], rw_g1[l], rw_g2[l], rw_k_k[l], rw_k_a[l],
                           rw_r_k[l], rw_lnx_g[l], rw_lnx_b[l], ln_g[l, 1], ln_b[l, 1], alpha)
        else:
            j = l - n_a
            q = _matmul(h, fx_w_q[j].astype(BF16), act="scale", act_arg=FOX_Q_SCALE, out_dtype=BF16,
                        tn=1024, name="q_proj")
            o = _fox_attention(q, k_s, vt_s, c, batch)
            h = _matmul_ln(o, fx_w_o[j].astype(BF16), h, ln_g[l, 1], ln_b[l, 1], alpha)
        h, wb = ffn(h, wb, l, 1)
        if l == n_a - 1:
            w_kvf = fx_w_kvf[:, :2 * d].astype(BF16)
            k_s = _matmul(h, w_kvf, n=d, out_dtype=BF16, tn=1024, name="k_proj")
            vt_s = _matmul(h, w_kvf, n=d, woff=d, out_dtype=BF16, tm=512, tn=1024, transpose_out=True,
                           name="vt_proj")
            c = _fgate_cumsum(h, fx_w_kvf[:, 2 * d:], fx_b_f, batch)[:, :nh].reshape(batch, lp, nh)
    return h.reshape(batch, lp, d)[:, N_META:l_real]
```

```python
import functools

import jax
import jax.numpy as jnp
from jax import lax
from jax.experimental import pallas as pl
from jax.experimental.pallas import tpu as pltpu

N_META = 16
RW_HEAD = 64
FX_HEAD = 128
LN_EPS = 1e-5
RW_GN_EPS = RW_HEAD * 1e-5
NEG_INF = -1e30
LOG2E = 1.4426950408889634
DECAY_SCALE = 0.6065306597126334

LANES = 128
SEQ_ALIGN = 128
WKV_CHUNK = 64
WKV_PAIRS_PER_STEP = 16
FOX_HEADS_PER_STEP = 8
VMEM_LIMIT = 56 * 1024 * 1024

F32 = jnp.float32
BF16 = jnp.bfloat16


def _cparams(sem):
    return pltpu.CompilerParams(dimension_semantics=sem, vmem_limit_bytes=VMEM_LIMIT)


def _dot(a, b):
    return jnp.dot(a, b, preferred_element_type=F32)


def _dot_nt(a, b):
    return lax.dot_general(a, b, (((1,), (1,)), ((), ())), preferred_element_type=F32)


def _split(x):
    hi = x.astype(BF16)
    lo = (x - hi.astype(F32)).astype(BF16)
    return hi, lo


def _softplus(x):
    return jnp.maximum(x, 0.0) + jnp.log1p(jnp.exp(-jnp.abs(x)))


def _sigmoid(x):
    return 1.0 / (1.0 + jnp.exp(-x))


def _layer_norm(y, g, b):
    mu = jnp.mean(y, axis=-1, keepdims=True)
    yc = y - mu
    var = jnp.mean(yc * yc, axis=-1, keepdims=True)
    return yc * lax.rsqrt(var + LN_EPS) * g + b


def _proj_kernel(x_ref, w_ref, o_ref, xb_ref, *, scale, transpose_out):
    @pl.when(pl.program_id(1) == 0)
    def _():
        xb_ref[...] = x_ref[...].astype(BF16)

    acc = _dot(xb_ref[...], w_ref[...])
    if scale is not None:
        acc = acc * scale
    if transpose_out:
        acc = acc.T
    o_ref[...] = acc.astype(o_ref.dtype)


def _proj(x, w, n, name, woff=0, scale=None, transpose_out=False, tm=None, tn=1024):
    t, k = x.shape
    tm = tm or _pick_tm(t)
    jo = woff // tn
    if transpose_out:
        out_spec, out_shape = pl.BlockSpec((tn, tm), lambda i, j: (j, i)), (n, t)
    else:
        out_spec, out_shape = pl.BlockSpec((tm, tn), lambda i, j: (i, j)), (t, n)
    return pl.pallas_call(
        functools.partial(_proj_kernel, scale=scale, transpose_out=transpose_out),
        grid=(t // tm, n // tn),
        in_specs=[pl.BlockSpec((tm, k), lambda i, j: (i, 0)),
                  pl.BlockSpec((k, tn), lambda i, j: (0, j + jo))],
        out_specs=out_spec,
        out_shape=jax.ShapeDtypeStruct(out_shape, BF16),
        scratch_shapes=[pltpu.VMEM((tm, k), BF16)],
        name=name,
        compiler_params=_cparams(("parallel", "arbitrary")),
    )(x, w)


def _pick_tm(t, cap=1088):
    for tm in range(min(cap, t), 0, -16):
        if t % tm == 0:
            return tm
    return t


PREV_ROWS = 8


def _rwkv_in_kernel(h_ref, prev_ref, mu_ref, w_ref, lw_ref, rkv_ref, lora_ref, xs_ref, *, lp, nj, r1):
    i = pl.program_id(0)
    j = pl.program_id(1)
    tm = h_ref.shape[0]

    @pl.when(j == 0)
    def _():
        h = h_ref[...]
        last = prev_ref[PREV_ROWS - 1:, :]
        last = jnp.where(lax.rem(i * tm, lp) == 0, 0.0, last)
        row = lax.broadcasted_iota(jnp.int32, h.shape, 0)
        xx = jnp.where(row == 0, last, pltpu.roll(h, 1, 0)) - h

        def mix(n):
            return (h + xx * mu_ref[n:n + 1, :]).astype(BF16)

        xs_ref[0] = mix(0)
        xs_ref[1] = mix(2)
        xs_ref[2] = mix(3)
        tw = jnp.tanh(_dot(mix(1), lw_ref[:, :r1]))
        ta = _dot(mix(4), lw_ref[:, r1:2 * r1])
        tg = _sigmoid(_dot(mix(5), lw_ref[:, 2 * r1:]))
        lora_ref[...] = jnp.concatenate([tw, ta, tg], axis=1).astype(BF16)

    rkv_ref[...] = _dot(xs_ref[j // nj], w_ref[...])


def _rwkv_in(h, mu, w_rkv, lw, batch, r1, tm=None, tn=512):
    t, d = h.shape
    lp = t // batch
    tm = tm or _pick_tm(lp, 1088)
    nj = d // tn
    nl = lw.shape[1]
    pblk = tm // PREV_ROWS
    return pl.pallas_call(
        functools.partial(_rwkv_in_kernel, lp=lp, nj=nj, r1=r1),
        grid=(t // tm, 3 * nj),
        in_specs=[pl.BlockSpec((tm, d), lambda i, j: (i, 0)),
                  pl.BlockSpec((PREV_ROWS, d), lambda i, j: (jnp.maximum(i * pblk - 1, 0), 0)),
                  pl.BlockSpec((6, d), lambda i, j: (0, 0)),
                  pl.BlockSpec((None, d, tn), lambda i, j: (j // nj, 0, j % nj)),
                  pl.BlockSpec((d, nl), lambda i, j: (0, 0))],
        out_specs=[pl.BlockSpec((None, tm, tn), lambda i, j: (j // nj, i, j % nj)),
                   pl.BlockSpec((tm, nl), lambda i, j: (i, 0))],
        out_shape=[jax.ShapeDtypeStruct((3, t, d), F32), jax.ShapeDtypeStruct((t, nl), BF16)],
        scratch_shapes=[pltpu.VMEM((3, tm, d), BF16)],
        name="rwkv_in",
        compiler_params=_cparams(("parallel", "arbitrary")),
    )(h, h, mu, w_rkv, lw)


def _mm_ln_kernel(x_ref, w_ref, h_ref, g_ref, b_ref, o_ref, *, alpha, splits):
    rs = x_ref.shape[0] // splits
    rows = [slice(n * rs, (n + 1) * rs) for n in range(splits)]
    ys = [alpha * h_ref[r, :] + _dot(x_ref[r, :], w_ref[...]) for r in rows]
    for r, y in zip(rows, ys):
        o_ref[r, :] = _layer_norm(y, g_ref[...], b_ref[...])


def _matmul_ln(x, w, h, g, b, alpha, tm=None):
    t, k = x.shape
    n = w.shape[1]
    tm = tm or _pick_tm(t, 512)
    return pl.pallas_call(
        functools.partial(_mm_ln_kernel, alpha=alpha, splits=4 if tm % 64 == 0 else 1),
        grid=(t // tm,),
        in_specs=[pl.BlockSpec((tm, k), lambda i: (i, 0)),
                  pl.BlockSpec((k, n), lambda i: (0, 0)),
                  pl.BlockSpec((tm, n), lambda i: (i, 0)),
                  pl.BlockSpec((1, n), lambda i: (0, 0)),
                  pl.BlockSpec((1, n), lambda i: (0, 0))],
        out_specs=pl.BlockSpec((tm, n), lambda i: (i, 0)),
        out_shape=jax.ShapeDtypeStruct((t, n), F32),
        name="proj_ln",
        compiler_params=_cparams(("parallel",)),
    )(x, w, h, g.reshape(1, n), b.reshape(1, n))


def _ffn_ln_kernel(h_ref, w1_ref, w3_ref, w2_ref, g_ref, b_ref, *rest, alpha):
    has_next = len(rest) > 2
    nxt_refs, cast_refs = (rest[:3], rest[4:7]) if has_next else ((), ())
    o_ref, hb_ref = rest[3 if has_next else 0], rest[-1]
    j = pl.program_id(1)
    tm = hb_ref.shape[0]

    @pl.when(j == 0)
    def _():
        hb_ref[...] = h_ref[...].astype(BF16)
        o_ref[...] = jnp.zeros_like(o_ref)

    def cast_next_weights():
        for src, dst in zip(nxt_refs, cast_refs):
            dst[...] = src[...].astype(BF16)

    def partial_out(rows):
        hb = hb_ref[rows, :]
        u = _dot(hb, w1_ref[...])
        v = _dot(hb, w3_ref[...])
        return _dot((u * _sigmoid(u) * v).astype(BF16), w2_ref[...])

    last = pl.num_programs(1) - 1

    @pl.when(j < last)
    def _():
        cast_next_weights()
        o_ref[...] += partial_out(slice(0, tm))

    @pl.when(j == last)
    def _():
        cast_next_weights()
        for rows in (slice(0, tm // 2), slice(tm // 2, tm)):
            y = alpha * h_ref[rows, :] + 0.5 * (o_ref[rows, :] + partial_out(rows))
            o_ref[rows, :] = _layer_norm(y, g_ref[...], b_ref[...])


def _ffn_ln(h, w1, w3, w2, g, b, alpha, nxt=None, tm=None, tf=512):
    t, d = h.shape
    f = w1.shape[-1]
    tm = tm or _pick_tm(t, 544)
    ni, nj = t // tm, f // tf
    in_specs = [pl.BlockSpec((tm, d), lambda i, j: (i, 0)),
                pl.BlockSpec((d, tf), lambda i, j: (0, j)),
                pl.BlockSpec((d, tf), lambda i, j: (0, j)),
                pl.BlockSpec((tf, d), lambda i, j: (j, 0)),
                pl.BlockSpec((1, d), lambda i, j: (0, 0)),
                pl.BlockSpec((1, d), lambda i, j: (0, 0))]
    out_specs = [pl.BlockSpec((tm, d), lambda i, j: (i, 0))]
    out_shape = [jax.ShapeDtypeStruct((t, d), F32)]
    args = [h, w1, w3, w2, g.reshape(1, d), b.reshape(1, d)]
    if nxt is not None:
        n1, n3, n2, l, s = nxt
        dr = d // ni
        assert dr * ni == d and dr % LANES == 0
        in_specs += [pl.BlockSpec((None, None, dr, tf), lambda i, j: (l, s, i, j)),
                     pl.BlockSpec((None, None, dr, tf), lambda i, j: (l, s, i, j)),
                     pl.BlockSpec((None, None, tf, dr), lambda i, j: (l, s, j, i))]
        out_specs += [pl.BlockSpec((dr, tf), lambda i, j: (i, j)),
                      pl.BlockSpec((dr, tf), lambda i, j: (i, j)),
                      pl.BlockSpec((tf, dr), lambda i, j: (j, i))]
        out_shape += [jax.ShapeDtypeStruct((d, f), BF16), jax.ShapeDtypeStruct((d, f), BF16),
                      jax.ShapeDtypeStruct((f, d), BF16)]
        args += [n1, n3, n2]
    out = pl.pallas_call(
        functools.partial(_ffn_ln_kernel, alpha=alpha),
        grid=(ni, nj),
        in_specs=in_specs,
        out_specs=out_specs,
        out_shape=out_shape,
        scratch_shapes=[pltpu.VMEM((tm, d), BF16)],
        name="ffn_ln",
        compiler_params=_cparams(("parallel", "arbitrary")),
    )(*args)
    return out[0], tuple(out[1:])


def _cumsum_rows(x):
    n = x.shape[0]
    row = lax.broadcasted_iota(jnp.int32, x.shape, 0)
    s = 1
    while s < n:
        x = x + jnp.where(row >= s, pltpu.roll(x, s, 0), 0.0)
        s *= 2
    return x


def _mm1(a, b):
    return _dot(a.astype(BF16), b.astype(BF16))


def _wkv_chunk(tiles, params, states, c):
    m0, strict, incl, ones_bd = c
    cc = tiles[0][0].shape[0]
    c2 = 2 * cc
    n = len(tiles)
    inv_n = 1.0 / RW_HEAD

    def stack2(x):
        return jnp.concatenate([jnp.where(m0, x, 0.0), jnp.where(m0, 0.0, x)], axis=0)

    def segsum(xs):
        return [_dot(x.astype(BF16), ones_bd) for x in xs]

    kk0 = [t[2] * p[0] for t, p in zip(tiles, params)]
    nrm2 = segsum([x * x for x in kk0])
    kmod = [t[2] * (1.0 + (t[4] - 1.0) * p[1]) for t, p in zip(tiles, params)]
    bon = segsum([t[0] * km * p[2] for t, km, p in zip(tiles, kmod, params)])

    lhs, rhs, vs, upd_r, dec = [], [], [], [], []
    for i in range(n):
        r, lw, k, v, a, g = tiles[i]
        kk = kk0[i] / jnp.maximum(jnp.sqrt(nrm2[i]), 1e-12)
        b = kk * a
        lcum = _cumsum_rows(lw)
        ltot = lcum[cc - 1:cc, :]
        e_neg = jnp.exp(-lcum)
        e_rem = jnp.exp(ltot - lcum)
        kt = kk * jnp.exp(lcum - lw)
        rt = r * jnp.exp(lcum)
        khb = (kmod[i] * e_neg).astype(BF16)
        bhb = (b * e_neg).astype(BF16)
        lhs.append(jnp.concatenate([stack2(kt), stack2(rt)], axis=0).astype(BF16))
        rhs.append(jnp.concatenate([khb, khb, bhb, bhb, states[i].astype(BF16)], axis=0))
        vs.append(stack2(v))
        upd_r.append(jnp.concatenate([stack2(kmod[i] * e_rem), stack2(b * e_rem)], axis=0).astype(BF16))
        dec.append(jnp.exp(ltot))

    m = [_dot_nt(a_, b_) for a_, b_ in zip(lhs, rhs)]
    a_k = [jnp.where(strict, x[:c2, :c2], 0.0) for x in m]
    a_b = [jnp.where(strict, x[:c2, c2:2 * c2], 0.0) for x in m]
    bkb = [jnp.concatenate([jnp.where(incl, x[c2:, :c2], 0.0), jnp.where(incl, -x[c2:, c2:2 * c2], 0.0)],
                           axis=1).astype(BF16) for x in m]
    rhs_u = [x[:c2, 2 * c2:] + _mm1(ak, v_) for x, ak, v_ in zip(m, a_k, vs)]
    pw = [ab.astype(BF16) for ab in a_b]
    st = [_dot(p_, jnp.concatenate([p_, ru.astype(BF16)], axis=1)) for p_, ru in zip(pw, rhs_u)]
    pw = [s_[:, :c2].astype(BF16) for s_ in st]
    us = [ru - s_[:, c2:] for ru, s_ in zip(rhs_u, st)]
    lvl = 4
    while lvl < cc:
        st = [_dot(p_, jnp.concatenate([p_, u_.astype(BF16)], axis=1)) for p_, u_ in zip(pw, us)]
        pw = [s_[:, :c2].astype(BF16) for s_ in st]
        us = [u_ + s_[:, c2:] for u_, s_ in zip(us, st)]
        lvl *= 2
    us = [u_ + _dot(p_, u_.astype(BF16)) for u_, p_ in zip(us, pw)]
    ys = [x[c2:, 2 * c2:] + _dot(bk, jnp.concatenate([v_, u_], axis=0).astype(BF16))
          for x, bk, v_, u_ in zip(m, bkb, vs, us)]
    y = [x[:cc] + x[cc:] for x in ys]
    upd_l = [jnp.concatenate([v_, -u_], axis=0).T.astype(BF16) for v_, u_ in zip(vs, us)]
    s_new = [s0 * d_ + _dot(ul, ur) for s0, d_, ul, ur in zip(states, dec, upd_l, upd_r)]

    mu = [x * inv_n for x in segsum(y)]
    yc = [a_ - b_ for a_, b_ in zip(y, mu)]
    var = [x * inv_n for x in segsum([x * x for x in yc])]
    outs = []
    for i in range(n):
        yn = yc[i] * lax.rsqrt(var[i] + RW_GN_EPS) * params[i][3] + params[i][4]
        outs.append((yn + bon[i] * tiles[i][3]) * tiles[i][5])
    return outs, s_new


def _wkv_kernel(r_ref, k_ref, v_ref, lora_ref, w2_ref, a2_ref, g2_ref, w0_ref, a0_ref,
                kk_ref, ka_ref, rk_ref, lg_ref, lb_ref, o_ref, s_ref, *, pairs, r1):
    @pl.when(pl.program_id(2) == 0)
    def _():
        s_ref[...] = jnp.zeros_like(s_ref)

    lo = lora_ref[...]

    cc = r_ref.shape[0]
    c2 = 2 * cc
    lane = lax.broadcasted_iota(jnp.int32, (cc, LANES), 1)
    m0 = lane < RW_HEAD
    row = lax.broadcasted_iota(jnp.int32, (c2, c2), 0)
    col = lax.broadcasted_iota(jnp.int32, (c2, c2), 1)
    same = (row >= cc) == (col >= cc)
    strict = same & (col < row)
    incl = same & (col <= row)
    lr = lax.broadcasted_iota(jnp.int32, (LANES, LANES), 0)
    lc = lax.broadcasted_iota(jnp.int32, (LANES, LANES), 1)
    ones_bd = jnp.where((lr >= RW_HEAD) == (lc >= RW_HEAD), 1.0, 0.0).astype(BF16)
    consts = (m0, strict, incl, ones_bd)

    lw_all = -DECAY_SCALE * _sigmoid(_dot(lo[:, :r1], w2_ref[...]) + w0_ref[...])
    a_all = _sigmoid(_dot(lo[:, r1:2 * r1], a2_ref[...]) + a0_ref[...])
    g_all = _dot(lo[:, 2 * r1:], g2_ref[...])

    sls = [slice(p * LANES, (p + 1) * LANES) for p in range(pairs)]
    tiles = [(r_ref[:, sl], lw_all[:, sl], k_ref[:, sl], v_ref[:, sl], a_all[:, sl], g_all[:, sl]) for sl in sls]
    params = [tuple(ref[:, sl] for ref in (kk_ref, ka_ref, rk_ref, lg_ref, lb_ref)) for sl in sls]
    outs, s_new = _wkv_chunk(tiles, params, [s_ref[p] for p in range(pairs)], consts)
    for p in range(pairs):
        s_ref[p] = s_new[p]
        o_ref[:, sls[p]] = outs[p].astype(o_ref.dtype)


def _wkv(rkv, lora, w2, a2, g2, w0, a0, kkp, kap, rkp, lgp, lbp, batch, r1, out_dtype=BF16):
    _, t, d = rkv.shape
    nl = lora.shape[1]
    lp = t // batch
    cc = WKV_CHUNK
    nchunk = lp // cc
    pairs = min(WKV_PAIRS_PER_STEP, d // LANES)
    wblk = pairs * LANES
    tok = pl.BlockSpec((cc, wblk), lambda b, p, c: (b * nchunk + c, p))
    r_spec, k_spec, v_spec = [pl.BlockSpec((None, cc, wblk), lambda b, p, c, n=n: (n, b * nchunk + c, p))
                              for n in range(3)]
    lo_spec = pl.BlockSpec((cc, nl), lambda b, p, c: (b * nchunk + c, 0))
    up = [pl.BlockSpec((w.shape[0], wblk), lambda b, p, c: (0, p)) for w in (w2, a2, g2)]
    par = pl.BlockSpec((1, wblk), lambda b, p, c: (0, p))
    prm = [x.reshape(1, d) for x in (w0, a0, kkp, kap, rkp, lgp, lbp)]
    return pl.pallas_call(
        functools.partial(_wkv_kernel, pairs=pairs, r1=r1),
        grid=(batch, d // wblk, nchunk),
        in_specs=[r_spec, k_spec, v_spec, lo_spec] + up + [par] * 7,
        out_specs=tok,
        out_shape=jax.ShapeDtypeStruct((t, d), out_dtype),
        scratch_shapes=[pltpu.VMEM((pairs, LANES, LANES), F32)],
        name="wkv7",
        compiler_params=_cparams(("parallel", "parallel", "arbitrary")),
    )(rkv, rkv, rkv, lora, w2, a2, g2, *prm)


def _fgate_kernel(h_ref, wh_ref, wl_ref, b_ref, o_ref, carry_ref):
    @pl.when(pl.program_id(1) == 0)
    def _():
        carry_ref[...] = jnp.zeros_like(carry_ref)

    hh, hl = _split(h_ref[...])
    z = _dot(hh, wh_ref[...]) + (_dot(hh, wl_ref[...]) + _dot(hl, wh_ref[...])) + b_ref[...]
    logf = -_softplus(-z) * LOG2E
    cs = _cumsum_rows(logf) + carry_ref[...]
    o_ref[...] = cs
    carry_ref[...] = cs[cs.shape[0] - 1:, :]


def _fgate_cumsum(h, wf, bf, batch, tm=544):
    t, d = h.shape
    lp = t // batch
    nt = lp // tm
    nh = wf.shape[1]
    wpad = jnp.zeros((d, LANES), F32).at[:, :nh].set(wf)
    bpad = jnp.zeros((1, LANES), F32).at[0, :nh].set(bf)
    wh, wl = _split(wpad)
    return pl.pallas_call(
        _fgate_kernel,
        grid=(batch, nt),
        in_specs=[pl.BlockSpec((tm, d), lambda b, i: (b * nt + i, 0)),
                  pl.BlockSpec((d, LANES), lambda b, i: (0, 0)),
                  pl.BlockSpec((d, LANES), lambda b, i: (0, 0)),
                  pl.BlockSpec((1, LANES), lambda b, i: (0, 0))],
        out_specs=pl.BlockSpec((tm, LANES), lambda b, i: (b * nt + i, 0)),
        out_shape=jax.ShapeDtypeStruct((t, LANES), F32),
        scratch_shapes=[pltpu.VMEM((1, LANES), F32)],
        name="fgate_cumsum",
        compiler_params=_cparams(("parallel", "arbitrary")),
    )(h, wh, wl, bpad)


def _fox_kernel(q_ref, k_ref, vt_ref, cc_ref, cr_ref, o_ref, *, heads, first, tile):
    lp = q_ref.shape[0]
    nbig = (lp - first) // tile
    hs = [slice(h * FX_HEAD, (h + 1) * FX_HEAD) for h in range(heads)]

    def scores(q, kstart, ksize):
        return [_dot_nt(k_ref[pl.ds(kstart, ksize), hs[h]], q[h]) for h in range(heads)]

    def update(s, cq, carry, kstart, ksize, diag):
        s = [s[h] + cq[h] - cc_ref[0, 0, pl.ds(kstart, ksize), h:h + 1] for h in range(heads)]
        if diag:
            keep = (lax.broadcasted_iota(jnp.int32, (ksize, ksize), 0)
                    <= lax.broadcasted_iota(jnp.int32, (ksize, ksize), 1))
            s = [jnp.where(keep, x, NEG_INF) for x in s]
        m_new = [jnp.maximum(carry[h][0], jnp.max(s[h], axis=0, keepdims=True)) for h in range(heads)]
        alpha = [jnp.exp2(carry[h][0] - m_new[h]) for h in range(heads)]
        p = [jnp.exp2(s[h] - m_new[h]) for h in range(heads)]
        l_new = [alpha[h] * carry[h][1] + jnp.sum(p[h], axis=0, keepdims=True) for h in range(heads)]
        pv = [_dot(vt_ref[hs[h], pl.ds(kstart, ksize)], p[h].astype(BF16)) for h in range(heads)]
        return [(m_new[h], l_new[h], alpha[h] * carry[h][2] + pv[h]) for h in range(heads)]

    def q_tile(qstart, tq, nfull, with_first):
        q = [q_ref[pl.ds(qstart, tq), hs[h]] for h in range(heads)]
        cq = [cr_ref[0, h, :, pl.ds(qstart, tq)] for h in range(heads)]
        carry = [(jnp.full((1, tq), NEG_INF, F32), jnp.zeros((1, tq), F32), jnp.zeros((FX_HEAD, tq), F32))
                 for _ in range(heads)]
        def step(c, kstart, ksize, diag):
            return update(scores(q, kstart, ksize), cq, c, kstart, ksize, diag)

        def step2(j, c):
            k0 = pl.multiple_of(first + 2 * j * tile, LANES)
            k1 = pl.multiple_of(first + (2 * j + 1) * tile, LANES)
            s0, s1 = scores(q, k0, tile), scores(q, k1, tile)
            return update(s1, cq, update(s0, cq, c, k0, tile, False), k1, tile, False)

        if nfull is not None:
            carry = lax.fori_loop(0, lax.shift_right_logical(nfull, 1), step2, carry)
            carry = lax.fori_loop(
                0, nfull & 1,
                lambda _, c: step(c, pl.multiple_of(first + (nfull - 1) * tile, LANES), tile, False), carry)
        s_first = scores(q, 0, first) if with_first else None
        s_diag = scores(q, qstart, tq)
        if with_first:
            carry = update(s_first, cq, carry, 0, first, False)
        carry = update(s_diag, cq, carry, qstart, tq, True)
        for h in range(heads):
            o_ref[pl.ds(qstart, tq), hs[h]] = (carry[h][2] / carry[h][1]).T.astype(o_ref.dtype)

    if first:
        q_tile(0, first, None, False)

    def big(i, _):
        q_tile(pl.multiple_of(first + i * tile, LANES), tile, i, first > 0)
        return 0

    lax.fori_loop(0, nbig, big, 0)


FOX_Q_SCALE = FX_HEAD ** -0.5 * LOG2E


def _fox_attention(q, k, vt, c, batch, heads=FOX_HEADS_PER_STEP, tile=256):
    t, d = q.shape
    lp = t // batch
    nh = d // FX_HEAD
    ng = nh // heads
    wblk = heads * FX_HEAD
    first = lp % tile
    c_col = jnp.transpose(c.reshape(batch, lp, ng, heads), (0, 2, 1, 3))
    c_row = jnp.transpose(c, (0, 2, 1))[:, :, None, :]
    return pl.pallas_call(
        functools.partial(_fox_kernel, heads=heads, first=first, tile=tile),
        grid=(batch, ng),
        in_specs=[pl.BlockSpec((lp, wblk), lambda b, g: (b, g)),
                  pl.BlockSpec((lp, wblk), lambda b, g: (b, g)),
                  pl.BlockSpec((wblk, lp), lambda b, g: (g, b)),
                  pl.BlockSpec((1, 1, lp, heads), lambda b, g: (b, g, 0, 0)),
                  pl.BlockSpec((1, heads, 1, lp), lambda b, g: (b, g, 0, 0))],
        out_specs=pl.BlockSpec((lp, wblk), lambda b, g: (b, g)),
        out_shape=jax.ShapeDtypeStruct((t, d), BF16),
        name="fox_attention",
        compiler_params=_cparams(("parallel", "parallel")),
    )(q, k, vt, c_col, c_row)


def _pad_cols(w, n):
    return jnp.zeros((w.shape[0], n), w.dtype).at[:, :w.shape[1]].set(w)


def _pad_rows(w, n):
    return jnp.zeros((n, w.shape[1]), w.dtype).at[:w.shape[0], :].set(w)


def _rwkv7_mix(h, batch, mu, w_rkv, w_o, w0, w1, w2, a0, a1, a2, g1, g2, k_k, k_a, r_k, lnx_g, lnx_b,
               ln_g, ln_b, alpha):
    r1 = LANES
    lw = jnp.concatenate([_pad_cols(w1, r1), _pad_cols(a1, r1), g1], axis=1).astype(BF16)
    rkv, lora = _rwkv_in(h, mu, w_rkv.astype(BF16), lw, batch, r1)
    o = _wkv(rkv, lora, _pad_rows(w2, r1).astype(BF16), _pad_rows(a2, r1).astype(BF16), g2.astype(BF16),
             w0, a0, k_k, k_a, r_k.reshape(-1), lnx_g, lnx_b, batch, r1)
    return _matmul_ln(o, w_o.astype(BF16), h, ln_g, ln_b, alpha)


def kernel(x, meta_tokens, ln_g, ln_b, ffn_w1, ffn_w3, ffn_w2, rw_mu, rw_w_rkv, rw_w_o, rw_w0, rw_w1,
           rw_w2, rw_a0, rw_a1, rw_a2, rw_g1, rw_g2, rw_k_k, rw_k_a, rw_r_k, rw_lnx_g, rw_lnx_b,
           fx_w_q, fx_w_o, fx_w_kvf, fx_b_f):
    batch, seq, d = x.shape
    depth = ln_g.shape[0]
    n_a = rw_mu.shape[0]
    alpha = (2 * depth) ** 0.25
    l_real = seq + N_META
    lp = -(-l_real // SEQ_ALIGN) * SEQ_ALIGN
    t = batch * lp
    nh = d // FX_HEAD

    wb = (ffn_w1[0, 0].astype(BF16), ffn_w3[0, 0].astype(BF16), ffn_w2[0, 0].astype(BF16))

    def ffn(h, wb, l, s):
        last = l == depth - 1 and s == 1
        nxt = None if last else (ffn_w1, ffn_w3, ffn_w2, l + s, 1 - s)
        return _ffn_ln(h, *wb, ln_g[l, 2 * s], ln_b[l, 2 * s], alpha, nxt=nxt)

    meta = jnp.broadcast_to(meta_tokens.astype(x.dtype)[None], (batch, N_META, d))
    h = jnp.concatenate([meta, x, jnp.zeros((batch, lp - l_real, d), x.dtype)], axis=1).reshape(t, d)
    k_s = vt_s = c = None
    for l in range(depth):
        h, wb = ffn(h, wb, l, 0)
        if l < n_a:
            h = _rwkv7_mix(h, batch, rw_mu[l], rw_w_rkv[l], rw_w_o[l], rw_w0[l], rw_w1[l], rw_w2[l],
                           rw_a0[l], rw_a1[l], rw_a2[l], rw_g1[l], rw_g2[l], rw_k_k[l], rw_k_a[l],
                           rw_r_k[l], rw_lnx_g[l], rw_lnx_b[l], ln_g[l, 1], ln_b[l, 1], alpha)
        else:
            j = l - n_a
            q = _proj(h, fx_w_q[j].astype(BF16), d, "q_proj", scale=FOX_Q_SCALE)
            o = _fox_attention(q, k_s, vt_s, c, batch)
            h = _matmul_ln(o, fx_w_o[j].astype(BF16), h, ln_g[l, 1], ln_b[l, 1], alpha)
        h, wb = ffn(h, wb, l, 1)
        if l == n_a - 1:
            w_kvf = fx_w_kvf[:, :2 * d].astype(BF16)
            k_s = _proj(h, w_kvf, d, "k_proj")
            vt_s = _proj(h, w_kvf, d, "vt_proj", woff=d, transpose_out=True, tm=512)
            c = _fgate_cumsum(h, fx_w_kvf[:, 2 * d:], fx_b_f, batch)[:, :nh].reshape(batch, lp, nh)
    return h.reshape(batch, lp, d)[:, N_META:l_real]
```

```python
import functools
import math

import jax
import jax.numpy as jnp
from jax import lax
from jax.experimental import pallas as pl
from jax.experimental.pallas import tpu as pltpu

N_META = 16
RW_HEAD = 64
FX_HEAD = 128
LN_EPS = 1e-5
RW_GN_EPS = RW_HEAD * 1e-5
NEG_INF = -1e30
LOG2E = math.log2(math.e)
DECAY_SCALE = math.exp(-0.5)

LANES = 128
V7X_VMEM_BYTES = 64 * 1024 * 1024
VMEM_LIMIT = V7X_VMEM_BYTES * 7 // 8
SEQ_ALIGN = 128
WKV_CHUNK = 64
WKV_PAIRS_PER_STEP = 16
FOX_HEADS_PER_STEP = 8

F32 = jnp.float32
BF16 = jnp.bfloat16


def _cparams(sem):
    return pltpu.CompilerParams(dimension_semantics=sem, vmem_limit_bytes=VMEM_LIMIT)


def _dot(a, b):
    return jnp.dot(a, b, preferred_element_type=F32)


def _dot_nt(a, b):
    return lax.dot_general(a, b, (((1,), (1,)), ((), ())), preferred_element_type=F32)


def _split(x):
    hi = x.astype(BF16)
    lo = (x - hi.astype(F32)).astype(BF16)
    return hi, lo


def _softplus(x):
    return jnp.maximum(x, 0.0) + jnp.log1p(jnp.exp(-jnp.abs(x)))


def _sigmoid(x):
    return 1.0 / (1.0 + jnp.exp(-x))


def _layer_norm(y, g, b):
    mu = jnp.mean(y, axis=-1, keepdims=True)
    yc = y - mu
    var = jnp.mean(yc * yc, axis=-1, keepdims=True)
    return yc * lax.rsqrt(var + LN_EPS) * g + b


def _proj_kernel(x_ref, w_ref, o_ref, xb_ref, *, scale, transpose_out):
    @pl.when(pl.program_id(1) == 0)
    def _():
        xb_ref[...] = x_ref[...].astype(BF16)

    acc = _dot(xb_ref[...], w_ref[...])
    if scale is not None:
        acc = acc * scale
    if transpose_out:
        acc = acc.T
    o_ref[...] = acc.astype(o_ref.dtype)


def _proj(x, w, n, name, woff=0, scale=None, transpose_out=False, tm=None, tn=1024):
    t, k = x.shape
    tm = tm or _pick_tm(t)
    jo = woff // tn
    if transpose_out:
        out_spec, out_shape = pl.BlockSpec((tn, tm), lambda i, j: (j, i)), (n, t)
    else:
        out_spec, out_shape = pl.BlockSpec((tm, tn), lambda i, j: (i, j)), (t, n)
    return pl.pallas_call(
        functools.partial(_proj_kernel, scale=scale, transpose_out=transpose_out),
        grid=(t // tm, n // tn),
        in_specs=[pl.BlockSpec((tm, k), lambda i, j: (i, 0)),
                  pl.BlockSpec((k, tn), lambda i, j: (0, j + jo))],
        out_specs=out_spec,
        out_shape=jax.ShapeDtypeStruct(out_shape, BF16),
        scratch_shapes=[pltpu.VMEM((tm, k), BF16)],
        name=name,
        compiler_params=_cparams(("parallel", "arbitrary")),
    )(x, w)


def _pick_tm(t, cap=1088):
    for tm in range(min(cap, t), 0, -16):
        if t % tm == 0:
            return tm
    return t


PREV_ROWS = 8


def _rwkv_in_kernel(h_ref, prev_ref, mu_ref, w_ref, lw_ref, rkv_ref, lora_ref, xs_ref, *, lp, nj, r1):
    i = pl.program_id(0)
    j = pl.program_id(1)
    tm = h_ref.shape[0]

    @pl.when(j == 0)
    def _():
        h = h_ref[...]
        last = prev_ref[PREV_ROWS - 1:, :]
        last = jnp.where(lax.rem(i * tm, lp) == 0, 0.0, last)
        row = lax.broadcasted_iota(jnp.int32, h.shape, 0)
        xx = jnp.where(row == 0, last, pltpu.roll(h, 1, 0)) - h

        def mix(n):
            return (h + xx * mu_ref[n:n + 1, :]).astype(BF16)

        xs_ref[0] = mix(0)
        xs_ref[1] = mix(2)
        xs_ref[2] = mix(3)
        tw = jnp.tanh(_dot(mix(1), lw_ref[:, :r1]))
        ta = _dot(mix(4), lw_ref[:, r1:2 * r1])
        tg = _sigmoid(_dot(mix(5), lw_ref[:, 2 * r1:]))
        lora_ref[...] = jnp.concatenate([tw, ta, tg], axis=1).astype(BF16)

    rkv_ref[...] = _dot(xs_ref[j // nj], w_ref[...]).astype(rkv_ref.dtype)


def _rwkv_in(h, mu, w_rkv, lw, batch, r1, tm=None, tn=512):
    t, d = h.shape
    lp = t // batch
    tm = tm or _pick_tm(lp, 1088)
    nj = d // tn
    nl = lw.shape[1]
    pblk = tm // PREV_ROWS
    return pl.pallas_call(
        functools.partial(_rwkv_in_kernel, lp=lp, nj=nj, r1=r1),
        grid=(t // tm, 3 * nj),
        in_specs=[pl.BlockSpec((tm, d), lambda i, j: (i, 0)),
                  pl.BlockSpec((PREV_ROWS, d), lambda i, j: (jnp.maximum(i * pblk - 1, 0), 0)),
                  pl.BlockSpec((6, d), lambda i, j: (0, 0)),
                  pl.BlockSpec((None, d, tn), lambda i, j: (j // nj, 0, j % nj)),
                  pl.BlockSpec((d, nl), lambda i, j: (0, 0))],
        out_specs=[pl.BlockSpec((None, tm, tn), lambda i, j: (j // nj, i, j % nj)),
                   pl.BlockSpec((tm, nl), lambda i, j: (i, 0))],
        out_shape=[jax.ShapeDtypeStruct((3, t, d), BF16), jax.ShapeDtypeStruct((t, nl), BF16)],
        scratch_shapes=[pltpu.VMEM((3, tm, d), BF16)],
        name="rwkv_in",
        compiler_params=_cparams(("parallel", "arbitrary")),
    )(h, h, mu, w_rkv, lw)


def _mm_ln_kernel(x_ref, w_ref, h_ref, g_ref, b_ref, o_ref, *, alpha, splits):
    rs = x_ref.shape[0] // splits
    rows = [slice(n * rs, (n + 1) * rs) for n in range(splits)]
    ys = [alpha * h_ref[r, :] + _dot(x_ref[r, :], w_ref[...]) for r in rows]
    for r, y in zip(rows, ys):
        o_ref[r, :] = _layer_norm(y, g_ref[...], b_ref[...])


def _matmul_ln(x, w, h, g, b, alpha, tm=None):
    t, k = x.shape
    n = w.shape[1]
    tm = tm or _pick_tm(t, 512)
    return pl.pallas_call(
        functools.partial(_mm_ln_kernel, alpha=alpha, splits=4 if tm % 64 == 0 else 1),
        grid=(t // tm,),
        in_specs=[pl.BlockSpec((tm, k), lambda i: (i, 0)),
                  pl.BlockSpec((k, n), lambda i: (0, 0)),
                  pl.BlockSpec((tm, n), lambda i: (i, 0)),
                  pl.BlockSpec((1, n), lambda i: (0, 0)),
                  pl.BlockSpec((1, n), lambda i: (0, 0))],
        out_specs=pl.BlockSpec((tm, n), lambda i: (i, 0)),
        out_shape=jax.ShapeDtypeStruct((t, n), F32),
        name="proj_ln",
        compiler_params=_cparams(("parallel",)),
    )(x, w, h, g.reshape(1, n), b.reshape(1, n))


def _ffn_ln_kernel(h_ref, w1_ref, w3_ref, w2_ref, g_ref, b_ref, *rest, alpha):
    has_next = len(rest) > 2
    nxt_refs, cast_refs = (rest[:3], rest[4:7]) if has_next else ((), ())
    o_ref, hb_ref = rest[3 if has_next else 0], rest[-1]
    j = pl.program_id(1)
    tm = hb_ref.shape[0]

    @pl.when(j == 0)
    def _():
        hb_ref[...] = h_ref[...].astype(BF16)
        o_ref[...] = jnp.zeros_like(o_ref)

    def cast_next_weights():
        for src, dst in zip(nxt_refs, cast_refs):
            dst[...] = src[...].astype(BF16)

    def partial_out(rows):
        hb = hb_ref[rows, :]
        u = _dot(hb, w1_ref[...])
        v = _dot(hb, w3_ref[...])
        return _dot((u * _sigmoid(u) * v).astype(BF16), w2_ref[...])

    last = pl.num_programs(1) - 1

    @pl.when(j < last)
    def _():
        cast_next_weights()
        o_ref[...] += partial_out(slice(0, tm))

    @pl.when(j == last)
    def _():
        cast_next_weights()
        for rows in (slice(0, tm // 2), slice(tm // 2, tm)):
            y = alpha * h_ref[rows, :] + 0.5 * (o_ref[rows, :] + partial_out(rows))
            o_ref[rows, :] = _layer_norm(y, g_ref[...], b_ref[...])


def _ffn_ln(h, w1, w3, w2, g, b, alpha, nxt=None, tm=None, tf=512):
    t, d = h.shape
    f = w1.shape[-1]
    tm = tm or _pick_tm(t, 544)
    ni, nj = t // tm, f // tf
    in_specs = [pl.BlockSpec((tm, d), lambda i, j: (i, 0)),
                pl.BlockSpec((d, tf), lambda i, j: (0, j)),
                pl.BlockSpec((d, tf), lambda i, j: (0, j)),
                pl.BlockSpec((tf, d), lambda i, j: (j, 0)),
                pl.BlockSpec((1, d), lambda i, j: (0, 0)),
                pl.BlockSpec((1, d), lambda i, j: (0, 0))]
    out_specs = [pl.BlockSpec((tm, d), lambda i, j: (i, 0))]
    out_shape = [jax.ShapeDtypeStruct((t, d), F32)]
    args = [h, w1, w3, w2, g.reshape(1, d), b.reshape(1, d)]
    if nxt is not None:
        n1, n3, n2, l, s = nxt
        dr = d // ni
        assert dr * ni == d and dr % LANES == 0
        in_specs += [pl.BlockSpec((None, None, dr, tf), lambda i, j: (l, s, i, j)),
                     pl.BlockSpec((None, None, dr, tf), lambda i, j: (l, s, i, j)),
                     pl.BlockSpec((None, None, tf, dr), lambda i, j: (l, s, j, i))]
        out_specs += [pl.BlockSpec((dr, tf), lambda i, j: (i, j)),
                      pl.BlockSpec((dr, tf), lambda i, j: (i, j)),
                      pl.BlockSpec((tf, dr), lambda i, j: (j, i))]
        out_shape += [jax.ShapeDtypeStruct((d, f), BF16), jax.ShapeDtypeStruct((d, f), BF16),
                      jax.ShapeDtypeStruct((f, d), BF16)]
        args += [n1, n3, n2]
    out = pl.pallas_call(
        functools.partial(_ffn_ln_kernel, alpha=alpha),
        grid=(ni, nj),
        in_specs=in_specs,
        out_specs=out_specs,
        out_shape=out_shape,
        scratch_shapes=[pltpu.VMEM((tm, d), BF16)],
        name="ffn_ln",
        compiler_params=_cparams(("parallel", "arbitrary")),
    )(*args)
    return out[0], tuple(out[1:])


def _cumsum_rows(x):
    n = x.shape[0]
    row = lax.broadcasted_iota(jnp.int32, x.shape, 0)
    s = 1
    while s < n:
        x = x + jnp.where(row >= s, pltpu.roll(x, s, 0), 0.0)
        s *= 2
    return x


def _mm1(a, b):
    return _dot(a.astype(BF16), b.astype(BF16))


def _wkv_chunk(tiles, params, states, c):
    m0, strict, incl, ones_bd = c
    cc = tiles[0][0].shape[0]
    c2 = 2 * cc
    n = len(tiles)
    inv_n = 1.0 / RW_HEAD

    def stack2(x):
        return jnp.concatenate([jnp.where(m0, x, 0.0), jnp.where(m0, 0.0, x)], axis=0)

    def segsum(xs):
        return [_dot(x.astype(BF16), ones_bd) for x in xs]

    kk0 = [t[2] * p[0] for t, p in zip(tiles, params)]
    nrm2 = segsum([x * x for x in kk0])
    kmod = [t[2] * (1.0 + (t[4] - 1.0) * p[1]) for t, p in zip(tiles, params)]
    bon = segsum([t[0] * km * p[2] for t, km, p in zip(tiles, kmod, params)])

    lhs, rhs, vs, upd_r, dec = [], [], [], [], []
    for i in range(n):
        r, lw, k, v, a, g = tiles[i]
        kk = kk0[i] / jnp.maximum(jnp.sqrt(nrm2[i]), 1e-12)
        b = kk * a
        lcum = _cumsum_rows(lw)
        ltot = lcum[cc - 1:cc, :]
        e_neg = jnp.exp(-lcum)
        e_rem = jnp.exp(ltot - lcum)
        kt = kk * jnp.exp(lcum - lw)
        rt = r * jnp.exp(lcum)
        khb = (kmod[i] * e_neg).astype(BF16)
        bhb = (b * e_neg).astype(BF16)
        lhs.append(jnp.concatenate([stack2(kt), stack2(rt)], axis=0).astype(BF16))
        rhs.append(jnp.concatenate([khb, khb, bhb, bhb, states[i].astype(BF16)], axis=0))
        vs.append(stack2(v))
        upd_r.append(jnp.concatenate([stack2(kmod[i] * e_rem), stack2(b * e_rem)], axis=0).astype(BF16))
        dec.append(jnp.exp(ltot))

    m = [_dot_nt(a_, b_) for a_, b_ in zip(lhs, rhs)]
    a_k = [jnp.where(strict, x[:c2, :c2], 0.0) for x in m]
    a_b = [jnp.where(strict, x[:c2, c2:2 * c2], 0.0) for x in m]
    bkb = [jnp.concatenate([jnp.where(incl, x[c2:, :c2], 0.0), jnp.where(incl, -x[c2:, c2:2 * c2], 0.0)],
                           axis=1).astype(BF16) for x in m]
    rhs_u = [x[:c2, 2 * c2:] + _mm1(ak, v_) for x, ak, v_ in zip(m, a_k, vs)]
    pw = [ab.astype(BF16) for ab in a_b]
    st = [_dot(p_, jnp.concatenate([p_, ru.astype(BF16)], axis=1)) for p_, ru in zip(pw, rhs_u)]
    pw = [s_[:, :c2].astype(BF16) for s_ in st]
    us = [ru - s_[:, c2:] for ru, s_ in zip(rhs_u, st)]
    lvl = 4
    while lvl < cc:
        st = [_dot(p_, jnp.concatenate([p_, u_.astype(BF16)], axis=1)) for p_, u_ in zip(pw, us)]
        pw = [s_[:, :c2].astype(BF16) for s_ in st]
        us = [u_ + s_[:, c2:] for u_, s_ in zip(us, st)]
        lvl *= 2
    us = [u_ + _dot(p_, u_.astype(BF16)) for u_, p_ in zip(us, pw)]
    ys = [x[c2:, 2 * c2:] + _dot(bk, jnp.concatenate([v_, u_], axis=0).astype(BF16))
          for x, bk, v_, u_ in zip(m, bkb, vs, us)]
    y = [x[:cc] + x[cc:] for x in ys]
    upd_l = [jnp.concatenate([v_, -u_], axis=0).T.astype(BF16) for v_, u_ in zip(vs, us)]
    s_new = [s0 * d_ + _dot(ul, ur) for s0, d_, ul, ur in zip(states, dec, upd_l, upd_r)]

    mu = [x * inv_n for x in segsum(y)]
    yc = [a_ - b_ for a_, b_ in zip(y, mu)]
    var = [x * inv_n for x in segsum([x * x for x in yc])]
    outs = []
    for i in range(n):
        yn = yc[i] * lax.rsqrt(var[i] + RW_GN_EPS) * params[i][3] + params[i][4]
        outs.append((yn + bon[i] * tiles[i][3]) * tiles[i][5])
    return outs, s_new


def _wkv_kernel(r_ref, k_ref, v_ref, lora_ref, w2_ref, a2_ref, g2_ref, w0_ref, a0_ref,
                kk_ref, ka_ref, rk_ref, lg_ref, lb_ref, o_ref, s_ref, *, pairs, r1):
    @pl.when(pl.program_id(2) == 0)
    def _():
        s_ref[...] = jnp.zeros_like(s_ref)

    lo = lora_ref[...]

    cc = r_ref.shape[0]
    c2 = 2 * cc
    lane = lax.broadcasted_iota(jnp.int32, (cc, LANES), 1)
    m0 = lane < RW_HEAD
    row = lax.broadcasted_iota(jnp.int32, (c2, c2), 0)
    col = lax.broadcasted_iota(jnp.int32, (c2, c2), 1)
    same = (row >= cc) == (col >= cc)
    strict = same & (col < row)
    incl = same & (col <= row)
    lr = lax.broadcasted_iota(jnp.int32, (LANES, LANES), 0)
    lc = lax.broadcasted_iota(jnp.int32, (LANES, LANES), 1)
    ones_bd = jnp.where((lr >= RW_HEAD) == (lc >= RW_HEAD), 1.0, 0.0).astype(BF16)
    consts = (m0, strict, incl, ones_bd)

    lw_all = -DECAY_SCALE * _sigmoid(_dot(lo[:, :r1], w2_ref[...]) + w0_ref[...])
    a_all = _sigmoid(_dot(lo[:, r1:2 * r1], a2_ref[...]) + a0_ref[...])
    g_all = _dot(lo[:, 2 * r1:], g2_ref[...])

    sls = [slice(p * LANES, (p + 1) * LANES) for p in range(pairs)]
    tiles = [(r_ref[:, sl].astype(F32), lw_all[:, sl], k_ref[:, sl].astype(F32), v_ref[:, sl].astype(F32),
              a_all[:, sl], g_all[:, sl]) for sl in sls]
    params = [tuple(ref[:, sl] for ref in (kk_ref, ka_ref, rk_ref, lg_ref, lb_ref)) for sl in sls]
    outs, s_new = _wkv_chunk(tiles, params, [s_ref[p] for p in range(pairs)], consts)
    for p in range(pairs):
        s_ref[p] = s_new[p]
        o_ref[:, sls[p]] = outs[p].astype(o_ref.dtype)


def _wkv(rkv, lora, w2, a2, g2, w0, a0, kkp, kap, rkp, lgp, lbp, batch, r1, out_dtype=BF16):
    _, t, d = rkv.shape
    nl = lora.shape[1]
    lp = t // batch
    cc = WKV_CHUNK
    nchunk = lp // cc
    pairs = min(WKV_PAIRS_PER_STEP, d // LANES)
    wblk = pairs * LANES
    tok = pl.BlockSpec((cc, wblk), lambda b, p, c: (b * nchunk + c, p))
    r_spec, k_spec, v_spec = [pl.BlockSpec((None, cc, wblk), lambda b, p, c, n=n: (n, b * nchunk + c, p))
                              for n in range(3)]
    lo_spec = pl.BlockSpec((cc, nl), lambda b, p, c: (b * nchunk + c, 0))
    up = [pl.BlockSpec((w.shape[0], wblk), lambda b, p, c: (0, p)) for w in (w2, a2, g2)]
    par = pl.BlockSpec((1, wblk), lambda b, p, c: (0, p))
    prm = [x.reshape(1, d) for x in (w0, a0, kkp, kap, rkp, lgp, lbp)]
    return pl.pallas_call(
        functools.partial(_wkv_kernel, pairs=pairs, r1=r1),
        grid=(batch, d // wblk, nchunk),
        in_specs=[r_spec, k_spec, v_spec, lo_spec] + up + [par] * 7,
        out_specs=tok,
        out_shape=jax.ShapeDtypeStruct((t, d), out_dtype),
        scratch_shapes=[pltpu.VMEM((pairs, LANES, LANES), F32)],
        name="wkv7",
        compiler_params=_cparams(("parallel", "parallel", "arbitrary")),
    )(rkv, rkv, rkv, lora, w2, a2, g2, *prm)


def _fgate_kernel(h_ref, wh_ref, wl_ref, b_ref, o_ref, carry_ref):
    @pl.when(pl.program_id(1) == 0)
    def _():
        carry_ref[...] = jnp.zeros_like(carry_ref)

    hh, hl = _split(h_ref[...])
    z = _dot(hh, wh_ref[...]) + (_dot(hh, wl_ref[...]) + _dot(hl, wh_ref[...])) + b_ref[...]
    logf = -_softplus(-z) * LOG2E
    cs = _cumsum_rows(logf) + carry_ref[...]
    o_ref[...] = cs
    carry_ref[...] = cs[cs.shape[0] - 1:, :]


def _fgate_cumsum(h, wf, bf, batch, tm=544):
    t, d = h.shape
    lp = t // batch
    nt = lp // tm
    nh = wf.shape[1]
    wpad = jnp.zeros((d, LANES), F32).at[:, :nh].set(wf)
    bpad = jnp.zeros((1, LANES), F32).at[0, :nh].set(bf)
    wh, wl = _split(wpad)
    return pl.pallas_call(
        _fgate_kernel,
        grid=(batch, nt),
        in_specs=[pl.BlockSpec((tm, d), lambda b, i: (b * nt + i, 0)),
                  pl.BlockSpec((d, LANES), lambda b, i: (0, 0)),
                  pl.BlockSpec((d, LANES), lambda b, i: (0, 0)),
                  pl.BlockSpec((1, LANES), lambda b, i: (0, 0))],
        out_specs=pl.BlockSpec((tm, LANES), lambda b, i: (b * nt + i, 0)),
        out_shape=jax.ShapeDtypeStruct((t, LANES), F32),
        scratch_shapes=[pltpu.VMEM((1, LANES), F32)],
        name="fgate_cumsum",
        compiler_params=_cparams(("parallel", "arbitrary")),
    )(h, wh, wl, bpad)


def _fox_kernel(q_ref, k_ref, vt_ref, cc_ref, cr_ref, o_ref, *, heads, first, tile):
    lp = q_ref.shape[0]
    nbig = (lp - first) // tile
    hs = [slice(h * FX_HEAD, (h + 1) * FX_HEAD) for h in range(heads)]

    def scores(q, kstart, ksize):
        return [_dot_nt(k_ref[pl.ds(kstart, ksize), hs[h]], q[h]) for h in range(heads)]

    def update(s, cq, carry, kstart, ksize, diag):
        s = [s[h] + cq[h] - cc_ref[0, 0, pl.ds(kstart, ksize), h:h + 1] for h in range(heads)]
        if diag:
            keep = (lax.broadcasted_iota(jnp.int32, (ksize, ksize), 0)
                    <= lax.broadcasted_iota(jnp.int32, (ksize, ksize), 1))
            s = [jnp.where(keep, x, NEG_INF) for x in s]
        m_new = [jnp.maximum(carry[h][0], jnp.max(s[h], axis=0, keepdims=True)) for h in range(heads)]
        alpha = [jnp.exp2(carry[h][0] - m_new[h]) for h in range(heads)]
        p = [jnp.exp2(s[h] - m_new[h]) for h in range(heads)]
        l_new = [alpha[h] * carry[h][1] + jnp.sum(p[h], axis=0, keepdims=True) for h in range(heads)]
        pv = [_dot(vt_ref[hs[h], pl.ds(kstart, ksize)], p[h].astype(BF16)) for h in range(heads)]
        return [(m_new[h], l_new[h], alpha[h] * carry[h][2] + pv[h]) for h in range(heads)]

    def q_tile(qstart, tq, nfull, with_first):
        q = [q_ref[pl.ds(qstart, tq), hs[h]] for h in range(heads)]
        cq = [cr_ref[0, h, :, pl.ds(qstart, tq)] for h in range(heads)]
        carry = [(jnp.full((1, tq), NEG_INF, F32), jnp.zeros((1, tq), F32), jnp.zeros((FX_HEAD, tq), F32))
                 for _ in range(heads)]
        def step(c, kstart, ksize, diag):
            return update(scores(q, kstart, ksize), cq, c, kstart, ksize, diag)

        def step2(j, c):
            k0 = pl.multiple_of(first + 2 * j * tile, LANES)
            k1 = pl.multiple_of(first + (2 * j + 1) * tile, LANES)
            s0, s1 = scores(q, k0, tile), scores(q, k1, tile)
            return update(s1, cq, update(s0, cq, c, k0, tile, False), k1, tile, False)

        if nfull is not None:
            carry = lax.fori_loop(0, lax.shift_right_logical(nfull, 1), step2, carry)
            carry = lax.fori_loop(
                0, nfull & 1,
                lambda _, c: step(c, pl.multiple_of(first + (nfull - 1) * tile, LANES), tile, False), carry)
        s_first = scores(q, 0, first) if with_first else None
        s_diag = scores(q, qstart, tq)
        if with_first:
            carry = update(s_first, cq, carry, 0, first, False)
        carry = update(s_diag, cq, carry, qstart, tq, True)
        for h in range(heads):
            o_ref[pl.ds(qstart, tq), hs[h]] = (carry[h][2] / carry[h][1]).T.astype(o_ref.dtype)

    if first:
        q_tile(0, first, None, False)

    def big(i, _):
        q_tile(pl.multiple_of(first + i * tile, LANES), tile, i, first > 0)
        return 0

    lax.fori_loop(0, nbig, big, 0)


FOX_Q_SCALE = FX_HEAD ** -0.5 * LOG2E


def _fox_attention(q, k, vt, c, batch, heads=FOX_HEADS_PER_STEP, tile=256):
    t, d = q.shape
    lp = t // batch
    nh = d // FX_HEAD
    ng = nh // heads
    wblk = heads * FX_HEAD
    first = lp % tile
    c_col = jnp.transpose(c.reshape(batch, lp, ng, heads), (0, 2, 1, 3))
    c_row = jnp.transpose(c, (0, 2, 1))[:, :, None, :]
    return pl.pallas_call(
        functools.partial(_fox_kernel, heads=heads, first=first, tile=tile),
        grid=(batch, ng),
        in_specs=[pl.BlockSpec((lp, wblk), lambda b, g: (b, g)),
                  pl.BlockSpec((lp, wblk), lambda b, g: (b, g)),
                  pl.BlockSpec((wblk, lp), lambda b, g: (g, b)),
                  pl.BlockSpec((1, 1, lp, heads), lambda b, g: (b, g, 0, 0)),
                  pl.BlockSpec((1, heads, 1, lp), lambda b, g: (b, g, 0, 0))],
        out_specs=pl.BlockSpec((lp, wblk), lambda b, g: (b, g)),
        out_shape=jax.ShapeDtypeStruct((t, d), BF16),
        name="fox_attention",
        compiler_params=_cparams(("parallel", "parallel")),
    )(q, k, vt, c_col, c_row)


def _pad_cols(w, n):
    return jnp.zeros((w.shape[0], n), w.dtype).at[:, :w.shape[1]].set(w)


def _pad_rows(w, n):
    return jnp.zeros((n, w.shape[1]), w.dtype).at[:w.shape[0], :].set(w)


def _rwkv7_mix(h, batch, mu, w_rkv, w_o, w0, w1, w2, a0, a1, a2, g1, g2, k_k, k_a, r_k, lnx_g, lnx_b,
               ln_g, ln_b, alpha):
    r1 = LANES
    lw = jnp.concatenate([_pad_cols(w1, r1), _pad_cols(a1, r1), g1], axis=1).astype(BF16)
    rkv, lora = _rwkv_in(h, mu, w_rkv.astype(BF16), lw, batch, r1)
    o = _wkv(rkv, lora, _pad_rows(w2, r1).astype(BF16), _pad_rows(a2, r1).astype(BF16), g2.astype(BF16),
             w0, a0, k_k, k_a, r_k.reshape(-1), lnx_g, lnx_b, batch, r1)
    return _matmul_ln(o, w_o.astype(BF16), h, ln_g, ln_b, alpha)


def kernel(x, meta_tokens, ln_g, ln_b, ffn_w1, ffn_w3, ffn_w2, rw_mu, rw_w_rkv, rw_w_o, rw_w0, rw_w1,
           rw_w2, rw_a0, rw_a1, rw_a2, rw_g1, rw_g2, rw_k_k, rw_k_a, rw_r_k, rw_lnx_g, rw_lnx_b,
           fx_w_q, fx_w_o, fx_w_kvf, fx_b_f):
    batch, seq, d = x.shape
    depth = ln_g.shape[0]
    n_a = rw_mu.shape[0]
    alpha = (2 * depth) ** 0.25
    l_real = seq + N_META
    lp = -(-l_real // SEQ_ALIGN) * SEQ_ALIGN
    t = batch * lp
    nh = d // FX_HEAD

    wb = (ffn_w1[0, 0].astype(BF16), ffn_w3[0, 0].astype(BF16), ffn_w2[0, 0].astype(BF16))

    def ffn(h, wb, l, s):
        last = l == depth - 1 and s == 1
        nxt = None if last else (ffn_w1, ffn_w3, ffn_w2, l + s, 1 - s)
        return _ffn_ln(h, *wb, ln_g[l, 2 * s], ln_b[l, 2 * s], alpha, nxt=nxt)

    meta = jnp.broadcast_to(meta_tokens.astype(x.dtype)[None], (batch, N_META, d))
    h = jnp.concatenate([meta, x, jnp.zeros((batch, lp - l_real, d), x.dtype)], axis=1).reshape(t, d)
    k_s = vt_s = c = None
    for l in range(depth):
        h, wb = ffn(h, wb, l, 0)
        if l < n_a:
            h = _rwkv7_mix(h, batch, rw_mu[l], rw_w_rkv[l], rw_w_o[l], rw_w0[l], rw_w1[l], rw_w2[l],
                           rw_a0[l], rw_a1[l], rw_a2[l], rw_g1[l], rw_g2[l], rw_k_k[l], rw_k_a[l],
                           rw_r_k[l], rw_lnx_g[l], rw_lnx_b[l], ln_g[l, 1], ln_b[l, 1], alpha)
        else:
            j = l - n_a
            q = _proj(h, fx_w_q[j].astype(BF16), d, "q_proj", scale=FOX_Q_SCALE)
            o = _fox_attention(q, k_s, vt_s, c, batch)
            h = _matmul_ln(o, fx_w_o[j].astype(BF16), h, ln_g[l, 1], ln_b[l, 1], alpha)
        h, wb = ffn(h, wb, l, 1)
        if l == n_a - 1:
            w_kvf = fx_w_kvf.astype(BF16)
            k_s = _proj(h, w_kvf, d, "k_proj")
            vt_s = _proj(h, w_kvf, d, "vt_proj", woff=d, transpose_out=True, tm=512)
            c = _fgate_cumsum(h, fx_w_kvf[:, 2 * d:], fx_b_f, batch)[:, :nh].reshape(batch, lp, nh)
    return h.reshape(batch, lp, d)[:, N_META:l_real]
```

```python
import functools
import math

import jax
import jax.numpy as jnp
from jax import lax
from jax.experimental import pallas as pl
from jax.experimental.pallas import tpu as pltpu

N_META = 16
RW_HEAD = 64
FX_HEAD = 128
LN_EPS = 1e-5
RW_GN_EPS = RW_HEAD * 1e-5
NEG_INF = -1e30
LOG2E = math.log2(math.e)
DECAY_SCALE = math.exp(-0.5)

LANES = 128
V7X_VMEM_BYTES = 64 * 1024 * 1024
VMEM_LIMIT = V7X_VMEM_BYTES * 7 // 8
SEQ_ALIGN = 128
WKV_CHUNK = 64
WKV_PAIRS_PER_STEP = 16
FOX_HEADS_PER_STEP = 8

F32 = jnp.float32
BF16 = jnp.bfloat16


def _cparams(sem):
    return pltpu.CompilerParams(dimension_semantics=sem, vmem_limit_bytes=VMEM_LIMIT)


def _dot(a, b):
    return jnp.dot(a, b, preferred_element_type=F32)


def _dot_nt(a, b):
    return lax.dot_general(a, b, (((1,), (1,)), ((), ())), preferred_element_type=F32)


def _split(x):
    hi = x.astype(BF16)
    lo = (x - hi.astype(F32)).astype(BF16)
    return hi, lo


def _softplus(x):
    return jnp.maximum(x, 0.0) + jnp.log1p(jnp.exp(-jnp.abs(x)))


def _sigmoid(x):
    return 1.0 / (1.0 + jnp.exp(-x))


def _layer_norm(y, g, b):
    mu = jnp.mean(y, axis=-1, keepdims=True)
    yc = y - mu
    var = jnp.mean(yc * yc, axis=-1, keepdims=True)
    return yc * lax.rsqrt(var + LN_EPS) * g + b


def _proj_kernel(x_ref, w_ref, o_ref, xb_ref, *, scale, transpose_out):
    @pl.when(pl.program_id(1) == 0)
    def _():
        xb_ref[...] = x_ref[...].astype(BF16)

    acc = _dot(xb_ref[...], w_ref[...])
    if scale is not None:
        acc = acc * scale
    if transpose_out:
        acc = acc.T
    o_ref[...] = acc.astype(o_ref.dtype)


def _proj(x, w, n, name, woff=0, scale=None, transpose_out=False, tm=None, tn=1024):
    t, k = x.shape
    tm = tm or _pick_tm(t)
    jo = woff // tn
    if transpose_out:
        out_spec, out_shape = pl.BlockSpec((tn, tm), lambda i, j: (j, i)), (n, t)
    else:
        out_spec, out_shape = pl.BlockSpec((tm, tn), lambda i, j: (i, j)), (t, n)
    return pl.pallas_call(
        functools.partial(_proj_kernel, scale=scale, transpose_out=transpose_out),
        grid=(t // tm, n // tn),
        in_specs=[pl.BlockSpec((tm, k), lambda i, j: (i, 0)),
                  pl.BlockSpec((k, tn), lambda i, j: (0, j + jo))],
        out_specs=out_spec,
        out_shape=jax.ShapeDtypeStruct(out_shape, BF16),
        scratch_shapes=[pltpu.VMEM((tm, k), BF16)],
        name=name,
        compiler_params=_cparams(("parallel", "arbitrary")),
    )(x, w)


def _pick_tm(t, cap=1088):
    for tm in range(min(cap, t), 0, -16):
        if t % tm == 0:
            return tm
    return t


PREV_ROWS = 8


def _rwkv_in_kernel(h_ref, prev_ref, mu_ref, w_ref, lw_ref, rkv_ref, lora_ref, xs_ref, *, lp, nj, r1):
    i = pl.program_id(0)
    j = pl.program_id(1)
    tm = h_ref.shape[0]

    @pl.when(j == 0)
    def _():
        h = h_ref[...]
        last = prev_ref[PREV_ROWS - 1:, :]
        last = jnp.where(lax.rem(i * tm, lp) == 0, 0.0, last)
        row = lax.broadcasted_iota(jnp.int32, h.shape, 0)
        xx = jnp.where(row == 0, last, pltpu.roll(h, 1, 0)) - h

        def mix(n):
            return (h + xx * mu_ref[n:n + 1, :]).astype(BF16)

        xs_ref[0] = mix(0)
        xs_ref[1] = mix(2)
        xs_ref[2] = mix(3)
        tw = jnp.tanh(_dot(mix(1), lw_ref[:, :r1]))
        ta = _dot(mix(4), lw_ref[:, r1:2 * r1])
        tg = _sigmoid(_dot(mix(5), lw_ref[:, 2 * r1:]))
        lora_ref[...] = jnp.concatenate([tw, ta, tg], axis=1).astype(BF16)

    rkv_ref[...] = _dot(xs_ref[j // nj], w_ref[...])


def _rwkv_in(h, mu, w_rkv, lw, batch, r1, tm=None, tn=512):
    t, d = h.shape
    lp = t // batch
    tm = tm or _pick_tm(lp, 1088)
    nj = d // tn
    nl = lw.shape[1]
    pblk = tm // PREV_ROWS
    return pl.pallas_call(
        functools.partial(_rwkv_in_kernel, lp=lp, nj=nj, r1=r1),
        grid=(t // tm, 3 * nj),
        in_specs=[pl.BlockSpec((tm, d), lambda i, j: (i, 0)),
                  pl.BlockSpec((PREV_ROWS, d), lambda i, j: (jnp.maximum(i * pblk - 1, 0), 0)),
                  pl.BlockSpec((6, d), lambda i, j: (0, 0)),
                  pl.BlockSpec((None, d, tn), lambda i, j: (j // nj, 0, j % nj)),
                  pl.BlockSpec((d, nl), lambda i, j: (0, 0))],
        out_specs=[pl.BlockSpec((None, tm, tn), lambda i, j: (j // nj, i, j % nj)),
                   pl.BlockSpec((tm, nl), lambda i, j: (i, 0))],
        out_shape=[jax.ShapeDtypeStruct((3, t, d), F32), jax.ShapeDtypeStruct((t, nl), BF16)],
        scratch_shapes=[pltpu.VMEM((3, tm, d), BF16)],
        name="rwkv_in",
        compiler_params=_cparams(("parallel", "arbitrary")),
    )(h, h, mu, w_rkv, lw)


def _mm_ln_kernel(x_ref, w_ref, h_ref, g_ref, b_ref, o_ref, *, alpha, splits):
    rs = x_ref.shape[0] // splits
    rows = [slice(n * rs, (n + 1) * rs) for n in range(splits)]
    ys = [alpha * h_ref[r, :] + _dot(x_ref[r, :], w_ref[...]) for r in rows]
    for r, y in zip(rows, ys):
        o_ref[r, :] = _layer_norm(y, g_ref[...], b_ref[...])


def _matmul_ln(x, w, h, g, b, alpha, tm=None):
    t, k = x.shape
    n = w.shape[1]
    tm = tm or _pick_tm(t, 512)
    return pl.pallas_call(
        functools.partial(_mm_ln_kernel, alpha=alpha, splits=4 if tm % 64 == 0 else 1),
        grid=(t // tm,),
        in_specs=[pl.BlockSpec((tm, k), lambda i: (i, 0)),
                  pl.BlockSpec((k, n), lambda i: (0, 0)),
                  pl.BlockSpec((tm, n), lambda i: (i, 0)),
                  pl.BlockSpec((1, n), lambda i: (0, 0)),
                  pl.BlockSpec((1, n), lambda i: (0, 0))],
        out_specs=pl.BlockSpec((tm, n), lambda i: (i, 0)),
        out_shape=jax.ShapeDtypeStruct((t, n), F32),
        name="proj_ln",
        compiler_params=_cparams(("parallel",)),
    )(x, w, h, g.reshape(1, n), b.reshape(1, n))


def _ffn_ln_kernel(*refs, alpha, has_next, head):
    refs = list(refs)
    h_ref = refs.pop(0)
    meta_ref = refs.pop(0) if head else None
    w1_ref, w3_ref, w2_ref, g_ref, b_ref = refs[:5]
    nxt_refs = refs[5:8] if has_next else ()
    o_ref = refs[8 if has_next else 5]
    cast_refs = refs[9:12] if has_next else ()
    hb_ref = refs[12 if has_next else 6]
    hf_ref, sem = refs[-2:] if head else (None, None)
    i = pl.program_id(0)
    j = pl.program_id(1)
    tm = hb_ref.shape[0]
    slot = lax.rem(i, 2)

    def tile_rows(c):
        n_meta, seq, _ = head
        lo, hi = max(c * tm - n_meta, 0), min((c + 1) * tm - n_meta, seq)
        return lo, lo + n_meta - c * tm, hi - lo

    def tile_dma(tile, buf, act):
        tiles = head[2]
        for c in range(tiles):
            lo, dst, n = tile_rows(c)

            @pl.when(lax.rem(tile, tiles) == c)
            def _():
                cp = pltpu.make_async_copy(h_ref.at[tile // tiles, pl.ds(lo, n), :],
                                           hf_ref.at[buf, pl.ds(dst, n), :], sem.at[buf])
                cp.start() if act == "start" else cp.wait()

    def fill_tile():
        n_meta, _, tiles = head
        for c in range(tiles):
            _, dst, n = tile_rows(c)

            @pl.when(lax.rem(i, tiles) == c)
            def _():
                if dst > 0:
                    hf_ref[slot, 0:dst, :] = meta_ref[n_meta - dst:, :]
                if dst + n < tm:
                    hf_ref[slot, dst + n:, :] = jnp.zeros((tm - dst - n, hf_ref.shape[2]), F32)

    def load_h(rows):
        return h_ref[rows, :] if not head else hf_ref[slot, rows, :]

    @pl.when(j == 0)
    def _():
        if head:
            @pl.when(i == 0)
            def _():
                tile_dma(i, slot, "start")

            tile_dma(i, slot, "wait")
            fill_tile()

            @pl.when(i + 1 < pl.num_programs(0))
            def _():
                tile_dma(i + 1, 1 - slot, "start")

        hb_ref[...] = load_h(slice(0, tm)).astype(BF16)
        o_ref[...] = jnp.zeros_like(o_ref)

    def cast_next_weights():
        for src, dst in zip(nxt_refs, cast_refs):
            dst[...] = src[...].astype(BF16)

    def partial_out(rows):
        hb = hb_ref[rows, :]
        u = _dot(hb, w1_ref[...])
        v = _dot(hb, w3_ref[...])
        return _dot((u * _sigmoid(u) * v).astype(BF16), w2_ref[...])

    last = pl.num_programs(1) - 1

    @pl.when(j < last)
    def _():
        cast_next_weights()
        o_ref[...] += partial_out(slice(0, tm))

    @pl.when(j == last)
    def _():
        cast_next_weights()
        for rows in (slice(0, tm // 2), slice(tm // 2, tm)):
            y = alpha * load_h(rows) + 0.5 * (o_ref[rows, :] + partial_out(rows))
            o_ref[rows, :] = _layer_norm(y, g_ref[...], b_ref[...])


def _ffn_ln(h, w1, w3, w2, g, b, alpha, lp, nxt=None, meta=None, tm=None, tf=512):
    d = h.shape[-1]
    f = w1.shape[-1]
    batch = h.shape[0] if meta is not None else h.shape[0] // lp
    t = batch * lp
    tm = tm or _pick_tm(lp, 544)
    ni, nj = t // tm, f // tf
    head = None
    if meta is None:
        in_specs, args = [pl.BlockSpec((tm, d), lambda i, j: (i, 0))], [h]
        scratch = [pltpu.VMEM((tm, d), BF16)]
    else:
        head = (meta.shape[0], h.shape[1], lp // tm)
        in_specs = [pl.BlockSpec(memory_space=pl.ANY), pl.BlockSpec(meta.shape, lambda i, j: (0, 0))]
        args = [h, meta]
        scratch = [pltpu.VMEM((tm, d), BF16), pltpu.VMEM((2, tm, d), F32), pltpu.SemaphoreType.DMA((2,))]
    in_specs += [pl.BlockSpec((d, tf), lambda i, j: (0, j)),
                 pl.BlockSpec((d, tf), lambda i, j: (0, j)),
                 pl.BlockSpec((tf, d), lambda i, j: (j, 0)),
                 pl.BlockSpec((1, d), lambda i, j: (0, 0)),
                 pl.BlockSpec((1, d), lambda i, j: (0, 0))]
    args += [w1, w3, w2, g.reshape(1, d), b.reshape(1, d)]
    out_specs = [pl.BlockSpec((tm, d), lambda i, j: (i, 0))]
    out_shape = [jax.ShapeDtypeStruct((t, d), F32)]
    if nxt is not None:
        n1, n3, n2, l, s = nxt
        dr = d // ni
        assert dr * ni == d and dr % LANES == 0
        in_specs += [pl.BlockSpec((None, None, dr, tf), lambda i, j: (l, s, i, j)),
                     pl.BlockSpec((None, None, dr, tf), lambda i, j: (l, s, i, j)),
                     pl.BlockSpec((None, None, tf, dr), lambda i, j: (l, s, j, i))]
        out_specs += [pl.BlockSpec((dr, tf), lambda i, j: (i, j)),
                      pl.BlockSpec((dr, tf), lambda i, j: (i, j)),
                      pl.BlockSpec((tf, dr), lambda i, j: (j, i))]
        out_shape += [jax.ShapeDtypeStruct((d, f), BF16), jax.ShapeDtypeStruct((d, f), BF16),
                      jax.ShapeDtypeStruct((f, d), BF16)]
        args += [n1, n3, n2]
    out = pl.pallas_call(
        functools.partial(_ffn_ln_kernel, alpha=alpha, has_next=nxt is not None, head=head),
        grid=(ni, nj),
        in_specs=in_specs,
        out_specs=out_specs,
        out_shape=out_shape,
        scratch_shapes=scratch,
        name="ffn_ln",
        compiler_params=_cparams(("arbitrary" if head else "parallel", "arbitrary")),
    )(*args)
    return out[0], tuple(out[1:])


def _cumsum_rows(x):
    n = x.shape[0]
    row = lax.broadcasted_iota(jnp.int32, x.shape, 0)
    s = 1
    while s < n:
        x = x + jnp.where(row >= s, pltpu.roll(x, s, 0), 0.0)
        s *= 2
    return x


def _mm1(a, b):
    return _dot(a.astype(BF16), b.astype(BF16))


def _wkv_chunk(tiles, params, states, c):
    m0, strict, incl, ones_bd = c
    cc = tiles[0][0].shape[0]
    c2 = 2 * cc
    n = len(tiles)
    inv_n = 1.0 / RW_HEAD

    def stack2(x):
        return jnp.concatenate([jnp.where(m0, x, 0.0), jnp.where(m0, 0.0, x)], axis=0)

    def segsum(xs):
        return [_dot(x.astype(BF16), ones_bd) for x in xs]

    kk0 = [t[2] * p[0] for t, p in zip(tiles, params)]
    nrm2 = segsum([x * x for x in kk0])
    kmod = [t[2] * (1.0 + (t[4] - 1.0) * p[1]) for t, p in zip(tiles, params)]
    bon = segsum([t[0] * km * p[2] for t, km, p in zip(tiles, kmod, params)])

    lhs, rhs, vs, upd_r, dec = [], [], [], [], []
    for i in range(n):
        r, lw, k, v, a, g = tiles[i]
        kk = kk0[i] / jnp.maximum(jnp.sqrt(nrm2[i]), 1e-12)
        b = kk * a
        lcum = _cumsum_rows(lw)
        ltot = lcum[cc - 1:cc, :]
        e_neg = jnp.exp(-lcum)
        e_rem = jnp.exp(ltot - lcum)
        kt = kk * jnp.exp(lcum - lw)
        rt = r * jnp.exp(lcum)
        khb = (kmod[i] * e_neg).astype(BF16)
        bhb = (b * e_neg).astype(BF16)
        lhs.append(jnp.concatenate([stack2(kt), stack2(rt)], axis=0).astype(BF16))
        rhs.append(jnp.concatenate([khb, khb, bhb, bhb, states[i].astype(BF16)], axis=0))
        vs.append(stack2(v))
        upd_r.append(jnp.concatenate([stack2(kmod[i] * e_rem), stack2(b * e_rem)], axis=0).astype(BF16))
        dec.append(jnp.exp(ltot))

    m = [_dot_nt(a_, b_) for a_, b_ in zip(lhs, rhs)]
    a_k = [jnp.where(strict, x[:c2, :c2], 0.0) for x in m]
    a_b = [jnp.where(strict, x[:c2, c2:2 * c2], 0.0) for x in m]
    bkb = [jnp.concatenate([jnp.where(incl, x[c2:, :c2], 0.0), jnp.where(incl, -x[c2:, c2:2 * c2], 0.0)],
                           axis=1).astype(BF16) for x in m]
    rhs_u = [x[:c2, 2 * c2:] + _mm1(ak, v_) for x, ak, v_ in zip(m, a_k, vs)]
    pw = [ab.astype(BF16) for ab in a_b]
    st = [_dot(p_, jnp.concatenate([p_, ru.astype(BF16)], axis=1)) for p_, ru in zip(pw, rhs_u)]
    pw = [s_[:, :c2].astype(BF16) for s_ in st]
    us = [ru - s_[:, c2:] for ru, s_ in zip(rhs_u, st)]
    lvl = 4
    while lvl < cc:
        st = [_dot(p_, jnp.concatenate([p_, u_.astype(BF16)], axis=1)) for p_, u_ in zip(pw, us)]
        pw = [s_[:, :c2].astype(BF16) for s_ in st]
        us = [u_ + s_[:, c2:] for u_, s_ in zip(us, st)]
        lvl *= 2
    us = [u_ + _dot(p_, u_.astype(BF16)) for u_, p_ in zip(us, pw)]
    ys = [x[c2:, 2 * c2:] + _dot(bk, jnp.concatenate([v_, u_], axis=0).astype(BF16))
          for x, bk, v_, u_ in zip(m, bkb, vs, us)]
    y = [x[:cc] + x[cc:] for x in ys]
    upd_l = [jnp.concatenate([v_, -u_], axis=0).T.astype(BF16) for v_, u_ in zip(vs, us)]
    s_new = [s0 * d_ + _dot(ul, ur) for s0, d_, ul, ur in zip(states, dec, upd_l, upd_r)]

    mu = [x * inv_n for x in segsum(y)]
    yc = [a_ - b_ for a_, b_ in zip(y, mu)]
    var = [x * inv_n for x in segsum([x * x for x in yc])]
    outs = []
    for i in range(n):
        yn = yc[i] * lax.rsqrt(var[i] + RW_GN_EPS) * params[i][3] + params[i][4]
        outs.append((yn + bon[i] * tiles[i][3]) * tiles[i][5])
    return outs, s_new


def _wkv_kernel(r_ref, k_ref, v_ref, lora_ref, w2_ref, a2_ref, g2_ref, w0_ref, a0_ref,
                kk_ref, ka_ref, rk_ref, lg_ref, lb_ref, o_ref, s_ref, *, pairs, r1):
    @pl.when(pl.program_id(2) == 0)
    def _():
        s_ref[...] = jnp.zeros_like(s_ref)

    lo = lora_ref[...]

    cc = r_ref.shape[0]
    c2 = 2 * cc
    lane = lax.broadcasted_iota(jnp.int32, (cc, LANES), 1)
    m0 = lane < RW_HEAD
    row = lax.broadcasted_iota(jnp.int32, (c2, c2), 0)
    col = lax.broadcasted_iota(jnp.int32, (c2, c2), 1)
    same = (row >= cc) == (col >= cc)
    strict = same & (col < row)
    incl = same & (col <= row)
    lr = lax.broadcasted_iota(jnp.int32, (LANES, LANES), 0)
    lc = lax.broadcasted_iota(jnp.int32, (LANES, LANES), 1)
    ones_bd = jnp.where((lr >= RW_HEAD) == (lc >= RW_HEAD), 1.0, 0.0).astype(BF16)
    consts = (m0, strict, incl, ones_bd)

    lw_all = -DECAY_SCALE * _sigmoid(_dot(lo[:, :r1], w2_ref[...]) + w0_ref[...])
    a_all = _sigmoid(_dot(lo[:, r1:2 * r1], a2_ref[...]) + a0_ref[...])
    g_all = _dot(lo[:, 2 * r1:], g2_ref[...])

    sls = [slice(p * LANES, (p + 1) * LANES) for p in range(pairs)]
    tiles = [(r_ref[:, sl], lw_all[:, sl], k_ref[:, sl], v_ref[:, sl], a_all[:, sl], g_all[:, sl]) for sl in sls]
    params = [tuple(ref[:, sl] for ref in (kk_ref, ka_ref, rk_ref, lg_ref, lb_ref)) for sl in sls]
    outs, s_new = _wkv_chunk(tiles, params, [s_ref[p] for p in range(pairs)], consts)
    for p in range(pairs):
        s_ref[p] = s_new[p]
        o_ref[:, sls[p]] = outs[p].astype(o_ref.dtype)


def _wkv(rkv, lora, w2, a2, g2, w0, a0, kkp, kap, rkp, lgp, lbp, batch, r1, out_dtype=BF16):
    _, t, d = rkv.shape
    nl = lora.shape[1]
    lp = t // batch
    cc = WKV_CHUNK
    nchunk = lp // cc
    pairs = min(WKV_PAIRS_PER_STEP, d // LANES)
    wblk = pairs * LANES
    tok = pl.BlockSpec((cc, wblk), lambda b, p, c: (b * nchunk + c, p))
    r_spec, k_spec, v_spec = [pl.BlockSpec((None, cc, wblk), lambda b, p, c, n=n: (n, b * nchunk + c, p))
                              for n in range(3)]
    lo_spec = pl.BlockSpec((cc, nl), lambda b, p, c: (b * nchunk + c, 0))
    up = [pl.BlockSpec((w.shape[0], wblk), lambda b, p, c: (0, p)) for w in (w2, a2, g2)]
    par = pl.BlockSpec((1, wblk), lambda b, p, c: (0, p))
    prm = [x.reshape(1, d) for x in (w0, a0, kkp, kap, rkp, lgp, lbp)]
    return pl.pallas_call(
        functools.partial(_wkv_kernel, pairs=pairs, r1=r1),
        grid=(batch, d // wblk, nchunk),
        in_specs=[r_spec, k_spec, v_spec, lo_spec] + up + [par] * 7,
        out_specs=tok,
        out_shape=jax.ShapeDtypeStruct((t, d), out_dtype),
        scratch_shapes=[pltpu.VMEM((pairs, LANES, LANES), F32)],
        name="wkv7",
        compiler_params=_cparams(("parallel", "parallel", "arbitrary")),
    )(rkv, rkv, rkv, lora, w2, a2, g2, *prm)


def _fgate_kernel(h_ref, wh_ref, wl_ref, b_ref, o_ref, carry_ref):
    @pl.when(pl.program_id(1) == 0)
    def _():
        carry_ref[...] = jnp.zeros_like(carry_ref)

    hh, hl = _split(h_ref[...])
    z = _dot(hh, wh_ref[...]) + (_dot(hh, wl_ref[...]) + _dot(hl, wh_ref[...])) + b_ref[...]
    logf = -_softplus(-z) * LOG2E
    cs = _cumsum_rows(logf) + carry_ref[...]
    o_ref[...] = cs
    carry_ref[...] = cs[cs.shape[0] - 1:, :]


def _fgate_cumsum(h, wf, bf, batch, tm=544):
    t, d = h.shape
    lp = t // batch
    nt = lp // tm
    nh = wf.shape[1]
    wpad = jnp.zeros((d, LANES), F32).at[:, :nh].set(wf)
    bpad = jnp.zeros((1, LANES), F32).at[0, :nh].set(bf)
    wh, wl = _split(wpad)
    return pl.pallas_call(
        _fgate_kernel,
        grid=(batch, nt),
        in_specs=[pl.BlockSpec((tm, d), lambda b, i: (b * nt + i, 0)),
                  pl.BlockSpec((d, LANES), lambda b, i: (0, 0)),
                  pl.BlockSpec((d, LANES), lambda b, i: (0, 0)),
                  pl.BlockSpec((1, LANES), lambda b, i: (0, 0))],
        out_specs=pl.BlockSpec((tm, LANES), lambda b, i: (b * nt + i, 0)),
        out_shape=jax.ShapeDtypeStruct((t, LANES), F32),
        scratch_shapes=[pltpu.VMEM((1, LANES), F32)],
        name="fgate_cumsum",
        compiler_params=_cparams(("parallel", "arbitrary")),
    )(h, wh, wl, bpad)


def _fox_kernel(q_ref, k_ref, vt_ref, cc_ref, cr_ref, o_ref, *, heads, first, tile):
    lp = q_ref.shape[0]
    nbig = (lp - first) // tile
    hs = [slice(h * FX_HEAD, (h + 1) * FX_HEAD) for h in range(heads)]

    def scores(q, kstart, ksize):
        return [_dot_nt(k_ref[pl.ds(kstart, ksize), hs[h]], q[h]) for h in range(heads)]

    def update(s, cq, carry, kstart, ksize, diag):
        s = [s[h] + cq[h] - cc_ref[0, 0, pl.ds(kstart, ksize), h:h + 1] for h in range(heads)]
        if diag:
            keep = (lax.broadcasted_iota(jnp.int32, (ksize, ksize), 0)
                    <= lax.broadcasted_iota(jnp.int32, (ksize, ksize), 1))
            s = [jnp.where(keep, x, NEG_INF) for x in s]
        m_new = [jnp.maximum(carry[h][0], jnp.max(s[h], axis=0, keepdims=True)) for h in range(heads)]
        alpha = [jnp.exp2(carry[h][0] - m_new[h]) for h in range(heads)]
        p = [jnp.exp2(s[h] - m_new[h]) for h in range(heads)]
        l_new = [alpha[h] * carry[h][1] + jnp.sum(p[h], axis=0, keepdims=True) for h in range(heads)]
        pv = [_dot(vt_ref[hs[h], pl.ds(kstart, ksize)], p[h].astype(BF16)) for h in range(heads)]
        return [(m_new[h], l_new[h], alpha[h] * carry[h][2] + pv[h]) for h in range(heads)]

    def q_tile(qstart, tq, nfull, with_first):
        q = [q_ref[pl.ds(qstart, tq), hs[h]] for h in range(heads)]
        cq = [cr_ref[0, h, :, pl.ds(qstart, tq)] for h in range(heads)]
        carry = [(jnp.full((1, tq), NEG_INF, F32), jnp.zeros((1, tq), F32), jnp.zeros((FX_HEAD, tq), F32))
                 for _ in range(heads)]
        def step(c, kstart, ksize, diag):
            return update(scores(q, kstart, ksize), cq, c, kstart, ksize, diag)

        def step2(j, c):
            k0 = pl.multiple_of(first + 2 * j * tile, LANES)
            k1 = pl.multiple_of(first + (2 * j + 1) * tile, LANES)
            s0, s1 = scores(q, k0, tile), scores(q, k1, tile)
            return update(s1, cq, update(s0, cq, c, k0, tile, False), k1, tile, False)

        if nfull is not None:
            carry = lax.fori_loop(0, lax.shift_right_logical(nfull, 1), step2, carry)
            carry = lax.fori_loop(
                0, nfull & 1,
                lambda _, c: step(c, pl.multiple_of(first + (nfull - 1) * tile, LANES), tile, False), carry)
        s_first = scores(q, 0, first) if with_first else None
        s_diag = scores(q, qstart, tq)
        if with_first:
            carry = update(s_first, cq, carry, 0, first, False)
        carry = update(s_diag, cq, carry, qstart, tq, True)
        for h in range(heads):
            o_ref[pl.ds(qstart, tq), hs[h]] = (carry[h][2] / carry[h][1]).T.astype(o_ref.dtype)

    if first:
        q_tile(0, first, None, False)

    def big(i, _):
        q_tile(pl.multiple_of(first + i * tile, LANES), tile, i, first > 0)
        return 0

    lax.fori_loop(0, nbig, big, 0)


FOX_Q_SCALE = FX_HEAD ** -0.5 * LOG2E


def _fox_attention(q, k, vt, c, batch, heads=FOX_HEADS_PER_STEP, tile=256):
    t, d = q.shape
    lp = t // batch
    nh = d // FX_HEAD
    ng = nh // heads
    wblk = heads * FX_HEAD
    first = lp % tile
    c_col = jnp.transpose(c.reshape(batch, lp, ng, heads), (0, 2, 1, 3))
    c_row = jnp.transpose(c, (0, 2, 1))[:, :, None, :]
    return pl.pallas_call(
        functools.partial(_fox_kernel, heads=heads, first=first, tile=tile),
        grid=(batch, ng),
        in_specs=[pl.BlockSpec((lp, wblk), lambda b, g: (b, g)),
                  pl.BlockSpec((lp, wblk), lambda b, g: (b, g)),
                  pl.BlockSpec((wblk, lp), lambda b, g: (g, b)),
                  pl.BlockSpec((1, 1, lp, heads), lambda b, g: (b, g, 0, 0)),
                  pl.BlockSpec((1, heads, 1, lp), lambda b, g: (b, g, 0, 0))],
        out_specs=pl.BlockSpec((lp, wblk), lambda b, g: (b, g)),
        out_shape=jax.ShapeDtypeStruct((t, d), BF16),
        name="fox_attention",
        compiler_params=_cparams(("parallel", "parallel")),
    )(q, k, vt, c_col, c_row)


def _pad_cols(w, n):
    return jnp.zeros((w.shape[0], n), w.dtype).at[:, :w.shape[1]].set(w)


def _pad_rows(w, n):
    return jnp.zeros((n, w.shape[1]), w.dtype).at[:w.shape[0], :].set(w)


def _rwkv7_mix(h, batch, mu, w_rkv, w_o, w0, w1, w2, a0, a1, a2, g1, g2, k_k, k_a, r_k, lnx_g, lnx_b,
               ln_g, ln_b, alpha):
    r1 = LANES
    lw = jnp.concatenate([_pad_cols(w1, r1), _pad_cols(a1, r1), g1], axis=1).astype(BF16)
    rkv, lora = _rwkv_in(h, mu, w_rkv.astype(BF16), lw, batch, r1)
    o = _wkv(rkv, lora, _pad_rows(w2, r1).astype(BF16), _pad_rows(a2, r1).astype(BF16), g2.astype(BF16),
             w0, a0, k_k, k_a, r_k.reshape(-1), lnx_g, lnx_b, batch, r1)
    return _matmul_ln(o, w_o.astype(BF16), h, ln_g, ln_b, alpha)


def kernel(x, meta_tokens, ln_g, ln_b, ffn_w1, ffn_w3, ffn_w2, rw_mu, rw_w_rkv, rw_w_o, rw_w0, rw_w1,
           rw_w2, rw_a0, rw_a1, rw_a2, rw_g1, rw_g2, rw_k_k, rw_k_a, rw_r_k, rw_lnx_g, rw_lnx_b,
           fx_w_q, fx_w_o, fx_w_kvf, fx_b_f):
    batch, seq, d = x.shape
    depth = ln_g.shape[0]
    n_a = rw_mu.shape[0]
    alpha = (2 * depth) ** 0.25
    l_real = seq + N_META
    lp = -(-l_real // SEQ_ALIGN) * SEQ_ALIGN
    t = batch * lp
    nh = d // FX_HEAD

    wb = (ffn_w1[0, 0].astype(BF16), ffn_w3[0, 0].astype(BF16), ffn_w2[0, 0].astype(BF16))

    def ffn(h, wb, l, s):
        first = l == 0 and s == 0
        last = l == depth - 1 and s == 1
        nxt = None if last else (ffn_w1, ffn_w3, ffn_w2, l + s, 1 - s)
        return _ffn_ln(h, *wb, ln_g[l, 2 * s], ln_b[l, 2 * s], alpha, lp, nxt=nxt,
                       meta=meta_tokens.astype(x.dtype) if first else None)

    h = x
    k_s = vt_s = c = None
    for l in range(depth):
        h, wb = ffn(h, wb, l, 0)
        if l < n_a:
            h = _rwkv7_mix(h, batch, rw_mu[l], rw_w_rkv[l], rw_w_o[l], rw_w0[l], rw_w1[l], rw_w2[l],
                           rw_a0[l], rw_a1[l], rw_a2[l], rw_g1[l], rw_g2[l], rw_k_k[l], rw_k_a[l],
                           rw_r_k[l], rw_lnx_g[l], rw_lnx_b[l], ln_g[l, 1], ln_b[l, 1], alpha)
        else:
            j = l - n_a
            q = _proj(h, fx_w_q[j].astype(BF16), d, "q_proj", scale=FOX_Q_SCALE)
            o = _fox_attention(q, k_s, vt_s, c, batch)
            h = _matmul_ln(o, fx_w_o[j].astype(BF16), h, ln_g[l, 1], ln_b[l, 1], alpha)
        h, wb = ffn(h, wb, l, 1)
        if l == n_a - 1:
            w_kvf = fx_w_kvf.astype(BF16)
            k_s = _proj(h, w_kvf, d, "k_proj")
            vt_s = _proj(h, w_kvf, d, "vt_proj", woff=d, transpose_out=True, tm=512)
            c = _fgate_cumsum(h, fx_w_kvf[:, 2 * d:], fx_b_f, batch)[:, :nh].reshape(batch, lp, nh)
    return h.reshape(batch, lp, d)[:, N_META:l_real]
```

```python
import functools
import math

import jax
import jax.numpy as jnp
from jax import lax
from jax.experimental import pallas as pl
from jax.experimental.pallas import tpu as pltpu

N_META = 16
RW_HEAD = 64
FX_HEAD = 128
LN_EPS = 1e-5
RW_GN_EPS = RW_HEAD * 1e-5
NEG_INF = -1e30
LOG2E = math.log2(math.e)
DECAY_SCALE = math.exp(-0.5)

LANES = 128
V7X_VMEM_BYTES = 64 * 1024 * 1024
VMEM_LIMIT = V7X_VMEM_BYTES * 7 // 8
SEQ_ALIGN = 128
WKV_CHUNK = 64
WKV_PAIRS_PER_STEP = 16
FOX_HEADS_PER_STEP = 8

F32 = jnp.float32
BF16 = jnp.bfloat16


def _cparams(sem):
    return pltpu.CompilerParams(dimension_semantics=sem, vmem_limit_bytes=VMEM_LIMIT)


def _dot(a, b):
    return jnp.dot(a, b, preferred_element_type=F32)


def _dot_nt(a, b):
    return lax.dot_general(a, b, (((1,), (1,)), ((), ())), preferred_element_type=F32)


def _split(x):
    hi = x.astype(BF16)
    lo = (x - hi.astype(F32)).astype(BF16)
    return hi, lo


def _softplus(x):
    return jnp.maximum(x, 0.0) + jnp.log1p(jnp.exp(-jnp.abs(x)))


def _sigmoid(x):
    return 1.0 / (1.0 + jnp.exp(-x))


def _layer_norm(y, g, b):
    mu = jnp.mean(y, axis=-1, keepdims=True)
    yc = y - mu
    var = jnp.mean(yc * yc, axis=-1, keepdims=True)
    return yc * lax.rsqrt(var + LN_EPS) * g + b


def _proj_kernel(x_ref, w_ref, o_ref, xb_ref, *, scale, transpose_out):
    @pl.when(pl.program_id(1) == 0)
    def _():
        xb_ref[...] = x_ref[...].astype(BF16)

    acc = _dot(xb_ref[...], w_ref[...])
    if scale is not None:
        acc = acc * scale
    if transpose_out:
        acc = acc.T
    o_ref[...] = acc.astype(o_ref.dtype)


def _proj(x, w, n, name, woff=0, scale=None, transpose_out=False, tm=None, tn=1024):
    t, k = x.shape
    tm = tm or _pick_tm(t)
    jo = woff // tn
    if transpose_out:
        out_spec, out_shape = pl.BlockSpec((tn, tm), lambda i, j: (j, i)), (n, t)
    else:
        out_spec, out_shape = pl.BlockSpec((tm, tn), lambda i, j: (i, j)), (t, n)
    return pl.pallas_call(
        functools.partial(_proj_kernel, scale=scale, transpose_out=transpose_out),
        grid=(t // tm, n // tn),
        in_specs=[pl.BlockSpec((tm, k), lambda i, j: (i, 0)),
                  pl.BlockSpec((k, tn), lambda i, j: (0, j + jo))],
        out_specs=out_spec,
        out_shape=jax.ShapeDtypeStruct(out_shape, BF16),
        scratch_shapes=[pltpu.VMEM((tm, k), BF16)],
        name=name,
        compiler_params=_cparams(("parallel", "arbitrary")),
    )(x, w)


def _pick_tm(t, cap=1088):
    for tm in range(min(cap, t), 0, -16):
        if t % tm == 0:
            return tm
    return t


PREV_ROWS = 8


def _rwkv_in_kernel(h_ref, prev_ref, mu_ref, w_ref, lw_ref, rkv_ref, lora_ref, xs_ref, *, lp, nj, r1):
    i = pl.program_id(0)
    j = pl.program_id(1)
    tm = h_ref.shape[0]

    @pl.when(j == 0)
    def _():
        h = h_ref[...]
        last = prev_ref[PREV_ROWS - 1:, :]
        last = jnp.where(lax.rem(i * tm, lp) == 0, 0.0, last)
        row = lax.broadcasted_iota(jnp.int32, h.shape, 0)
        xx = jnp.where(row == 0, last, pltpu.roll(h, 1, 0)) - h

        def mix(n):
            return (h + xx * mu_ref[n:n + 1, :]).astype(BF16)

        xs_ref[0] = mix(0)
        xs_ref[1] = mix(2)
        xs_ref[2] = mix(3)
        tw = jnp.tanh(_dot(mix(1), lw_ref[:, :r1]))
        ta = _dot(mix(4), lw_ref[:, r1:2 * r1])
        tg = _sigmoid(_dot(mix(5), lw_ref[:, 2 * r1:]))
        lora_ref[...] = jnp.concatenate([tw, ta, tg], axis=1).astype(BF16)

    rkv_ref[...] = _dot(xs_ref[j // nj], w_ref[...])


def _rwkv_in(h, mu, w_rkv, lw, batch, r1, tm=None, tn=512):
    t, d = h.shape
    lp = t // batch
    tm = tm or _pick_tm(lp, 1088)
    nj = d // tn
    nl = lw.shape[1]
    pblk = tm // PREV_ROWS
    return pl.pallas_call(
        functools.partial(_rwkv_in_kernel, lp=lp, nj=nj, r1=r1),
        grid=(t // tm, 3 * nj),
        in_specs=[pl.BlockSpec((tm, d), lambda i, j: (i, 0)),
                  pl.BlockSpec((PREV_ROWS, d), lambda i, j: (jnp.maximum(i * pblk - 1, 0), 0)),
                  pl.BlockSpec((6, d), lambda i, j: (0, 0)),
                  pl.BlockSpec((None, d, tn), lambda i, j: (j // nj, 0, j % nj)),
                  pl.BlockSpec((d, nl), lambda i, j: (0, 0))],
        out_specs=[pl.BlockSpec((None, tm, tn), lambda i, j: (j // nj, i, j % nj)),
                   pl.BlockSpec((tm, nl), lambda i, j: (i, 0))],
        out_shape=[jax.ShapeDtypeStruct((3, t, d), F32), jax.ShapeDtypeStruct((t, nl), BF16)],
        scratch_shapes=[pltpu.VMEM((3, tm, d), BF16)],
        name="rwkv_in",
        compiler_params=_cparams(("parallel", "arbitrary")),
    )(h, h, mu, w_rkv, lw)


def _mm_ln_kernel(x_ref, w_ref, h_ref, g_ref, b_ref, o_ref, *, alpha, splits):
    rs = x_ref.shape[0] // splits
    rows = [slice(n * rs, (n + 1) * rs) for n in range(splits)]
    ys = [alpha * h_ref[r, :] + _dot(x_ref[r, :], w_ref[...]) for r in rows]
    for r, y in zip(rows, ys):
        o_ref[r, :] = _layer_norm(y, g_ref[...], b_ref[...])


def _matmul_ln(x, w, h, g, b, alpha, tm=None):
    t, k = x.shape
    n = w.shape[1]
    tm = tm or _pick_tm(t, 512)
    return pl.pallas_call(
        functools.partial(_mm_ln_kernel, alpha=alpha, splits=4 if tm % 64 == 0 else 1),
        grid=(t // tm,),
        in_specs=[pl.BlockSpec((tm, k), lambda i: (i, 0)),
                  pl.BlockSpec((k, n), lambda i: (0, 0)),
                  pl.BlockSpec((tm, n), lambda i: (i, 0)),
                  pl.BlockSpec((1, n), lambda i: (0, 0)),
                  pl.BlockSpec((1, n), lambda i: (0, 0))],
        out_specs=pl.BlockSpec((tm, n), lambda i: (i, 0)),
        out_shape=jax.ShapeDtypeStruct((t, n), F32),
        name="proj_ln",
        compiler_params=_cparams(("parallel",)),
    )(x, w, h, g.reshape(1, n), b.reshape(1, n))


def _ffn_ln_kernel(*refs, alpha, has_next, head, tail):
    refs = list(refs)
    h_ref = refs.pop(0)
    meta_ref = refs.pop(0) if head else None
    w1_ref, w3_ref, w2_ref, g_ref, b_ref = refs[:5]
    nxt_refs = refs[5:8] if has_next else ()
    o_ref = refs[8 if has_next else 5]
    cast_refs = refs[9:12] if has_next else ()
    hb_ref = refs[12 if has_next else 6]
    buf_ref, sem = refs[-2:] if head or tail else (None, None)
    i = pl.program_id(0)
    j = pl.program_id(1)
    ni = pl.num_programs(0)
    tm = hb_ref.shape[0]
    slot = lax.rem(i, 2)
    acc_ref = buf_ref.at[slot] if tail else o_ref

    def tile_rows(c):
        n_meta, seq, _ = head or tail
        lo, hi = max(c * tm - n_meta, 0), min((c + 1) * tm - n_meta, seq)
        return lo, lo + n_meta - c * tm, hi - lo

    def tile_dma(tile, buf, act):
        tiles = (head or tail)[2]
        for c in range(tiles):
            lo, dst, n = tile_rows(c)

            @pl.when(lax.rem(tile, tiles) == c)
            def _():
                hbm = (h_ref if head else o_ref).at[tile // tiles, pl.ds(lo, n), :]
                vmem = buf_ref.at[buf, pl.ds(dst, n), :]
                cp = pltpu.make_async_copy(*((hbm, vmem) if head else (vmem, hbm)), sem.at[buf])
                cp.start() if act == "start" else cp.wait()

    def fill_tile():
        n_meta, _, tiles = head
        hf_ref = buf_ref
        for c in range(tiles):
            _, dst, n = tile_rows(c)

            @pl.when(lax.rem(i, tiles) == c)
            def _():
                if dst > 0:
                    hf_ref[slot, 0:dst, :] = meta_ref[n_meta - dst:, :]
                if dst + n < tm:
                    hf_ref[slot, dst + n:, :] = jnp.zeros((tm - dst - n, hf_ref.shape[2]), F32)

    def load_h(rows):
        return h_ref[rows, :] if not head else buf_ref[slot, rows, :]

    @pl.when(j == 0)
    def _():
        if head:
            @pl.when(i == 0)
            def _():
                tile_dma(i, slot, "start")

            tile_dma(i, slot, "wait")
            fill_tile()

            @pl.when(i + 1 < ni)
            def _():
                tile_dma(i + 1, 1 - slot, "start")
        if tail:
            @pl.when(i >= 2)
            def _():
                tile_dma(i - 2, slot, "wait")

        hb_ref[...] = load_h(slice(0, tm)).astype(BF16)
        acc_ref[...] = jnp.zeros(acc_ref.shape, F32)

    def cast_next_weights():
        for src, dst in zip(nxt_refs, cast_refs):
            dst[...] = src[...].astype(BF16)

    def partial_out(rows):
        hb = hb_ref[rows, :]
        u = _dot(hb, w1_ref[...])
        v = _dot(hb, w3_ref[...])
        return _dot((u * _sigmoid(u) * v).astype(BF16), w2_ref[...])

    last = pl.num_programs(1) - 1

    @pl.when(j < last)
    def _():
        cast_next_weights()
        acc_ref[...] += partial_out(slice(0, tm))

    @pl.when(j == last)
    def _():
        cast_next_weights()
        for rows in (slice(0, tm // 2), slice(tm // 2, tm)):
            y = alpha * load_h(rows) + 0.5 * (acc_ref[rows, :] + partial_out(rows))
            acc_ref[rows, :] = _layer_norm(y, g_ref[...], b_ref[...])
        if tail:
            tile_dma(i, slot, "start")

            @pl.when(i == ni - 1)
            def _():
                @pl.when(i >= 1)
                def _():
                    tile_dma(i - 1, 1 - slot, "wait")

                tile_dma(i, slot, "wait")


def _ffn_ln(h, w1, w3, w2, g, b, alpha, lp, nxt=None, meta=None, unpad=None, tm=None, tf=512):
    d = h.shape[-1]
    f = w1.shape[-1]
    batch = h.shape[0] if meta is not None else h.shape[0] // lp
    t = batch * lp
    tm = tm or _pick_tm(lp, 544)
    ni, nj = t // tm, f // tf
    head = tail = None
    scratch = [pltpu.VMEM((tm, d), BF16)]
    if meta is None:
        in_specs, args = [pl.BlockSpec((tm, d), lambda i, j: (i, 0))], [h]
    else:
        head = (meta.shape[0], h.shape[1], lp // tm)
        in_specs = [pl.BlockSpec(memory_space=pl.ANY), pl.BlockSpec(meta.shape, lambda i, j: (0, 0))]
        args = [h, meta]
    in_specs += [pl.BlockSpec((d, tf), lambda i, j: (0, j)),
                 pl.BlockSpec((d, tf), lambda i, j: (0, j)),
                 pl.BlockSpec((tf, d), lambda i, j: (j, 0)),
                 pl.BlockSpec((1, d), lambda i, j: (0, 0)),
                 pl.BlockSpec((1, d), lambda i, j: (0, 0))]
    args += [w1, w3, w2, g.reshape(1, d), b.reshape(1, d)]
    if unpad is None:
        out_specs = [pl.BlockSpec((tm, d), lambda i, j: (i, 0))]
        out_shape = [jax.ShapeDtypeStruct((t, d), F32)]
    else:
        assert meta is None
        tail = (*unpad, lp // tm)
        out_specs = [pl.BlockSpec(memory_space=pl.ANY)]
        out_shape = [jax.ShapeDtypeStruct((batch, unpad[1], d), F32)]
    if head or tail:
        scratch += [pltpu.VMEM((2, tm, d), F32), pltpu.SemaphoreType.DMA((2,))]
    if nxt is not None:
        n1, n3, n2, l, s = nxt
        dr = d // ni
        assert dr * ni == d and dr % LANES == 0
        in_specs += [pl.BlockSpec((None, None, dr, tf), lambda i, j: (l, s, i, j)),
                     pl.BlockSpec((None, None, dr, tf), lambda i, j: (l, s, i, j)),
                     pl.BlockSpec((None, None, tf, dr), lambda i, j: (l, s, j, i))]
        out_specs += [pl.BlockSpec((dr, tf), lambda i, j: (i, j)),
                      pl.BlockSpec((dr, tf), lambda i, j: (i, j)),
                      pl.BlockSpec((tf, dr), lambda i, j: (j, i))]
        out_shape += [jax.ShapeDtypeStruct((d, f), BF16), jax.ShapeDtypeStruct((d, f), BF16),
                      jax.ShapeDtypeStruct((f, d), BF16)]
        args += [n1, n3, n2]
    out = pl.pallas_call(
        functools.partial(_ffn_ln_kernel, alpha=alpha, has_next=nxt is not None, head=head, tail=tail),
        grid=(ni, nj),
        in_specs=in_specs,
        out_specs=out_specs,
        out_shape=out_shape,
        scratch_shapes=scratch,
        name="ffn_ln",
        compiler_params=_cparams(("arbitrary" if head or tail else "parallel", "arbitrary")),
    )(*args)
    return out[0], tuple(out[1:])


def _cumsum_rows(x):
    n = x.shape[0]
    row = lax.broadcasted_iota(jnp.int32, x.shape, 0)
    s = 1
    while s < n:
        x = x + jnp.where(row >= s, pltpu.roll(x, s, 0), 0.0)
        s *= 2
    return x


def _mm1(a, b):
    return _dot(a.astype(BF16), b.astype(BF16))


def _wkv_chunk(tiles, params, states, c):
    m0, strict, incl, ones_bd = c
    cc = tiles[0][0].shape[0]
    c2 = 2 * cc
    n = len(tiles)
    inv_n = 1.0 / RW_HEAD

    def stack2(x):
        return jnp.concatenate([jnp.where(m0, x, 0.0), jnp.where(m0, 0.0, x)], axis=0)

    def segsum(xs):
        return [_dot(x.astype(BF16), ones_bd) for x in xs]

    kk0 = [t[2] * p[0] for t, p in zip(tiles, params)]
    nrm2 = segsum([x * x for x in kk0])
    kmod = [t[2] * (1.0 + (t[4] - 1.0) * p[1]) for t, p in zip(tiles, params)]
    bon = segsum([t[0] * km * p[2] for t, km, p in zip(tiles, kmod, params)])

    lhs, rhs, vs, upd_r, dec = [], [], [], [], []
    for i in range(n):
        r, lw, k, v, a, g = tiles[i]
        kk = kk0[i] / jnp.maximum(jnp.sqrt(nrm2[i]), 1e-12)
        b = kk * a
        lcum = _cumsum_rows(lw)
        ltot = lcum[cc - 1:cc, :]
        e_neg = jnp.exp(-lcum)
        e_rem = jnp.exp(ltot - lcum)
        kt = kk * jnp.exp(lcum - lw)
        rt = r * jnp.exp(lcum)
        khb = (kmod[i] * e_neg).astype(BF16)
        bhb = (b * e_neg).astype(BF16)
        lhs.append(jnp.concatenate([stack2(kt), stack2(rt)], axis=0).astype(BF16))
        rhs.append(jnp.concatenate([khb, khb, bhb, bhb, states[i].astype(BF16)], axis=0))
        vs.append(stack2(v))
        upd_r.append(jnp.concatenate([stack2(kmod[i] * e_rem), stack2(b * e_rem)], axis=0).astype(BF16))
        dec.append(jnp.exp(ltot))

    m = [_dot_nt(a_, b_) for a_, b_ in zip(lhs, rhs)]
    a_k = [jnp.where(strict, x[:c2, :c2], 0.0) for x in m]
    a_b = [jnp.where(strict, x[:c2, c2:2 * c2], 0.0) for x in m]
    bkb = [jnp.concatenate([jnp.where(incl, x[c2:, :c2], 0.0), jnp.where(incl, -x[c2:, c2:2 * c2], 0.0)],
                           axis=1).astype(BF16) for x in m]
    rhs_u = [x[:c2, 2 * c2:] + _mm1(ak, v_) for x, ak, v_ in zip(m, a_k, vs)]
    pw = [ab.astype(BF16) for ab in a_b]
    st = [_dot(p_, jnp.concatenate([p_, ru.astype(BF16)], axis=1)) for p_, ru in zip(pw, rhs_u)]
    pw = [s_[:, :c2].astype(BF16) for s_ in st]
    us = [ru - s_[:, c2:] for ru, s_ in zip(rhs_u, st)]
    lvl = 4
    while lvl < cc:
        st = [_dot(p_, jnp.concatenate([p_, u_.astype(BF16)], axis=1)) for p_, u_ in zip(pw, us)]
        pw = [s_[:, :c2].astype(BF16) for s_ in st]
        us = [u_ + s_[:, c2:] for u_, s_ in zip(us, st)]
        lvl *= 2
    us = [u_ + _dot(p_, u_.astype(BF16)) for u_, p_ in zip(us, pw)]
    ys = [x[c2:, 2 * c2:] + _dot(bk, jnp.concatenate([v_, u_], axis=0).astype(BF16))
          for x, bk, v_, u_ in zip(m, bkb, vs, us)]
    y = [x[:cc] + x[cc:] for x in ys]
    upd_l = [jnp.concatenate([v_, -u_], axis=0).T.astype(BF16) for v_, u_ in zip(vs, us)]
    s_new = [s0 * d_ + _dot(ul, ur) for s0, d_, ul, ur in zip(states, dec, upd_l, upd_r)]

    mu = [x * inv_n for x in segsum(y)]
    yc = [a_ - b_ for a_, b_ in zip(y, mu)]
    var = [x * inv_n for x in segsum([x * x for x in yc])]
    outs = []
    for i in range(n):
        yn = yc[i] * lax.rsqrt(var[i] + RW_GN_EPS) * params[i][3] + params[i][4]
        outs.append((yn + bon[i] * tiles[i][3]) * tiles[i][5])
    return outs, s_new


def _wkv_kernel(r_ref, k_ref, v_ref, lora_ref, w2_ref, a2_ref, g2_ref, w0_ref, a0_ref,
                kk_ref, ka_ref, rk_ref, lg_ref, lb_ref, o_ref, s_ref, *, pairs, r1):
    @pl.when(pl.program_id(2) == 0)
    def _():
        s_ref[...] = jnp.zeros_like(s_ref)

    lo = lora_ref[...]

    cc = r_ref.shape[0]
    c2 = 2 * cc
    lane = lax.broadcasted_iota(jnp.int32, (cc, LANES), 1)
    m0 = lane < RW_HEAD
    row = lax.broadcasted_iota(jnp.int32, (c2, c2), 0)
    col = lax.broadcasted_iota(jnp.int32, (c2, c2), 1)
    same = (row >= cc) == (col >= cc)
    strict = same & (col < row)
    incl = same & (col <= row)
    lr = lax.broadcasted_iota(jnp.int32, (LANES, LANES), 0)
    lc = lax.broadcasted_iota(jnp.int32, (LANES, LANES), 1)
    ones_bd = jnp.where((lr >= RW_HEAD) == (lc >= RW_HEAD), 1.0, 0.0).astype(BF16)
    consts = (m0, strict, incl, ones_bd)

    lw_all = -DECAY_SCALE * _sigmoid(_dot(lo[:, :r1], w2_ref[...]) + w0_ref[...])
    a_all = _sigmoid(_dot(lo[:, r1:2 * r1], a2_ref[...]) + a0_ref[...])
    g_all = _dot(lo[:, 2 * r1:], g2_ref[...])

    sls = [slice(p * LANES, (p + 1) * LANES) for p in range(pairs)]
    tiles = [(r_ref[:, sl], lw_all[:, sl], k_ref[:, sl], v_ref[:, sl], a_all[:, sl], g_all[:, sl]) for sl in sls]
    params = [tuple(ref[:, sl] for ref in (kk_ref, ka_ref, rk_ref, lg_ref, lb_ref)) for sl in sls]
    outs, s_new = _wkv_chunk(tiles, params, [s_ref[p] for p in range(pairs)], consts)
    for p in range(pairs):
        s_ref[p] = s_new[p]
        o_ref[:, sls[p]] = outs[p].astype(o_ref.dtype)


def _wkv(rkv, lora, w2, a2, g2, w0, a0, kkp, kap, rkp, lgp, lbp, batch, r1, out_dtype=BF16):
    _, t, d = rkv.shape
    nl = lora.shape[1]
    lp = t // batch
    cc = WKV_CHUNK
    nchunk = lp // cc
    pairs = min(WKV_PAIRS_PER_STEP, d // LANES)
    wblk = pairs * LANES
    tok = pl.BlockSpec((cc, wblk), lambda b, p, c: (b * nchunk + c, p))
    r_spec, k_spec, v_spec = [pl.BlockSpec((None, cc, wblk), lambda b, p, c, n=n: (n, b * nchunk + c, p))
                              for n in range(3)]
    lo_spec = pl.BlockSpec((cc, nl), lambda b, p, c: (b * nchunk + c, 0))
    up = [pl.BlockSpec((w.shape[0], wblk), lambda b, p, c: (0, p)) for w in (w2, a2, g2)]
    par = pl.BlockSpec((1, wblk), lambda b, p, c: (0, p))
    prm = [x.reshape(1, d) for x in (w0, a0, kkp, kap, rkp, lgp, lbp)]
    return pl.pallas_call(
        functools.partial(_wkv_kernel, pairs=pairs, r1=r1),
        grid=(batch, d // wblk, nchunk),
        in_specs=[r_spec, k_spec, v_spec, lo_spec] + up + [par] * 7,
        out_specs=tok,
        out_shape=jax.ShapeDtypeStruct((t, d), out_dtype),
        scratch_shapes=[pltpu.VMEM((pairs, LANES, LANES), F32)],
        name="wkv7",
        compiler_params=_cparams(("parallel", "parallel", "arbitrary")),
    )(rkv, rkv, rkv, lora, w2, a2, g2, *prm)


def _fgate_kernel(h_ref, wh_ref, wl_ref, b_ref, o_ref, carry_ref):
    @pl.when(pl.program_id(1) == 0)
    def _():
        carry_ref[...] = jnp.zeros_like(carry_ref)

    hh, hl = _split(h_ref[...])
    z = _dot(hh, wh_ref[...]) + (_dot(hh, wl_ref[...]) + _dot(hl, wh_ref[...])) + b_ref[...]
    logf = -_softplus(-z) * LOG2E
    cs = _cumsum_rows(logf) + carry_ref[...]
    o_ref[...] = cs
    carry_ref[...] = cs[cs.shape[0] - 1:, :]


def _fgate_cumsum(h, wf, bf, batch, tm=544):
    t, d = h.shape
    lp = t // batch
    nt = lp // tm
    nh = wf.shape[1]
    wpad = jnp.zeros((d, LANES), F32).at[:, :nh].set(wf)
    bpad = jnp.zeros((1, LANES), F32).at[0, :nh].set(bf)
    wh, wl = _split(wpad)
    return pl.pallas_call(
        _fgate_kernel,
        grid=(batch, nt),
        in_specs=[pl.BlockSpec((tm, d), lambda b, i: (b * nt + i, 0)),
                  pl.BlockSpec((d, LANES), lambda b, i: (0, 0)),
                  pl.BlockSpec((d, LANES), lambda b, i: (0, 0)),
                  pl.BlockSpec((1, LANES), lambda b, i: (0, 0))],
        out_specs=pl.BlockSpec((tm, LANES), lambda b, i: (b * nt + i, 0)),
        out_shape=jax.ShapeDtypeStruct((t, LANES), F32),
        scratch_shapes=[pltpu.VMEM((1, LANES), F32)],
        name="fgate_cumsum",
        compiler_params=_cparams(("parallel", "arbitrary")),
    )(h, wh, wl, bpad)


def _fox_kernel(q_ref, k_ref, vt_ref, cc_ref, cr_ref, o_ref, *, heads, first, tile):
    lp = q_ref.shape[0]
    nbig = (lp - first) // tile
    hs = [slice(h * FX_HEAD, (h + 1) * FX_HEAD) for h in range(heads)]

    def scores(q, kstart, ksize):
        return [_dot_nt(k_ref[pl.ds(kstart, ksize), hs[h]], q[h]) for h in range(heads)]

    def update(s, cq, carry, kstart, ksize, diag):
        s = [s[h] + cq[h] - cc_ref[0, 0, pl.ds(kstart, ksize), h:h + 1] for h in range(heads)]
        if diag:
            keep = (lax.broadcasted_iota(jnp.int32, (ksize, ksize), 0)
                    <= lax.broadcasted_iota(jnp.int32, (ksize, ksize), 1))
            s = [jnp.where(keep, x, NEG_INF) for x in s]
        m_new = [jnp.maximum(carry[h][0], jnp.max(s[h], axis=0, keepdims=True)) for h in range(heads)]
        alpha = [jnp.exp2(carry[h][0] - m_new[h]) for h in range(heads)]
        p = [jnp.exp2(s[h] - m_new[h]) for h in range(heads)]
        l_new = [alpha[h] * carry[h][1] + jnp.sum(p[h], axis=0, keepdims=True) for h in range(heads)]
        pv = [_dot(vt_ref[hs[h], pl.ds(kstart, ksize)], p[h].astype(BF16)) for h in range(heads)]
        return [(m_new[h], l_new[h], alpha[h] * carry[h][2] + pv[h]) for h in range(heads)]

    def q_tile(qstart, tq, nfull, with_first):
        q = [q_ref[pl.ds(qstart, tq), hs[h]] for h in range(heads)]
        cq = [cr_ref[0, h, :, pl.ds(qstart, tq)] for h in range(heads)]
        carry = [(jnp.full((1, tq), NEG_INF, F32), jnp.zeros((1, tq), F32), jnp.zeros((FX_HEAD, tq), F32))
                 for _ in range(heads)]
        def step(c, kstart, ksize, diag):
            return update(scores(q, kstart, ksize), cq, c, kstart, ksize, diag)

        def step2(j, c):
            k0 = pl.multiple_of(first + 2 * j * tile, LANES)
            k1 = pl.multiple_of(first + (2 * j + 1) * tile, LANES)
            s0, s1 = scores(q, k0, tile), scores(q, k1, tile)
            return update(s1, cq, update(s0, cq, c, k0, tile, False), k1, tile, False)

        if nfull is not None:
            carry = lax.fori_loop(0, lax.shift_right_logical(nfull, 1), step2, carry)
            carry = lax.fori_loop(
                0, nfull & 1,
                lambda _, c: step(c, pl.multiple_of(first + (nfull - 1) * tile, LANES), tile, False), carry)
        s_first = scores(q, 0, first) if with_first else None
        s_diag = scores(q, qstart, tq)
        if with_first:
            carry = update(s_first, cq, carry, 0, first, False)
        carry = update(s_diag, cq, carry, qstart, tq, True)
        for h in range(heads):
            o_ref[pl.ds(qstart, tq), hs[h]] = (carry[h][2] / carry[h][1]).T.astype(o_ref.dtype)

    if first:
        q_tile(0, first, None, False)

    def big(i, _):
        q_tile(pl.multiple_of(first + i * tile, LANES), tile, i, first > 0)
        return 0

    lax.fori_loop(0, nbig, big, 0)


FOX_Q_SCALE = FX_HEAD ** -0.5 * LOG2E


def _fox_attention(q, k, vt, c, batch, heads=FOX_HEADS_PER_STEP, tile=256):
    t, d = q.shape
    lp = t // batch
    nh = d // FX_HEAD
    ng = nh // heads
    wblk = heads * FX_HEAD
    first = lp % tile
    c_col = jnp.transpose(c.reshape(batch, lp, ng, heads), (0, 2, 1, 3))
    c_row = jnp.transpose(c, (0, 2, 1))[:, :, None, :]
    return pl.pallas_call(
        functools.partial(_fox_kernel, heads=heads, first=first, tile=tile),
        grid=(batch, ng),
        in_specs=[pl.BlockSpec((lp, wblk), lambda b, g: (b, g)),
                  pl.BlockSpec((lp, wblk), lambda b, g: (b, g)),
                  pl.BlockSpec((wblk, lp), lambda b, g: (g, b)),
                  pl.BlockSpec((1, 1, lp, heads), lambda b, g: (b, g, 0, 0)),
                  pl.BlockSpec((1, heads, 1, lp), lambda b, g: (b, g, 0, 0))],
        out_specs=pl.BlockSpec((lp, wblk), lambda b, g: (b, g)),
        out_shape=jax.ShapeDtypeStruct((t, d), BF16),
        name="fox_attention",
        compiler_params=_cparams(("parallel", "parallel")),
    )(q, k, vt, c_col, c_row)


def _pad_cols(w, n):
    return jnp.zeros((w.shape[0], n), w.dtype).at[:, :w.shape[1]].set(w)


def _pad_rows(w, n):
    return jnp.zeros((n, w.shape[1]), w.dtype).at[:w.shape[0], :].set(w)


def _rwkv7_mix(h, batch, mu, w_rkv, w_o, w0, w1, w2, a0, a1, a2, g1, g2, k_k, k_a, r_k, lnx_g, lnx_b,
               ln_g, ln_b, alpha):
    r1 = LANES
    lw = jnp.concatenate([_pad_cols(w1, r1), _pad_cols(a1, r1), g1], axis=1).astype(BF16)
    rkv, lora = _rwkv_in(h, mu, w_rkv.astype(BF16), lw, batch, r1)
    o = _wkv(rkv, lora, _pad_rows(w2, r1).astype(BF16), _pad_rows(a2, r1).astype(BF16), g2.astype(BF16),
             w0, a0, k_k, k_a, r_k.reshape(-1), lnx_g, lnx_b, batch, r1)
    return _matmul_ln(o, w_o.astype(BF16), h, ln_g, ln_b, alpha)


def kernel(x, meta_tokens, ln_g, ln_b, ffn_w1, ffn_w3, ffn_w2, rw_mu, rw_w_rkv, rw_w_o, rw_w0, rw_w1,
           rw_w2, rw_a0, rw_a1, rw_a2, rw_g1, rw_g2, rw_k_k, rw_k_a, rw_r_k, rw_lnx_g, rw_lnx_b,
           fx_w_q, fx_w_o, fx_w_kvf, fx_b_f):
    batch, seq, d = x.shape
    depth = ln_g.shape[0]
    n_a = rw_mu.shape[0]
    alpha = (2 * depth) ** 0.25
    lp = -(-(seq + N_META) // SEQ_ALIGN) * SEQ_ALIGN
    nh = d // FX_HEAD

    wb = (ffn_w1[0, 0].astype(BF16), ffn_w3[0, 0].astype(BF16), ffn_w2[0, 0].astype(BF16))

    def ffn(h, wb, l, s):
        first = l == 0 and s == 0
        last = l == depth - 1 and s == 1
        nxt = None if last else (ffn_w1, ffn_w3, ffn_w2, l + s, 1 - s)
        return _ffn_ln(h, *wb, ln_g[l, 2 * s], ln_b[l, 2 * s], alpha, lp, nxt=nxt,
                       meta=meta_tokens.astype(x.dtype) if first else None,
                       unpad=(N_META, seq) if last else None)

    h = x
    k_s = vt_s = c = None
    for l in range(depth):
        h, wb = ffn(h, wb, l, 0)
        if l < n_a:
            h = _rwkv7_mix(h, batch, rw_mu[l], rw_w_rkv[l], rw_w_o[l], rw_w0[l], rw_w1[l], rw_w2[l],
                           rw_a0[l], rw_a1[l], rw_a2[l], rw_g1[l], rw_g2[l], rw_k_k[l], rw_k_a[l],
                           rw_r_k[l], rw_lnx_g[l], rw_lnx_b[l], ln_g[l, 1], ln_b[l, 1], alpha)
        else:
            j = l - n_a
            q = _proj(h, fx_w_q[j].astype(BF16), d, "q_proj", scale=FOX_Q_SCALE)
            o = _fox_attention(q, k_s, vt_s, c, batch)
            h = _matmul_ln(o, fx_w_o[j].astype(BF16), h, ln_g[l, 1], ln_b[l, 1], alpha)
        h, wb = ffn(h, wb, l, 1)
        if l == n_a - 1:
            w_kvf = fx_w_kvf.astype(BF16)
            k_s = _proj(h, w_kvf, d, "k_proj")
            vt_s = _proj(h, w_kvf, d, "vt_proj", woff=d, transpose_out=True, tm=512)
            c = _fgate_cumsum(h, fx_w_kvf[:, 2 * d:], fx_b_f, batch)[:, :nh].reshape(batch, lp, nh)
    return h
```

```python
import functools
import math

import jax
import jax.numpy as jnp
from jax import lax
from jax.experimental import pallas as pl
from jax.experimental.pallas import tpu as pltpu

N_META = 16
RW_HEAD = 64
FX_HEAD = 128
LN_EPS = 1e-5
RW_GN_EPS = RW_HEAD * 1e-5
NEG_INF = -1e30
LOG2E = math.log2(math.e)
DECAY_SCALE = math.exp(-0.5)

LANES = 128
V7X_VMEM_BYTES = 64 * 1024 * 1024
VMEM_LIMIT = V7X_VMEM_BYTES * 7 // 8
SEQ_ALIGN = 128
WKV_CHUNK = 64
WKV_PAIRS_PER_STEP = 16
WKV_CHUNKS_PER_STEP = 2
FOX_HEADS_PER_STEP = 8

F32 = jnp.float32
BF16 = jnp.bfloat16


def _cparams(sem):
    return pltpu.CompilerParams(dimension_semantics=sem, vmem_limit_bytes=VMEM_LIMIT)


def _dot(a, b):
    return jnp.dot(a, b, preferred_element_type=F32)


def _dot_nt(a, b):
    return lax.dot_general(a, b, (((1,), (1,)), ((), ())), preferred_element_type=F32)


def _split(x):
    hi = x.astype(BF16)
    lo = (x - hi.astype(F32)).astype(BF16)
    return hi, lo


def _softplus(x):
    return jnp.maximum(x, 0.0) + jnp.log1p(jnp.exp(-jnp.abs(x)))


def _sigmoid(x):
    return 1.0 / (1.0 + jnp.exp(-x))


def _layer_norm(y, g, b):
    mu = jnp.mean(y, axis=-1, keepdims=True)
    yc = y - mu
    var = jnp.mean(yc * yc, axis=-1, keepdims=True)
    return yc * lax.rsqrt(var + LN_EPS) * g + b


def _proj_kernel(x_ref, w_ref, o_ref, xb_ref, *, scale, transpose_out):
    @pl.when(pl.program_id(1) == 0)
    def _():
        xb_ref[...] = x_ref[...].astype(BF16)

    acc = _dot(xb_ref[...], w_ref[...])
    if scale is not None:
        acc = acc * scale
    if transpose_out:
        acc = acc.T
    o_ref[...] = acc.astype(o_ref.dtype)


def _proj(x, w, n, name, woff=0, scale=None, transpose_out=False, tm=None, tn=1024):
    t, k = x.shape
    tm = tm or _pick_tm(t)
    jo = woff // tn
    if transpose_out:
        out_spec, out_shape = pl.BlockSpec((tn, tm), lambda i, j: (j, i)), (n, t)
    else:
        out_spec, out_shape = pl.BlockSpec((tm, tn), lambda i, j: (i, j)), (t, n)
    return pl.pallas_call(
        functools.partial(_proj_kernel, scale=scale, transpose_out=transpose_out),
        grid=(t // tm, n // tn),
        in_specs=[pl.BlockSpec((tm, k), lambda i, j: (i, 0)),
                  pl.BlockSpec((k, tn), lambda i, j: (0, j + jo))],
        out_specs=out_spec,
        out_shape=jax.ShapeDtypeStruct(out_shape, BF16),
        scratch_shapes=[pltpu.VMEM((tm, k), BF16)],
        name=name,
        compiler_params=_cparams(("parallel", "arbitrary")),
    )(x, w)


def _pick_tm(t, cap=1088):
    for tm in range(min(cap, t), 0, -16):
        if t % tm == 0:
            return tm
    return t


PREV_ROWS = 8


def _rwkv_in_kernel(h_ref, prev_ref, mu_ref, w_ref, lw_ref, rkv_ref, lora_ref, xs_ref, *, lp, nj, r1):
    i = pl.program_id(0)
    j = pl.program_id(1)
    tm = h_ref.shape[0]

    @pl.when(j == 0)
    def _():
        h = h_ref[...]
        last = prev_ref[PREV_ROWS - 1:, :]
        last = jnp.where(lax.rem(i * tm, lp) == 0, 0.0, last)
        row = lax.broadcasted_iota(jnp.int32, h.shape, 0)
        xx = jnp.where(row == 0, last, pltpu.roll(h, 1, 0)) - h

        def mix(n):
            return (h + xx * mu_ref[n:n + 1, :]).astype(BF16)

        xs_ref[0] = mix(0)
        xs_ref[1] = mix(2)
        xs_ref[2] = mix(3)
        tw = jnp.tanh(_dot(mix(1), lw_ref[:, :r1]))
        ta = _dot(mix(4), lw_ref[:, r1:2 * r1])
        tg = _sigmoid(_dot(mix(5), lw_ref[:, 2 * r1:]))
        lora_ref[...] = jnp.concatenate([tw, ta, tg], axis=1).astype(BF16)

    rkv_ref[...] = _dot(xs_ref[j // nj], w_ref[...])


def _rwkv_in(h, mu, w_rkv, lw, batch, r1, tm=None, tn=512):
    t, d = h.shape
    lp = t // batch
    tm = tm or _pick_tm(lp, 1088)
    nj = d // tn
    nl = lw.shape[1]
    pblk = tm // PREV_ROWS
    return pl.pallas_call(
        functools.partial(_rwkv_in_kernel, lp=lp, nj=nj, r1=r1),
        grid=(t // tm, 3 * nj),
        in_specs=[pl.BlockSpec((tm, d), lambda i, j: (i, 0)),
                  pl.BlockSpec((PREV_ROWS, d), lambda i, j: (jnp.maximum(i * pblk - 1, 0), 0)),
                  pl.BlockSpec((6, d), lambda i, j: (0, 0)),
                  pl.BlockSpec((None, d, tn), lambda i, j: (j // nj, 0, j % nj)),
                  pl.BlockSpec((d, nl), lambda i, j: (0, 0))],
        out_specs=[pl.BlockSpec((None, tm, tn), lambda i, j: (j // nj, i, j % nj)),
                   pl.BlockSpec((tm, nl), lambda i, j: (i, 0))],
        out_shape=[jax.ShapeDtypeStruct((3, t, d), F32), jax.ShapeDtypeStruct((t, nl), BF16)],
        scratch_shapes=[pltpu.VMEM((3, tm, d), BF16)],
        name="rwkv_in",
        compiler_params=_cparams(("parallel", "arbitrary")),
    )(h, h, mu, w_rkv, lw)


def _mm_ln_kernel(x_ref, w_ref, h_ref, g_ref, b_ref, o_ref, *, alpha, splits):
    rs = x_ref.shape[0] // splits
    rows = [slice(n * rs, (n + 1) * rs) for n in range(splits)]
    ys = [alpha * h_ref[r, :] + _dot(x_ref[r, :], w_ref[...]) for r in rows]
    for r, y in zip(rows, ys):
        o_ref[r, :] = _layer_norm(y, g_ref[...], b_ref[...])


def _matmul_ln(x, w, h, g, b, alpha, tm=None):
    t, k = x.shape
    n = w.shape[1]
    tm = tm or _pick_tm(t, 512)
    return pl.pallas_call(
        functools.partial(_mm_ln_kernel, alpha=alpha, splits=4 if tm % 64 == 0 else 1),
        grid=(t // tm,),
        in_specs=[pl.BlockSpec((tm, k), lambda i: (i, 0)),
                  pl.BlockSpec((k, n), lambda i: (0, 0)),
                  pl.BlockSpec((tm, n), lambda i: (i, 0)),
                  pl.BlockSpec((1, n), lambda i: (0, 0)),
                  pl.BlockSpec((1, n), lambda i: (0, 0))],
        out_specs=pl.BlockSpec((tm, n), lambda i: (i, 0)),
        out_shape=jax.ShapeDtypeStruct((t, n), F32),
        name="proj_ln",
        compiler_params=_cparams(("parallel",)),
    )(x, w, h, g.reshape(1, n), b.reshape(1, n))


def _ffn_ln_kernel(*refs, alpha, has_next, head, tail):
    refs = list(refs)
    h_ref = refs.pop(0)
    meta_ref = refs.pop(0) if head else None
    w1_ref, w3_ref, w2_ref, g_ref, b_ref = refs[:5]
    nxt_refs = refs[5:8] if has_next else ()
    o_ref = refs[8 if has_next else 5]
    cast_refs = refs[9:12] if has_next else ()
    hb_ref = refs[12 if has_next else 6]
    buf_ref, sem = refs[-2:] if head or tail else (None, None)
    i = pl.program_id(0)
    j = pl.program_id(1)
    ni = pl.num_programs(0)
    tm = hb_ref.shape[0]
    slot = lax.rem(i, 2)
    acc_ref = buf_ref.at[slot] if tail else o_ref

    def tile_rows(c):
        n_meta, seq, _ = head or tail
        lo, hi = max(c * tm - n_meta, 0), min((c + 1) * tm - n_meta, seq)
        return lo, lo + n_meta - c * tm, hi - lo

    def tile_dma(tile, buf, act):
        tiles = (head or tail)[2]
        for c in range(tiles):
            lo, dst, n = tile_rows(c)

            @pl.when(lax.rem(tile, tiles) == c)
            def _():
                hbm = (h_ref if head else o_ref).at[tile // tiles, pl.ds(lo, n), :]
                vmem = buf_ref.at[buf, pl.ds(dst, n), :]
                cp = pltpu.make_async_copy(*((hbm, vmem) if head else (vmem, hbm)), sem.at[buf])
                cp.start() if act == "start" else cp.wait()

    def fill_tile():
        n_meta, _, tiles = head
        hf_ref = buf_ref
        for c in range(tiles):
            _, dst, n = tile_rows(c)

            @pl.when(lax.rem(i, tiles) == c)
            def _():
                if dst > 0:
                    hf_ref[slot, 0:dst, :] = meta_ref[n_meta - dst:, :]
                if dst + n < tm:
                    hf_ref[slot, dst + n:, :] = jnp.zeros((tm - dst - n, hf_ref.shape[2]), F32)

    def load_h(rows):
        return h_ref[rows, :] if not head else buf_ref[slot, rows, :]

    @pl.when(j == 0)
    def _():
        if head:
            @pl.when(i == 0)
            def _():
                tile_dma(i, slot, "start")

            tile_dma(i, slot, "wait")
            fill_tile()

            @pl.when(i + 1 < ni)
            def _():
                tile_dma(i + 1, 1 - slot, "start")
        if tail:
            @pl.when(i >= 2)
            def _():
                tile_dma(i - 2, slot, "wait")

        hb_ref[...] = load_h(slice(0, tm)).astype(BF16)
        acc_ref[...] = jnp.zeros(acc_ref.shape, F32)

    def cast_next_weights():
        for src, dst in zip(nxt_refs, cast_refs):
            dst[...] = src[...].astype(BF16)

    def partial_out(rows):
        hb = hb_ref[rows, :]
        u = _dot(hb, w1_ref[...])
        v = _dot(hb, w3_ref[...])
        return _dot((u * _sigmoid(u) * v).astype(BF16), w2_ref[...])

    last = pl.num_programs(1) - 1

    @pl.when(j < last)
    def _():
        cast_next_weights()
        acc_ref[...] += partial_out(slice(0, tm))

    @pl.when(j == last)
    def _():
        cast_next_weights()
        for rows in (slice(0, tm // 2), slice(tm // 2, tm)):
            y = alpha * load_h(rows) + 0.5 * (acc_ref[rows, :] + partial_out(rows))
            acc_ref[rows, :] = _layer_norm(y, g_ref[...], b_ref[...])
        if tail:
            tile_dma(i, slot, "start")

            @pl.when(i == ni - 1)
            def _():
                @pl.when(i >= 1)
                def _():
                    tile_dma(i - 1, 1 - slot, "wait")

                tile_dma(i, slot, "wait")


def _ffn_ln(h, w1, w3, w2, g, b, alpha, lp, nxt=None, meta=None, unpad=None, tm=None, tf=512):
    d = h.shape[-1]
    f = w1.shape[-1]
    batch = h.shape[0] if meta is not None else h.shape[0] // lp
    t = batch * lp
    tm = tm or _pick_tm(lp, 544)
    ni, nj = t // tm, f // tf
    head = tail = None
    scratch = [pltpu.VMEM((tm, d), BF16)]
    if meta is None:
        in_specs, args = [pl.BlockSpec((tm, d), lambda i, j: (i, 0))], [h]
    else:
        head = (meta.shape[0], h.shape[1], lp // tm)
        in_specs = [pl.BlockSpec(memory_space=pl.ANY), pl.BlockSpec(meta.shape, lambda i, j: (0, 0))]
        args = [h, meta]
    in_specs += [pl.BlockSpec((d, tf), lambda i, j: (0, j)),
                 pl.BlockSpec((d, tf), lambda i, j: (0, j)),
                 pl.BlockSpec((tf, d), lambda i, j: (j, 0)),
                 pl.BlockSpec((1, d), lambda i, j: (0, 0)),
                 pl.BlockSpec((1, d), lambda i, j: (0, 0))]
    args += [w1, w3, w2, g.reshape(1, d), b.reshape(1, d)]
    if unpad is None:
        out_specs = [pl.BlockSpec((tm, d), lambda i, j: (i, 0))]
        out_shape = [jax.ShapeDtypeStruct((t, d), F32)]
    else:
        assert meta is None
        tail = (*unpad, lp // tm)
        out_specs = [pl.BlockSpec(memory_space=pl.ANY)]
        out_shape = [jax.ShapeDtypeStruct((batch, unpad[1], d), F32)]
    if head or tail:
        scratch += [pltpu.VMEM((2, tm, d), F32), pltpu.SemaphoreType.DMA((2,))]
    if nxt is not None:
        n1, n3, n2, l, s = nxt
        dr = d // ni
        assert dr * ni == d and dr % LANES == 0
        in_specs += [pl.BlockSpec((None, None, dr, tf), lambda i, j: (l, s, i, j)),
                     pl.BlockSpec((None, None, dr, tf), lambda i, j: (l, s, i, j)),
                     pl.BlockSpec((None, None, tf, dr), lambda i, j: (l, s, j, i))]
        out_specs += [pl.BlockSpec((dr, tf), lambda i, j: (i, j)),
                      pl.BlockSpec((dr, tf), lambda i, j: (i, j)),
                      pl.BlockSpec((tf, dr), lambda i, j: (j, i))]
        out_shape += [jax.ShapeDtypeStruct((d, f), BF16), jax.ShapeDtypeStruct((d, f), BF16),
                      jax.ShapeDtypeStruct((f, d), BF16)]
        args += [n1, n3, n2]
    out = pl.pallas_call(
        functools.partial(_ffn_ln_kernel, alpha=alpha, has_next=nxt is not None, head=head, tail=tail),
        grid=(ni, nj),
        in_specs=in_specs,
        out_specs=out_specs,
        out_shape=out_shape,
        scratch_shapes=scratch,
        name="ffn_ln",
        compiler_params=_cparams(("arbitrary" if head or tail else "parallel", "arbitrary")),
    )(*args)
    return out[0], tuple(out[1:])


def _cumsum_rows(x):
    n = x.shape[0]
    row = lax.broadcasted_iota(jnp.int32, x.shape, 0)
    s = 1
    while s < n:
        x = x + jnp.where(row >= s, pltpu.roll(x, s, 0), 0.0)
        s *= 2
    return x


def _mm1(a, b):
    return _dot(a.astype(BF16), b.astype(BF16))


def _wkv_chunk(tiles, params, states, c):
    m0, strict, incl, ones_bd = c
    cc = tiles[0][0].shape[0]
    c2 = 2 * cc
    n = len(tiles)
    inv_n = 1.0 / RW_HEAD

    def stack2(x):
        return jnp.concatenate([jnp.where(m0, x, 0.0), jnp.where(m0, 0.0, x)], axis=0)

    def segsum(xs):
        return [_dot(x.astype(BF16), ones_bd) for x in xs]

    kk0 = [t[2] * p[0] for t, p in zip(tiles, params)]
    nrm2 = segsum([x * x for x in kk0])
    kmod = [t[2] * (1.0 + (t[4] - 1.0) * p[1]) for t, p in zip(tiles, params)]
    bon = segsum([t[0] * km * p[2] for t, km, p in zip(tiles, kmod, params)])

    lhs, rhs, vs, upd_r, dec = [], [], [], [], []
    for i in range(n):
        r, lw, k, v, a, g = tiles[i]
        kk = kk0[i] / jnp.maximum(jnp.sqrt(nrm2[i]), 1e-12)
        b = kk * a
        lcum = _cumsum_rows(lw)
        ltot = lcum[cc - 1:cc, :]
        e_neg = jnp.exp(-lcum)
        e_rem = jnp.exp(ltot - lcum)
        kt = kk * jnp.exp(lcum - lw)
        rt = r * jnp.exp(lcum)
        khb = (kmod[i] * e_neg).astype(BF16)
        bhb = (b * e_neg).astype(BF16)
        lhs.append(jnp.concatenate([stack2(kt), stack2(rt)], axis=0).astype(BF16))
        rhs.append(jnp.concatenate([khb, khb, bhb, bhb, states[i].astype(BF16)], axis=0))
        vs.append(stack2(v))
        upd_r.append(jnp.concatenate([stack2(kmod[i] * e_rem), stack2(b * e_rem)], axis=0).astype(BF16))
        dec.append(jnp.exp(ltot))

    m = [_dot_nt(a_, b_) for a_, b_ in zip(lhs, rhs)]
    a_k = [jnp.where(strict, x[:c2, :c2], 0.0) for x in m]
    a_b = [jnp.where(strict, x[:c2, c2:2 * c2], 0.0) for x in m]
    bkb = [jnp.concatenate([jnp.where(incl, x[c2:, :c2], 0.0), jnp.where(incl, -x[c2:, c2:2 * c2], 0.0)],
                           axis=1).astype(BF16) for x in m]
    rhs_u = [x[:c2, 2 * c2:] + _mm1(ak, v_) for x, ak, v_ in zip(m, a_k, vs)]
    pw = [ab.astype(BF16) for ab in a_b]
    st = [_dot(p_, jnp.concatenate([p_, ru.astype(BF16)], axis=1)) for p_, ru in zip(pw, rhs_u)]
    pw = [s_[:, :c2].astype(BF16) for s_ in st]
    us = [ru - s_[:, c2:] for ru, s_ in zip(rhs_u, st)]
    lvl = 4
    while lvl < cc:
        st = [_dot(p_, jnp.concatenate([p_, u_.astype(BF16)], axis=1)) for p_, u_ in zip(pw, us)]
        pw = [s_[:, :c2].astype(BF16) for s_ in st]
        us = [u_ + s_[:, c2:] for u_, s_ in zip(us, st)]
        lvl *= 2
    us = [u_ + _dot(p_, u_.astype(BF16)) for u_, p_ in zip(us, pw)]
    ys = [x[c2:, 2 * c2:] + _dot(bk, jnp.concatenate([v_, u_], axis=0).astype(BF16))
          for x, bk, v_, u_ in zip(m, bkb, vs, us)]
    y = [x[:cc] + x[cc:] for x in ys]
    upd_l = [jnp.concatenate([v_, -u_], axis=0).T.astype(BF16) for v_, u_ in zip(vs, us)]
    s_new = [s0 * d_ + _dot(ul, ur) for s0, d_, ul, ur in zip(states, dec, upd_l, upd_r)]

    mu = [x * inv_n for x in segsum(y)]
    yc = [a_ - b_ for a_, b_ in zip(y, mu)]
    var = [x * inv_n for x in segsum([x * x for x in yc])]
    outs = []
    for i in range(n):
        yn = yc[i] * lax.rsqrt(var[i] + RW_GN_EPS) * params[i][3] + params[i][4]
        outs.append((yn + bon[i] * tiles[i][3]) * tiles[i][5])
    return outs, s_new


def _wkv_kernel(r_ref, k_ref, v_ref, lora_ref, w2_ref, a2_ref, g2_ref, w0_ref, a0_ref,
                kk_ref, ka_ref, rk_ref, lg_ref, lb_ref, o_ref, s_ref, *, pairs, r1):
    @pl.when(pl.program_id(2) == 0)
    def _():
        s_ref[...] = jnp.zeros_like(s_ref)

    lo = lora_ref[...]

    cc = WKV_CHUNK
    c2 = 2 * cc
    lane = lax.broadcasted_iota(jnp.int32, (cc, LANES), 1)
    m0 = lane < RW_HEAD
    row = lax.broadcasted_iota(jnp.int32, (c2, c2), 0)
    col = lax.broadcasted_iota(jnp.int32, (c2, c2), 1)
    same = (row >= cc) == (col >= cc)
    strict = same & (col < row)
    incl = same & (col <= row)
    lr = lax.broadcasted_iota(jnp.int32, (LANES, LANES), 0)
    lc = lax.broadcasted_iota(jnp.int32, (LANES, LANES), 1)
    ones_bd = jnp.where((lr >= RW_HEAD) == (lc >= RW_HEAD), 1.0, 0.0).astype(BF16)
    consts = (m0, strict, incl, ones_bd)

    lw_all = -DECAY_SCALE * _sigmoid(_dot(lo[:, :r1], w2_ref[...]) + w0_ref[...])
    a_all = _sigmoid(_dot(lo[:, r1:2 * r1], a2_ref[...]) + a0_ref[...])
    g_all = _dot(lo[:, 2 * r1:], g2_ref[...])

    sls = [slice(p * LANES, (p + 1) * LANES) for p in range(pairs)]
    params = [tuple(ref[:, sl] for ref in (kk_ref, ka_ref, rk_ref, lg_ref, lb_ref)) for sl in sls]
    states = [s_ref[p] for p in range(pairs)]
    for c0 in range(0, r_ref.shape[0], cc):
        rows = slice(c0, c0 + cc)
        tiles = [(r_ref[rows, sl], lw_all[rows, sl], k_ref[rows, sl], v_ref[rows, sl], a_all[rows, sl],
                  g_all[rows, sl]) for sl in sls]
        outs, states = _wkv_chunk(tiles, params, states, consts)
        for sl, out in zip(sls, outs):
            o_ref[rows, sl] = out.astype(o_ref.dtype)
    for p in range(pairs):
        s_ref[p] = states[p]


def _wkv(rkv, lora, w2, a2, g2, w0, a0, kkp, kap, rkp, lgp, lbp, batch, r1, out_dtype=BF16):
    _, t, d = rkv.shape
    nl = lora.shape[1]
    lp = t // batch
    cc = WKV_CHUNK * (WKV_CHUNKS_PER_STEP if lp % (WKV_CHUNK * WKV_CHUNKS_PER_STEP) == 0 else 1)
    nchunk = lp // cc
    pairs = min(WKV_PAIRS_PER_STEP, d // LANES)
    wblk = pairs * LANES
    tok = pl.BlockSpec((cc, wblk), lambda b, p, c: (b * nchunk + c, p))
    r_spec, k_spec, v_spec = [pl.BlockSpec((None, cc, wblk), lambda b, p, c, n=n: (n, b * nchunk + c, p))
                              for n in range(3)]
    lo_spec = pl.BlockSpec((cc, nl), lambda b, p, c: (b * nchunk + c, 0))
    up = [pl.BlockSpec((w.shape[0], wblk), lambda b, p, c: (0, p)) for w in (w2, a2, g2)]
    par = pl.BlockSpec((1, wblk), lambda b, p, c: (0, p))
    prm = [x.reshape(1, d) for x in (w0, a0, kkp, kap, rkp, lgp, lbp)]
    return pl.pallas_call(
        functools.partial(_wkv_kernel, pairs=pairs, r1=r1),
        grid=(batch, d // wblk, nchunk),
        in_specs=[r_spec, k_spec, v_spec, lo_spec] + up + [par] * 7,
        out_specs=tok,
        out_shape=jax.ShapeDtypeStruct((t, d), out_dtype),
        scratch_shapes=[pltpu.VMEM((pairs, LANES, LANES), F32)],
        name="wkv7",
        compiler_params=_cparams(("parallel", "parallel", "arbitrary")),
    )(rkv, rkv, rkv, lora, w2, a2, g2, *prm)


def _fgate_kernel(h_ref, wh_ref, wl_ref, b_ref, o_ref, carry_ref):
    @pl.when(pl.program_id(1) == 0)
    def _():
        carry_ref[...] = jnp.zeros_like(carry_ref)

    hh, hl = _split(h_ref[...])
    z = _dot(hh, wh_ref[...]) + (_dot(hh, wl_ref[...]) + _dot(hl, wh_ref[...])) + b_ref[...]
    logf = -_softplus(-z) * LOG2E
    cs = _cumsum_rows(logf) + carry_ref[...]
    o_ref[...] = cs
    carry_ref[...] = cs[cs.shape[0] - 1:, :]


def _fgate_cumsum(h, wf, bf, batch, tm=544):
    t, d = h.shape
    lp = t // batch
    nt = lp // tm
    nh = wf.shape[1]
    wpad = jnp.zeros((d, LANES), F32).at[:, :nh].set(wf)
    bpad = jnp.zeros((1, LANES), F32).at[0, :nh].set(bf)
    wh, wl = _split(wpad)
    return pl.pallas_call(
        _fgate_kernel,
        grid=(batch, nt),
        in_specs=[pl.BlockSpec((tm, d), lambda b, i: (b * nt + i, 0)),
                  pl.BlockSpec((d, LANES), lambda b, i: (0, 0)),
                  pl.BlockSpec((d, LANES), lambda b, i: (0, 0)),
                  pl.BlockSpec((1, LANES), lambda b, i: (0, 0))],
        out_specs=pl.BlockSpec((tm, LANES), lambda b, i: (b * nt + i, 0)),
        out_shape=jax.ShapeDtypeStruct((t, LANES), F32),
        scratch_shapes=[pltpu.VMEM((1, LANES), F32)],
        name="fgate_cumsum",
        compiler_params=_cparams(("parallel", "arbitrary")),
    )(h, wh, wl, bpad)


def _fox_kernel(q_ref, k_ref, vt_ref, cc_ref, cr_ref, o_ref, *, heads, first, tile):
    lp = q_ref.shape[0]
    nbig = (lp - first) // tile
    hs = [slice(h * FX_HEAD, (h + 1) * FX_HEAD) for h in range(heads)]

    def scores(q, kstart, ksize):
        return [_dot_nt(k_ref[pl.ds(kstart, ksize), hs[h]], q[h]) for h in range(heads)]

    def update(s, cq, carry, kstart, ksize, diag):
        s = [s[h] + cq[h] - cc_ref[0, 0, pl.ds(kstart, ksize), h:h + 1] for h in range(heads)]
        if diag:
            keep = (lax.broadcasted_iota(jnp.int32, (ksize, ksize), 0)
                    <= lax.broadcasted_iota(jnp.int32, (ksize, ksize), 1))
            s = [jnp.where(keep, x, NEG_INF) for x in s]
        m_new = [jnp.maximum(carry[h][0], jnp.max(s[h], axis=0, keepdims=True)) for h in range(heads)]
        alpha = [jnp.exp2(carry[h][0] - m_new[h]) for h in range(heads)]
        p = [jnp.exp2(s[h] - m_new[h]) for h in range(heads)]
        l_new = [alpha[h] * carry[h][1] + jnp.sum(p[h], axis=0, keepdims=True) for h in range(heads)]
        pv = [_dot(vt_ref[hs[h], pl.ds(kstart, ksize)], p[h].astype(BF16)) for h in range(heads)]
        return [(m_new[h], l_new[h], alpha[h] * carry[h][2] + pv[h]) for h in range(heads)]

    def q_tile(qstart, tq, nfull, with_first):
        q = [q_ref[pl.ds(qstart, tq), hs[h]] for h in range(heads)]
        cq = [cr_ref[0, h, :, pl.ds(qstart, tq)] for h in range(heads)]
        carry = [(jnp.full((1, tq), NEG_INF, F32), jnp.zeros((1, tq), F32), jnp.zeros((FX_HEAD, tq), F32))
                 for _ in range(heads)]
        def step(c, kstart, ksize, diag):
            return update(scores(q, kstart, ksize), cq, c, kstart, ksize, diag)

        def step2(j, c):
            k0 = pl.multiple_of(first + 2 * j * tile, LANES)
            k1 = pl.multiple_of(first + (2 * j + 1) * tile, LANES)
            s0, s1 = scores(q, k0, tile), scores(q, k1, tile)
            return update(s1, cq, update(s0, cq, c, k0, tile, False), k1, tile, False)

        if nfull is not None:
            carry = lax.fori_loop(0, lax.shift_right_logical(nfull, 1), step2, carry)
            carry = lax.fori_loop(
                0, nfull & 1,
                lambda _, c: step(c, pl.multiple_of(first + (nfull - 1) * tile, LANES), tile, False), carry)
        s_first = scores(q, 0, first) if with_first else None
        s_diag = scores(q, qstart, tq)
        if with_first:
            carry = update(s_first, cq, carry, 0, first, False)
        carry = update(s_diag, cq, carry, qstart, tq, True)
        for h in range(heads):
            o_ref[pl.ds(qstart, tq), hs[h]] = (carry[h][2] / carry[h][1]).T.astype(o_ref.dtype)

    if first:
        q_tile(0, first, None, False)

    def big(i, _):
        q_tile(pl.multiple_of(first + i * tile, LANES), tile, i, first > 0)
        return 0

    lax.fori_loop(0, nbig, big, 0)


FOX_Q_SCALE = FX_HEAD ** -0.5 * LOG2E


def _fox_attention(q, k, vt, c, batch, heads=FOX_HEADS_PER_STEP, tile=256):
    t, d = q.shape
    lp = t // batch
    nh = d // FX_HEAD
    ng = nh // heads
    wblk = heads * FX_HEAD
    first = lp % tile
    c_col = jnp.transpose(c.reshape(batch, lp, ng, heads), (0, 2, 1, 3))
    c_row = jnp.transpose(c, (0, 2, 1))[:, :, None, :]
    return pl.pallas_call(
        functools.partial(_fox_kernel, heads=heads, first=first, tile=tile),
        grid=(batch, ng),
        in_specs=[pl.BlockSpec((lp, wblk), lambda b, g: (b, g)),
                  pl.BlockSpec((lp, wblk), lambda b, g: (b, g)),
                  pl.BlockSpec((wblk, lp), lambda b, g: (g, b)),
                  pl.BlockSpec((1, 1, lp, heads), lambda b, g: (b, g, 0, 0)),
                  pl.BlockSpec((1, heads, 1, lp), lambda b, g: (b, g, 0, 0))],
        out_specs=pl.BlockSpec((lp, wblk), lambda b, g: (b, g)),
        out_shape=jax.ShapeDtypeStruct((t, d), BF16),
        name="fox_attention",
        compiler_params=_cparams(("parallel", "parallel")),
    )(q, k, vt, c_col, c_row)


def _pad_cols(w, n):
    return jnp.zeros((w.shape[0], n), w.dtype).at[:, :w.shape[1]].set(w)


def _pad_rows(w, n):
    return jnp.zeros((n, w.shape[1]), w.dtype).at[:w.shape[0], :].set(w)


def _rwkv7_mix(h, batch, mu, w_rkv, w_o, w0, w1, w2, a0, a1, a2, g1, g2, k_k, k_a, r_k, lnx_g, lnx_b,
               ln_g, ln_b, alpha):
    r1 = LANES
    lw = jnp.concatenate([_pad_cols(w1, r1), _pad_cols(a1, r1), g1], axis=1).astype(BF16)
    rkv, lora = _rwkv_in(h, mu, w_rkv.astype(BF16), lw, batch, r1)
    o = _wkv(rkv, lora, _pad_rows(w2, r1).astype(BF16), _pad_rows(a2, r1).astype(BF16), g2.astype(BF16),
             w0, a0, k_k, k_a, r_k.reshape(-1), lnx_g, lnx_b, batch, r1)
    return _matmul_ln(o, w_o.astype(BF16), h, ln_g, ln_b, alpha)


def kernel(x, meta_tokens, ln_g, ln_b, ffn_w1, ffn_w3, ffn_w2, rw_mu, rw_w_rkv, rw_w_o, rw_w0, rw_w1,
           rw_w2, rw_a0, rw_a1, rw_a2, rw_g1, rw_g2, rw_k_k, rw_k_a, rw_r_k, rw_lnx_g, rw_lnx_b,
           fx_w_q, fx_w_o, fx_w_kvf, fx_b_f):
    batch, seq, d = x.shape
    depth = ln_g.shape[0]
    n_a = rw_mu.shape[0]
    alpha = (2 * depth) ** 0.25
    lp = -(-(seq + N_META) // SEQ_ALIGN) * SEQ_ALIGN
    nh = d // FX_HEAD

    wb = (ffn_w1[0, 0].astype(BF16), ffn_w3[0, 0].astype(BF16), ffn_w2[0, 0].astype(BF16))

    def ffn(h, wb, l, s):
        first = l == 0 and s == 0
        last = l == depth - 1 and s == 1
        nxt = None if last else (ffn_w1, ffn_w3, ffn_w2, l + s, 1 - s)
        return _ffn_ln(h, *wb, ln_g[l, 2 * s], ln_b[l, 2 * s], alpha, lp, nxt=nxt,
                       meta=meta_tokens.astype(x.dtype) if first else None,
                       unpad=(N_META, seq) if last else None)

    h = x
    k_s = vt_s = c = None
    for l in range(depth):
        h, wb = ffn(h, wb, l, 0)
        if l < n_a:
            h = _rwkv7_mix(h, batch, rw_mu[l], rw_w_rkv[l], rw_w_o[l], rw_w0[l], rw_w1[l], rw_w2[l],
                           rw_a0[l], rw_a1[l], rw_a2[l], rw_g1[l], rw_g2[l], rw_k_k[l], rw_k_a[l],
                           rw_r_k[l], rw_lnx_g[l], rw_lnx_b[l], ln_g[l, 1], ln_b[l, 1], alpha)
        else:
            j = l - n_a
            q = _proj(h, fx_w_q[j].astype(BF16), d, "q_proj", scale=FOX_Q_SCALE)
            o = _fox_attention(q, k_s, vt_s, c, batch)
            h = _matmul_ln(o, fx_w_o[j].astype(BF16), h, ln_g[l, 1], ln_b[l, 1], alpha)
        h, wb = ffn(h, wb, l, 1)
        if l == n_a - 1:
            w_kvf = fx_w_kvf.astype(BF16)
            k_s = _proj(h, w_kvf, d, "k_proj")
            vt_s = _proj(h, w_kvf, d, "vt_proj", woff=d, transpose_out=True, tm=512)
            c = _fgate_cumsum(h, fx_w_kvf[:, 2 * d:], fx_b_f, batch)[:, :nh].reshape(batch, lp, nh)
    return h
```

```python
import functools
import math

import jax
import jax.numpy as jnp
from jax import lax
from jax.experimental import pallas as pl
from jax.experimental.pallas import tpu as pltpu

N_META = 16
RW_HEAD = 64
FX_HEAD = 128
LN_EPS = 1e-5
RW_GN_EPS = RW_HEAD * 1e-5
NEG_INF = -1e30
LOG2E = math.log2(math.e)
DECAY_SCALE = math.exp(-0.5)

LANES = 128
V7X_VMEM_BYTES = 64 * 1024 * 1024
VMEM_LIMIT = V7X_VMEM_BYTES * 7 // 8
SEQ_ALIGN = 128
WKV_CHUNK = 64
WKV_PAIRS_PER_STEP = 16
WKV_CHUNKS_PER_STEP = 2
FOX_HEADS_PER_STEP = 8

F32 = jnp.float32
BF16 = jnp.bfloat16


def _cparams(sem):
    return pltpu.CompilerParams(dimension_semantics=sem, vmem_limit_bytes=VMEM_LIMIT)


def _dot(a, b):
    return jnp.dot(a, b, preferred_element_type=F32)


def _dot_nt(a, b):
    return lax.dot_general(a, b, (((1,), (1,)), ((), ())), preferred_element_type=F32)


def _split(x):
    hi = x.astype(BF16)
    lo = (x - hi.astype(F32)).astype(BF16)
    return hi, lo


def _softplus(x):
    return jnp.maximum(x, 0.0) + jnp.log1p(jnp.exp(-jnp.abs(x)))


def _sigmoid(x):
    return 1.0 / (1.0 + jnp.exp(-x))


def _layer_norm(y, g, b):
    mu = jnp.mean(y, axis=-1, keepdims=True)
    yc = y - mu
    var = jnp.mean(yc * yc, axis=-1, keepdims=True)
    return yc * lax.rsqrt(var + LN_EPS) * g + b


def _proj_kernel(x_ref, w_ref, o_ref, xb_ref, *, scale, transpose_out):
    @pl.when(pl.program_id(1) == 0)
    def _():
        xb_ref[...] = x_ref[...].astype(BF16)

    acc = _dot(xb_ref[...], w_ref[...])
    if scale is not None:
        acc = acc * scale
    if transpose_out:
        acc = acc.T
    o_ref[...] = acc.astype(o_ref.dtype)


def _proj(x, w, n, name, woff=0, scale=None, transpose_out=False, tm=None, tn=1024):
    t, k = x.shape
    tm = tm or _pick_tm(t)
    jo = woff // tn
    if transpose_out:
        out_spec, out_shape = pl.BlockSpec((tn, tm), lambda i, j: (j, i)), (n, t)
    else:
        out_spec, out_shape = pl.BlockSpec((tm, tn), lambda i, j: (i, j)), (t, n)
    return pl.pallas_call(
        functools.partial(_proj_kernel, scale=scale, transpose_out=transpose_out),
        grid=(t // tm, n // tn),
        in_specs=[pl.BlockSpec((tm, k), lambda i, j: (i, 0)),
                  pl.BlockSpec((k, tn), lambda i, j: (0, j + jo))],
        out_specs=out_spec,
        out_shape=jax.ShapeDtypeStruct(out_shape, BF16),
        scratch_shapes=[pltpu.VMEM((tm, k), BF16)],
        name=name,
        compiler_params=_cparams(("parallel", "arbitrary")),
    )(x, w)


def _pick_tm(t, cap=1088):
    for tm in range(min(cap, t), 0, -16):
        if t % tm == 0:
            return tm
    return t


PREV_ROWS = 8


def _rwkv_in_kernel(h_ref, prev_ref, mu_ref, w_ref, lw_ref, rkv_ref, lora_ref, xs_ref, *, lp, nj, r1):
    i = pl.program_id(0)
    j = pl.program_id(1)
    tm = h_ref.shape[0]

    @pl.when(j == 0)
    def _():
        h = h_ref[...]
        last = prev_ref[PREV_ROWS - 1:, :]
        last = jnp.where(lax.rem(i * tm, lp) == 0, 0.0, last)
        row = lax.broadcasted_iota(jnp.int32, h.shape, 0)
        xx = jnp.where(row == 0, last, pltpu.roll(h, 1, 0)) - h

        def mix(n):
            return (h + xx * mu_ref[n:n + 1, :]).astype(BF16)

        xs_ref[0] = mix(0)
        xs_ref[1] = mix(2)
        xs_ref[2] = mix(3)
        tw = jnp.tanh(_dot(mix(1), lw_ref[:, :r1]))
        ta = _dot(mix(4), lw_ref[:, r1:2 * r1])
        tg = _sigmoid(_dot(mix(5), lw_ref[:, 2 * r1:]))
        lora_ref[...] = jnp.concatenate([tw, ta, tg], axis=1).astype(BF16)

    rkv_ref[...] = _dot(xs_ref[j // nj], w_ref[...])


def _rwkv_in(h, mu, w_rkv, lw, batch, r1, tm=None, tn=512):
    t, d = h.shape
    lp = t // batch
    tm = tm or _pick_tm(lp, 1088)
    nj = d // tn
    nl = lw.shape[1]
    pblk = tm // PREV_ROWS
    return pl.pallas_call(
        functools.partial(_rwkv_in_kernel, lp=lp, nj=nj, r1=r1),
        grid=(t // tm, 3 * nj),
        in_specs=[pl.BlockSpec((tm, d), lambda i, j: (i, 0)),
                  pl.BlockSpec((PREV_ROWS, d), lambda i, j: (jnp.maximum(i * pblk - 1, 0), 0)),
                  pl.BlockSpec((6, d), lambda i, j: (0, 0)),
                  pl.BlockSpec((None, d, tn), lambda i, j: (j // nj, 0, j % nj)),
                  pl.BlockSpec((d, nl), lambda i, j: (0, 0))],
        out_specs=[pl.BlockSpec((None, tm, tn), lambda i, j: (j // nj, i, j % nj)),
                   pl.BlockSpec((tm, nl), lambda i, j: (i, 0))],
        out_shape=[jax.ShapeDtypeStruct((3, t, d), F32), jax.ShapeDtypeStruct((t, nl), BF16)],
        scratch_shapes=[pltpu.VMEM((3, tm, d), BF16)],
        name="rwkv_in",
        compiler_params=_cparams(("parallel", "arbitrary")),
    )(h, h, mu, w_rkv, lw)


def _mm_ln_kernel(x_ref, w_ref, h_ref, g_ref, b_ref, o_ref, *, alpha, splits):
    rs = x_ref.shape[0] // splits
    rows = [slice(n * rs, (n + 1) * rs) for n in range(splits)]
    ys = [alpha * h_ref[r, :] + _dot(x_ref[r, :], w_ref[...]) for r in rows]
    for r, y in zip(rows, ys):
        o_ref[r, :] = _layer_norm(y, g_ref[...], b_ref[...])


def _matmul_ln(x, w, h, g, b, alpha, tm=None):
    t, k = x.shape
    n = w.shape[1]
    tm = tm or _pick_tm(t, 512)
    return pl.pallas_call(
        functools.partial(_mm_ln_kernel, alpha=alpha, splits=4 if tm % 64 == 0 else 1),
        grid=(t // tm,),
        in_specs=[pl.BlockSpec((tm, k), lambda i: (i, 0)),
                  pl.BlockSpec((k, n), lambda i: (0, 0)),
                  pl.BlockSpec((tm, n), lambda i: (i, 0)),
                  pl.BlockSpec((1, n), lambda i: (0, 0)),
                  pl.BlockSpec((1, n), lambda i: (0, 0))],
        out_specs=pl.BlockSpec((tm, n), lambda i: (i, 0)),
        out_shape=jax.ShapeDtypeStruct((t, n), F32),
        name="proj_ln",
        compiler_params=_cparams(("parallel",)),
    )(x, w, h, g.reshape(1, n), b.reshape(1, n))


def _ffn_ln_kernel(*refs, alpha, has_next, head, tail):
    refs = list(refs)
    h_ref = refs.pop(0)
    meta_ref = refs.pop(0) if head else None
    w1_ref, w3_ref, w2_ref, g_ref, b_ref = refs[:5]
    nxt_refs = refs[5:8] if has_next else ()
    o_ref = refs[8 if has_next else 5]
    cast_refs = refs[9:12] if has_next else ()
    hb_ref = refs[12 if has_next else 6]
    buf_ref, sem = refs[-2:] if head or tail else (None, None)
    i = pl.program_id(0)
    j = pl.program_id(1)
    ni = pl.num_programs(0)
    tm = hb_ref.shape[0]
    slot = lax.rem(i, 2)
    acc_ref = buf_ref.at[slot] if tail else o_ref

    def tile_rows(c):
        n_meta, seq, _ = head or tail
        lo, hi = max(c * tm - n_meta, 0), min((c + 1) * tm - n_meta, seq)
        return lo, lo + n_meta - c * tm, hi - lo

    def tile_dma(tile, buf, act):
        tiles = (head or tail)[2]
        for c in range(tiles):
            lo, dst, n = tile_rows(c)

            @pl.when(lax.rem(tile, tiles) == c)
            def _():
                hbm = (h_ref if head else o_ref).at[tile // tiles, pl.ds(lo, n), :]
                vmem = buf_ref.at[buf, pl.ds(dst, n), :]
                cp = pltpu.make_async_copy(*((hbm, vmem) if head else (vmem, hbm)), sem.at[buf])
                cp.start() if act == "start" else cp.wait()

    def fill_tile():
        n_meta, _, tiles = head
        hf_ref = buf_ref
        for c in range(tiles):
            _, dst, n = tile_rows(c)

            @pl.when(lax.rem(i, tiles) == c)
            def _():
                if dst > 0:
                    hf_ref[slot, 0:dst, :] = meta_ref[n_meta - dst:, :]
                if dst + n < tm:
                    hf_ref[slot, dst + n:, :] = jnp.zeros((tm - dst - n, hf_ref.shape[2]), F32)

    def load_h(rows):
        return h_ref[rows, :] if not head else buf_ref[slot, rows, :]

    @pl.when(j == 0)
    def _():
        if head:
            @pl.when(i == 0)
            def _():
                tile_dma(i, slot, "start")

            tile_dma(i, slot, "wait")
            fill_tile()

            @pl.when(i + 1 < ni)
            def _():
                tile_dma(i + 1, 1 - slot, "start")
        if tail:
            @pl.when(i >= 2)
            def _():
                tile_dma(i - 2, slot, "wait")

        hb_ref[...] = load_h(slice(0, tm)).astype(BF16)
        acc_ref[...] = jnp.zeros(acc_ref.shape, F32)

    def cast_next_weights():
        for src, dst in zip(nxt_refs, cast_refs):
            dst[...] = src[...].astype(BF16)

    def partial_out(rows):
        hb = hb_ref[rows, :]
        u = _dot(hb, w1_ref[...])
        v = _dot(hb, w3_ref[...])
        return _dot((u * _sigmoid(u) * v).astype(BF16), w2_ref[...])

    last = pl.num_programs(1) - 1

    @pl.when(j < last)
    def _():
        cast_next_weights()
        acc_ref[...] += partial_out(slice(0, tm))

    @pl.when(j == last)
    def _():
        cast_next_weights()
        for rows in (slice(0, tm // 2), slice(tm // 2, tm)):
            y = alpha * load_h(rows) + 0.5 * (acc_ref[rows, :] + partial_out(rows))
            acc_ref[rows, :] = _layer_norm(y, g_ref[...], b_ref[...])
        if tail:
            tile_dma(i, slot, "start")

            @pl.when(i == ni - 1)
            def _():
                @pl.when(i >= 1)
                def _():
                    tile_dma(i - 1, 1 - slot, "wait")

                tile_dma(i, slot, "wait")


def _ffn_ln(h, w1, w3, w2, g, b, alpha, lp, nxt=None, meta=None, unpad=None, tm=None, tf=512):
    d = h.shape[-1]
    f = w1.shape[-1]
    batch = h.shape[0] if meta is not None else h.shape[0] // lp
    t = batch * lp
    tm = tm or _pick_tm(lp, 544)
    ni, nj = t // tm, f // tf
    head = tail = None
    scratch = [pltpu.VMEM((tm, d), BF16)]
    if meta is None:
        in_specs, args = [pl.BlockSpec((tm, d), lambda i, j: (i, 0))], [h]
    else:
        head = (meta.shape[0], h.shape[1], lp // tm)
        in_specs = [pl.BlockSpec(memory_space=pl.ANY), pl.BlockSpec(meta.shape, lambda i, j: (0, 0))]
        args = [h, meta]
    in_specs += [pl.BlockSpec((d, tf), lambda i, j: (0, j)),
                 pl.BlockSpec((d, tf), lambda i, j: (0, j)),
                 pl.BlockSpec((tf, d), lambda i, j: (j, 0)),
                 pl.BlockSpec((1, d), lambda i, j: (0, 0)),
                 pl.BlockSpec((1, d), lambda i, j: (0, 0))]
    args += [w1, w3, w2, g.reshape(1, d), b.reshape(1, d)]
    if unpad is None:
        out_specs = [pl.BlockSpec((tm, d), lambda i, j: (i, 0))]
        out_shape = [jax.ShapeDtypeStruct((t, d), F32)]
    else:
        assert meta is None
        tail = (*unpad, lp // tm)
        out_specs = [pl.BlockSpec(memory_space=pl.ANY)]
        out_shape = [jax.ShapeDtypeStruct((batch, unpad[1], d), F32)]
    if head or tail:
        scratch += [pltpu.VMEM((2, tm, d), F32), pltpu.SemaphoreType.DMA((2,))]
    if nxt is not None:
        n1, n3, n2, l, s = nxt
        dr = d // ni
        assert dr * ni == d and dr % LANES == 0
        in_specs += [pl.BlockSpec((None, None, dr, tf), lambda i, j: (l, s, i, j)),
                     pl.BlockSpec((None, None, dr, tf), lambda i, j: (l, s, i, j)),
                     pl.BlockSpec((None, None, tf, dr), lambda i, j: (l, s, j, i))]
        out_specs += [pl.BlockSpec((dr, tf), lambda i, j: (i, j)),
                      pl.BlockSpec((dr, tf), lambda i, j: (i, j)),
                      pl.BlockSpec((tf, dr), lambda i, j: (j, i))]
        out_shape += [jax.ShapeDtypeStruct((d, f), BF16), jax.ShapeDtypeStruct((d, f), BF16),
                      jax.ShapeDtypeStruct((f, d), BF16)]
        args += [n1, n3, n2]
    out = pl.pallas_call(
        functools.partial(_ffn_ln_kernel, alpha=alpha, has_next=nxt is not None, head=head, tail=tail),
        grid=(ni, nj),
        in_specs=in_specs,
        out_specs=out_specs,
        out_shape=out_shape,
        scratch_shapes=scratch,
        name="ffn_ln",
        compiler_params=_cparams(("arbitrary" if head or tail else "parallel", "arbitrary")),
    )(*args)
    return out[0], tuple(out[1:])


def _cumsum_rows(x):
    n = x.shape[0]
    row = lax.broadcasted_iota(jnp.int32, x.shape, 0)
    s = 1
    while s < n:
        x = x + jnp.where(row >= s, pltpu.roll(x, s, 0), 0.0)
        s *= 2
    return x


def _mm1(a, b):
    return _dot(a.astype(BF16), b.astype(BF16))


def _wkv_chunk(tiles, params, states, c):
    m0, same, strict, incl, ones_bd = c
    cc = tiles[0][0].shape[0]
    c2 = 2 * cc
    n = len(tiles)
    inv_n = 1.0 / RW_HEAD

    def stack2(x):
        return jnp.concatenate([jnp.where(m0, x, 0.0), jnp.where(m0, 0.0, x)], axis=0)

    def segsum(xs):
        return [_dot(x.astype(BF16), ones_bd) for x in xs]

    kk0 = [t[2] * p[0] for t, p in zip(tiles, params)]
    nrm2 = segsum([x * x for x in kk0])
    kmod = [t[2] * (1.0 + (t[4] - 1.0) * p[1]) for t, p in zip(tiles, params)]
    bon = segsum([t[0] * km * p[2] for t, km, p in zip(tiles, kmod, params)])

    lhs, rhs, vs, upd_r, dec = [], [], [], [], []
    for i in range(n):
        r, lw, k, v, a, g = tiles[i]
        kk = kk0[i] / jnp.maximum(jnp.sqrt(nrm2[i]), 1e-12)
        b = kk * a
        lcum = _cumsum_rows(lw)
        ltot = lcum[cc - 1:cc, :]
        e_neg = jnp.exp(-lcum)
        e_rem = jnp.exp(ltot - lcum)
        kt = kk * jnp.exp(lcum - lw)
        rt = r * jnp.exp(lcum)
        khb = (kmod[i] * e_neg).astype(BF16)
        bhb = (b * e_neg).astype(BF16)
        lhs.append(jnp.concatenate([stack2(kt), stack2(rt)], axis=0).astype(BF16))
        rhs.append(jnp.concatenate([khb, khb, bhb, bhb, states[i].astype(BF16)], axis=0))
        vs.append(stack2(v))
        upd_r.append(jnp.concatenate([stack2(kmod[i] * e_rem), stack2(b * e_rem)], axis=0).astype(BF16))
        dec.append(jnp.exp(ltot))

    m = [_dot_nt(a_, b_) for a_, b_ in zip(lhs, rhs)]
    a_k = [jnp.where(strict, x[:c2, :c2], 0.0) for x in m]
    a_b = [jnp.where(strict, x[:c2, c2:2 * c2], 0.0) for x in m]
    bkb = [jnp.concatenate([jnp.where(incl, x[c2:, :c2], 0.0), jnp.where(incl, -x[c2:, c2:2 * c2], 0.0)],
                           axis=1).astype(BF16) for x in m]
    rhs_u = [x[:c2, 2 * c2:] + _mm1(ak, v_) for x, ak, v_ in zip(m, a_k, vs)]
    pw = [(-ab).astype(BF16) for ab in a_b]
    cm = [pltpu.roll(ru, RW_HEAD, 1) - ab for ab, ru in zip(a_b, rhs_u)]
    lvl = 2
    while lvl < cc:
        st = [_dot(p_, c_.astype(BF16)) for p_, c_ in zip(pw, cm)]
        cm = [s_ + jnp.where(same, 0.0, c_) for s_, c_ in zip(st, cm)]
        pw = [jnp.where(same, c_, 0.0).astype(BF16) for c_ in cm]
        lvl *= 2
    ur = [jnp.where(same, 0.0, c_) for c_ in cm]
    ur = [u_ + _dot(p_, u_.astype(BF16)) for u_, p_ in zip(ur, pw)]
    us = [pltpu.roll(u_, RW_HEAD, 1) for u_ in ur]
    ys = [x[c2:, 2 * c2:] + _dot(bk, jnp.concatenate([v_, u_], axis=0).astype(BF16))
          for x, bk, v_, u_ in zip(m, bkb, vs, us)]
    y = [x[:cc] + x[cc:] for x in ys]
    upd_l = [jnp.concatenate([v_, -u_], axis=0).T.astype(BF16) for v_, u_ in zip(vs, us)]
    s_new = [s0 * d_ + _dot(ul, ur) for s0, d_, ul, ur in zip(states, dec, upd_l, upd_r)]

    mu = [x * inv_n for x in segsum(y)]
    yc = [a_ - b_ for a_, b_ in zip(y, mu)]
    var = [x * inv_n for x in segsum([x * x for x in yc])]
    outs = []
    for i in range(n):
        yn = yc[i] * lax.rsqrt(var[i] + RW_GN_EPS) * params[i][3] + params[i][4]
        outs.append((yn + bon[i] * tiles[i][3]) * tiles[i][5])
    return outs, s_new


def _wkv_kernel(r_ref, k_ref, v_ref, lora_ref, w2_ref, a2_ref, g2_ref, w0_ref, a0_ref,
                kk_ref, ka_ref, rk_ref, lg_ref, lb_ref, o_ref, s_ref, *, pairs, r1):
    @pl.when(pl.program_id(2) == 0)
    def _():
        s_ref[...] = jnp.zeros_like(s_ref)

    lo = lora_ref[...]

    cc = WKV_CHUNK
    c2 = 2 * cc
    lane = lax.broadcasted_iota(jnp.int32, (cc, LANES), 1)
    m0 = lane < RW_HEAD
    row = lax.broadcasted_iota(jnp.int32, (c2, c2), 0)
    col = lax.broadcasted_iota(jnp.int32, (c2, c2), 1)
    same = (row >= cc) == (col >= cc)
    strict = same & (col < row)
    incl = same & (col <= row)
    lr = lax.broadcasted_iota(jnp.int32, (LANES, LANES), 0)
    lc = lax.broadcasted_iota(jnp.int32, (LANES, LANES), 1)
    ones_bd = jnp.where((lr >= RW_HEAD) == (lc >= RW_HEAD), 1.0, 0.0).astype(BF16)
    consts = (m0, same, strict, incl, ones_bd)

    lw_all = -DECAY_SCALE * _sigmoid(_dot(lo[:, :r1], w2_ref[...]) + w0_ref[...])
    a_all = _sigmoid(_dot(lo[:, r1:2 * r1], a2_ref[...]) + a0_ref[...])
    g_all = _dot(lo[:, 2 * r1:], g2_ref[...])

    sls = [slice(p * LANES, (p + 1) * LANES) for p in range(pairs)]
    params = [tuple(ref[:, sl] for ref in (kk_ref, ka_ref, rk_ref, lg_ref, lb_ref)) for sl in sls]
    states = [s_ref[p] for p in range(pairs)]
    for c0 in range(0, r_ref.shape[0], cc):
        rows = slice(c0, c0 + cc)
        tiles = [(r_ref[rows, sl], lw_all[rows, sl], k_ref[rows, sl], v_ref[rows, sl], a_all[rows, sl],
                  g_all[rows, sl]) for sl in sls]
        outs, states = _wkv_chunk(tiles, params, states, consts)
        for sl, out in zip(sls, outs):
            o_ref[rows, sl] = out.astype(o_ref.dtype)
    for p in range(pairs):
        s_ref[p] = states[p]


def _wkv(rkv, lora, w2, a2, g2, w0, a0, kkp, kap, rkp, lgp, lbp, batch, r1, out_dtype=BF16):
    _, t, d = rkv.shape
    nl = lora.shape[1]
    lp = t // batch
    cc = WKV_CHUNK * (WKV_CHUNKS_PER_STEP if lp % (WKV_CHUNK * WKV_CHUNKS_PER_STEP) == 0 else 1)
    nchunk = lp // cc
    pairs = min(WKV_PAIRS_PER_STEP, d // LANES)
    wblk = pairs * LANES
    tok = pl.BlockSpec((cc, wblk), lambda b, p, c: (b * nchunk + c, p))
    r_spec, k_spec, v_spec = [pl.BlockSpec((None, cc, wblk), lambda b, p, c, n=n: (n, b * nchunk + c, p))
                              for n in range(3)]
    lo_spec = pl.BlockSpec((cc, nl), lambda b, p, c: (b * nchunk + c, 0))
    up = [pl.BlockSpec((w.shape[0], wblk), lambda b, p, c: (0, p)) for w in (w2, a2, g2)]
    par = pl.BlockSpec((1, wblk), lambda b, p, c: (0, p))
    prm = [x.reshape(1, d) for x in (w0, a0, kkp, kap, rkp, lgp, lbp)]
    return pl.pallas_call(
        functools.partial(_wkv_kernel, pairs=pairs, r1=r1),
        grid=(batch, d // wblk, nchunk),
        in_specs=[r_spec, k_spec, v_spec, lo_spec] + up + [par] * 7,
        out_specs=tok,
        out_shape=jax.ShapeDtypeStruct((t, d), out_dtype),
        scratch_shapes=[pltpu.VMEM((pairs, LANES, LANES), F32)],
        name="wkv7",
        compiler_params=_cparams(("parallel", "parallel", "arbitrary")),
    )(rkv, rkv, rkv, lora, w2, a2, g2, *prm)


def _fgate_kernel(h_ref, wh_ref, wl_ref, b_ref, o_ref, carry_ref):
    @pl.when(pl.program_id(1) == 0)
    def _():
        carry_ref[...] = jnp.zeros_like(carry_ref)

    hh, hl = _split(h_ref[...])
    z = _dot(hh, wh_ref[...]) + (_dot(hh, wl_ref[...]) + _dot(hl, wh_ref[...])) + b_ref[...]
    logf = -_softplus(-z) * LOG2E
    cs = _cumsum_rows(logf) + carry_ref[...]
    o_ref[...] = cs
    carry_ref[...] = cs[cs.shape[0] - 1:, :]


def _fgate_cumsum(h, wf, bf, batch, tm=544):
    t, d = h.shape
    lp = t // batch
    nt = lp // tm
    nh = wf.shape[1]
    wpad = jnp.zeros((d, LANES), F32).at[:, :nh].set(wf)
    bpad = jnp.zeros((1, LANES), F32).at[0, :nh].set(bf)
    wh, wl = _split(wpad)
    return pl.pallas_call(
        _fgate_kernel,
        grid=(batch, nt),
        in_specs=[pl.BlockSpec((tm, d), lambda b, i: (b * nt + i, 0)),
                  pl.BlockSpec((d, LANES), lambda b, i: (0, 0)),
                  pl.BlockSpec((d, LANES), lambda b, i: (0, 0)),
                  pl.BlockSpec((1, LANES), lambda b, i: (0, 0))],
        out_specs=pl.BlockSpec((tm, LANES), lambda b, i: (b * nt + i, 0)),
        out_shape=jax.ShapeDtypeStruct((t, LANES), F32),
        scratch_shapes=[pltpu.VMEM((1, LANES), F32)],
        name="fgate_cumsum",
        compiler_params=_cparams(("parallel", "arbitrary")),
    )(h, wh, wl, bpad)


def _fox_kernel(q_ref, k_ref, vt_ref, cc_ref, cr_ref, o_ref, *, heads, first, tile):
    lp = q_ref.shape[0]
    nbig = (lp - first) // tile
    hs = [slice(h * FX_HEAD, (h + 1) * FX_HEAD) for h in range(heads)]

    def scores(q, kstart, ksize):
        return [_dot_nt(k_ref[pl.ds(kstart, ksize), hs[h]], q[h]) for h in range(heads)]

    def update(s, cq, carry, kstart, ksize, diag):
        s = [s[h] + cq[h] - cc_ref[0, 0, pl.ds(kstart, ksize), h:h + 1] for h in range(heads)]
        if diag:
            keep = (lax.broadcasted_iota(jnp.int32, (ksize, ksize), 0)
                    <= lax.broadcasted_iota(jnp.int32, (ksize, ksize), 1))
            s = [jnp.where(keep, x, NEG_INF) for x in s]
        m_new = [jnp.maximum(carry[h][0], jnp.max(s[h], axis=0, keepdims=True)) for h in range(heads)]
        alpha = [jnp.exp2(carry[h][0] - m_new[h]) for h in range(heads)]
        p = [jnp.exp2(s[h] - m_new[h]) for h in range(heads)]
        l_new = [alpha[h] * carry[h][1] + jnp.sum(p[h], axis=0, keepdims=True) for h in range(heads)]
        pv = [_dot(vt_ref[hs[h], pl.ds(kstart, ksize)], p[h].astype(BF16)) for h in range(heads)]
        return [(m_new[h], l_new[h], alpha[h] * carry[h][2] + pv[h]) for h in range(heads)]

    def q_tile(qstart, tq, nfull, with_first):
        q = [q_ref[pl.ds(qstart, tq), hs[h]] for h in range(heads)]
        cq = [cr_ref[0, h, :, pl.ds(qstart, tq)] for h in range(heads)]
        carry = [(jnp.full((1, tq), NEG_INF, F32), jnp.zeros((1, tq), F32), jnp.zeros((FX_HEAD, tq), F32))
                 for _ in range(heads)]
        def step(c, kstart, ksize, diag):
            return update(scores(q, kstart, ksize), cq, c, kstart, ksize, diag)

        def step2(j, c):
            k0 = pl.multiple_of(first + 2 * j * tile, LANES)
            k1 = pl.multiple_of(first + (2 * j + 1) * tile, LANES)
            s0, s1 = scores(q, k0, tile), scores(q, k1, tile)
            return update(s1, cq, update(s0, cq, c, k0, tile, False), k1, tile, False)

        if nfull is not None:
            carry = lax.fori_loop(0, lax.shift_right_logical(nfull, 1), step2, carry)
            carry = lax.fori_loop(
                0, nfull & 1,
                lambda _, c: step(c, pl.multiple_of(first + (nfull - 1) * tile, LANES), tile, False), carry)
        s_first = scores(q, 0, first) if with_first else None
        s_diag = scores(q, qstart, tq)
        if with_first:
            carry = update(s_first, cq, carry, 0, first, False)
        carry = update(s_diag, cq, carry, qstart, tq, True)
        for h in range(heads):
            o_ref[pl.ds(qstart, tq), hs[h]] = (carry[h][2] / carry[h][1]).T.astype(o_ref.dtype)

    if first:
        q_tile(0, first, None, False)

    def big(i, _):
        q_tile(pl.multiple_of(first + i * tile, LANES), tile, i, first > 0)
        return 0

    lax.fori_loop(0, nbig, big, 0)


FOX_Q_SCALE = FX_HEAD ** -0.5 * LOG2E


def _fox_attention(q, k, vt, c, batch, heads=FOX_HEADS_PER_STEP, tile=256):
    t, d = q.shape
    lp = t // batch
    nh = d // FX_HEAD
    ng = nh // heads
    wblk = heads * FX_HEAD
    first = lp % tile
    c_col = jnp.transpose(c.reshape(batch, lp, ng, heads), (0, 2, 1, 3))
    c_row = jnp.transpose(c, (0, 2, 1))[:, :, None, :]
    return pl.pallas_call(
        functools.partial(_fox_kernel, heads=heads, first=first, tile=tile),
        grid=(batch, ng),
        in_specs=[pl.BlockSpec((lp, wblk), lambda b, g: (b, g)),
                  pl.BlockSpec((lp, wblk), lambda b, g: (b, g)),
                  pl.BlockSpec((wblk, lp), lambda b, g: (g, b)),
                  pl.BlockSpec((1, 1, lp, heads), lambda b, g: (b, g, 0, 0)),
                  pl.BlockSpec((1, heads, 1, lp), lambda b, g: (b, g, 0, 0))],
        out_specs=pl.BlockSpec((lp, wblk), lambda b, g: (b, g)),
        out_shape=jax.ShapeDtypeStruct((t, d), BF16),
        name="fox_attention",
        compiler_params=_cparams(("parallel", "parallel")),
    )(q, k, vt, c_col, c_row)


def _pad_cols(w, n):
    return jnp.zeros((w.shape[0], n), w.dtype).at[:, :w.shape[1]].set(w)


def _pad_rows(w, n):
    return jnp.zeros((n, w.shape[1]), w.dtype).at[:w.shape[0], :].set(w)


def _rwkv7_mix(h, batch, mu, w_rkv, w_o, w0, w1, w2, a0, a1, a2, g1, g2, k_k, k_a, r_k, lnx_g, lnx_b,
               ln_g, ln_b, alpha):
    r1 = LANES
    lw = jnp.concatenate([_pad_cols(w1, r1), _pad_cols(a1, r1), g1], axis=1).astype(BF16)
    rkv, lora = _rwkv_in(h, mu, w_rkv.astype(BF16), lw, batch, r1)
    o = _wkv(rkv, lora, _pad_rows(w2, r1).astype(BF16), _pad_rows(a2, r1).astype(BF16), g2.astype(BF16),
             w0, a0, k_k, k_a, r_k.reshape(-1), lnx_g, lnx_b, batch, r1)
    return _matmul_ln(o, w_o.astype(BF16), h, ln_g, ln_b, alpha)


def kernel(x, meta_tokens, ln_g, ln_b, ffn_w1, ffn_w3, ffn_w2, rw_mu, rw_w_rkv, rw_w_o, rw_w0, rw_w1,
           rw_w2, rw_a0, rw_a1, rw_a2, rw_g1, rw_g2, rw_k_k, rw_k_a, rw_r_k, rw_lnx_g, rw_lnx_b,
           fx_w_q, fx_w_o, fx_w_kvf, fx_b_f):
    batch, seq, d = x.shape
    depth = ln_g.shape[0]
    n_a = rw_mu.shape[0]
    alpha = (2 * depth) ** 0.25
    lp = -(-(seq + N_META) // SEQ_ALIGN) * SEQ_ALIGN
    nh = d // FX_HEAD

    wb = (ffn_w1[0, 0].astype(BF16), ffn_w3[0, 0].astype(BF16), ffn_w2[0, 0].astype(BF16))

    def ffn(h, wb, l, s):
        first = l == 0 and s == 0
        last = l == depth - 1 and s == 1
        nxt = None if last else (ffn_w1, ffn_w3, ffn_w2, l + s, 1 - s)
        return _ffn_ln(h, *wb, ln_g[l, 2 * s], ln_b[l, 2 * s], alpha, lp, nxt=nxt,
                       meta=meta_tokens.astype(x.dtype) if first else None,
                       unpad=(N_META, seq) if last else None)

    h = x
    k_s = vt_s = c = None
    for l in range(depth):
        h, wb = ffn(h, wb, l, 0)
        if l < n_a:
            h = _rwkv7_mix(h, batch, rw_mu[l], rw_w_rkv[l], rw_w_o[l], rw_w0[l], rw_w1[l], rw_w2[l],
                           rw_a0[l], rw_a1[l], rw_a2[l], rw_g1[l], rw_g2[l], rw_k_k[l], rw_k_a[l],
                           rw_r_k[l], rw_lnx_g[l], rw_lnx_b[l], ln_g[l, 1], ln_b[l, 1], alpha)
        else:
            j = l - n_a
            q = _proj(h, fx_w_q[j].astype(BF16), d, "q_proj", scale=FOX_Q_SCALE)
            o = _fox_attention(q, k_s, vt_s, c, batch)
            h = _matmul_ln(o, fx_w_o[j].astype(BF16), h, ln_g[l, 1], ln_b[l, 1], alpha)
        h, wb = ffn(h, wb, l, 1)
        if l == n_a - 1:
            w_kvf = fx_w_kvf.astype(BF16)
            k_s = _proj(h, w_kvf, d, "k_proj")
            vt_s = _proj(h, w_kvf, d, "vt_proj", woff=d, transpose_out=True, tm=512)
            c = _fgate_cumsum(h, fx_w_kvf[:, 2 * d:], fx_b_f, batch)[:, :nh].reshape(batch, lp, nh)
    return h
```
